```python
import math, functools
import jax, jax.numpy as jnp
from jax import lax
import numpy as np

D_MODEL = 1024
BATCH = 2
SEQ = 8192
DEPTH = 4
DEC_BATCH = 128
DEC_SEQ = 1
PAST_LEN = 2048
PAGE_SIZE = 128

P_DIM = 256
N_BRANCH = 4
BRANCH_W = D_MODEL // 4
N_HEADS = 4
HEAD_DIM = BRANCH_W // N_HEADS
DIFF_D = HEAD_DIM // 2
GLA_DK = HEAD_DIM // 2
GLA_LOWRANK = 16
GLA_TAU = 16.0
CONV_W = 4
ATTN_BLOCK = 128
CHUNK = 64
EPS = 1e-6
FORGET_BIAS = 3.0

SPLIT_SIZES = (
    BRANCH_W, BRANCH_W, BRANCH_W,
    BRANCH_W, BRANCH_W, BRANCH_W, N_HEADS, N_HEADS, BRANCH_W,
    N_HEADS * GLA_DK, N_HEADS * GLA_DK, BRANCH_W, GLA_LOWRANK,
    BRANCH_W, BRANCH_W, BRANCH_W,
    N_BRANCH * BRANCH_W,
)
N_IN = sum(SPLIT_SIZES)

kernel_name = 'hybrid_diffattn_mlstm_gla_stickbreak_decode_step'


def rmsnorm(x, g):
    xf = x.astype(jnp.float32)
    y = xf * lax.rsqrt(jnp.mean(xf * xf, axis=-1, keepdims=True) + EPS) * g.astype(jnp.float32)
    return y.astype(x.dtype)


def split_cols(proj):
    out, s = [], 0
    for n in SPLIT_SIZES:
        out.append(proj[..., s:s + n])
        s += n
    return out


def heads(t, d):
    return t.reshape(t.shape[:-1] + (N_HEADS, d))


def gather_pages(cache_l, page_table):
    g = cache_l[page_table]
    return g.reshape((g.shape[0], g.shape[1] * g.shape[2]) + g.shape[3:])


def blocked_attention(block_fn, q, k, v):
    B, T, H, dq = q.shape
    Tk = k.shape[1]
    qb = math.gcd(T, ATTN_BLOCK)
    nb = T // qb
    q_blocks = q.reshape(B, nb, qb, H, dq).transpose(1, 0, 2, 3, 4)
    qpos = (Tk - T + jnp.arange(T, dtype=jnp.int32)).reshape(nb, qb)
    kpos = jnp.arange(Tk, dtype=jnp.int32)
    out = lax.map(lambda a: block_fn(a[0], k, v, a[1], kpos), (q_blocks, qpos))
    return out.transpose(1, 0, 2, 3, 4).reshape(B, T, H, v.shape[-1])


def diff_block(lam, qb, k, v, qpos, kpos):
    f32 = jnp.float32
    qf, kf, vf = qb.astype(f32), k.astype(f32), v.astype(f32)
    mask = kpos[None, :] <= qpos[:, None]
    scale = DIFF_D ** -0.5

    def smax(qq, kk):
        s = jnp.einsum('bqhd,bkhd->bhqk', qq, kk) * scale
        return jax.nn.softmax(jnp.where(mask, s, -jnp.inf), axis=-1)

    a = smax(qf[..., :DIFF_D], kf[..., :DIFF_D]) - lam * smax(qf[..., DIFF_D:], kf[..., DIFF_D:])
    return jnp.einsum('bhqk,bkhv->bqhv', a, vf)


def stick_breaking_block(qb, k, v, qpos, kpos):
    f32 = jnp.float32
    z = jnp.einsum('bqhd,bkhd->bhqk', qb.astype(f32), k.astype(f32)) * HEAD_DIM ** -0.5
    mask = kpos[None, :] < qpos[:, None]
    log_fail = jnp.where(mask, jax.nn.log_sigmoid(-z), 0.0)
    later = lax.cumsum(log_fail, axis=3, reverse=True) - log_fail
    a = jnp.where(mask, jnp.exp(jax.nn.log_sigmoid(z) + later), 0.0)
    return jnp.einsum('bhqk,bkhv->bqhv', a, v.astype(f32))


def causal_conv(u, buf, w, b):
    T = u.shape[1]
    xc = jnp.concatenate([buf.astype(u.dtype), u], axis=1)
    y = b
    for j in range(CONV_W):
        y = y + xc[:, j:j + T] * w[j]
    return jax.nn.silu(y), xc[:, -(CONV_W - 1):]


def mlstm_chunked(q, k, v, ig, lf, C, n, m):
    B, T, H, _ = q.shape
    L = math.gcd(T, CHUNK)
    nc = T // L
    tril = jnp.tril(jnp.ones((L, L), dtype=bool))

    def to_chunks(t):
        return t.reshape((B, nc, L) + t.shape[2:]).swapaxes(0, 1)

    def step(carry, inp):
        C, n, m = carry
        qc, kc, vc, ic, fc = inp
        b = jnp.cumsum(fc, axis=1)
        dmat = b[:, :, None, :] - b[:, None, :, :] + ic[:, None, :, :]
        dmat = jnp.where(tril[None, :, :, None], dmat, -jnp.inf)
        inter = b + m[:, None, :]
        mt = jnp.maximum(inter, jnp.max(dmat, axis=2))
        w = jnp.einsum('bthd,bshd->btsh', qc, kc) * jnp.exp(dmat - mt[:, :, None, :])
        si = jnp.exp(inter - mt)
        num = si[..., None] * jnp.einsum('bthk,bhkv->bthv', qc, C) + jnp.einsum('btsh,bshv->bthv', w, vc)
        den = si * jnp.einsum('bthk,bhk->bth', qc, n) + jnp.sum(w, axis=2)
        h = num / jnp.maximum(jnp.abs(den), jnp.exp(-mt))[..., None]
        bl = b[:, -1]
        ws = bl[:, None, :] - b + ic
        m_new = jnp.maximum(bl + m, jnp.max(ws, axis=1))
        sc = jnp.exp(bl + m - m_new)
        e = jnp.exp(ws - m_new[:, None, :])
        C_new = sc[..., None, None] * C + jnp.einsum('bsh,bshk,bshv->bhkv', e, kc, vc)
        n_new = sc[..., None] * n + jnp.einsum('bsh,bshk->bhk', e, kc)
        return (C_new, n_new, m_new), h

    (C, n, m), hs = lax.scan(step, (C, n, m), tuple(to_chunks(t) for t in (q, k, v, ig, lf)))
    return hs.swapaxes(0, 1).reshape(B, T, H, v.shape[-1]), C, n, m


def gla_chunked(q, k, v, la, S):
    B, T, H, _ = q.shape
    L = math.gcd(T, CHUNK)
    nc = T // L
    tril = jnp.tril(jnp.ones((L, L), dtype=bool))

    def to_chunks(t):
        return t.reshape((B, nc, L) + t.shape[2:]).swapaxes(0, 1)

    def step(S, inp):
        qc, kc, vc, lc = inp
        bc = jnp.cumsum(lc, axis=1)
        inter = jnp.einsum('bthk,bhkv->bthv', qc * jnp.exp(bc), S)
        diff = bc[:, :, None] - bc[:, None, :]
        decay = jnp.exp(jnp.where(tril[None, :, :, None, None], diff, -jnp.inf))
        a = jnp.einsum('bthk,bshk,btshk->btsh', qc, kc, decay)
        o = inter + jnp.einsum('btsh,bshv->bthv', a, vc)
        bl = bc[:, -1]
        S_new = jnp.exp(bl)[..., None] * S + jnp.einsum('bshk,bshv->bhkv', kc * jnp.exp(bl[:, None] - bc), vc)
        return S_new, o

    S, os_ = lax.scan(step, S, tuple(to_chunks(t) for t in (q, k, v, la)))
    return os_.swapaxes(0, 1).reshape(B, T, H, v.shape[-1]), S


def trunk_layer(l, x, p_l, dk_past, dv_past, sk_past, sv_past, C0, n0, m0, conv0, S0, prm):
    f32 = jnp.float32
    B, T, _ = x.shape
    dt = x.dtype
    h = rmsnorm(x, prm['g_pre'])
    (dq, dk, dv, mq, mk, mv, mi, mf, mo, gq, gk, gv, ga, sq, sk, sv, z) = split_cols(h @ prm['w_in'])

    dq, dk, dv = heads(dq, HEAD_DIM), heads(dk, HEAD_DIM), heads(dv, HEAD_DIM)
    lam_init = 0.8 - 0.6 * math.exp(-0.3 * l)
    lq1, lk1, lq2, lk2 = prm['diff_lambda'].astype(f32)
    lam = jnp.exp(jnp.sum(lq1 * lk1)) - jnp.exp(jnp.sum(lq2 * lk2)) + lam_init
    ya = blocked_attention(functools.partial(diff_block, lam), dq,
                           jnp.concatenate([dk_past.astype(dt), dk], axis=1),
                           jnp.concatenate([dv_past.astype(dt), dv], axis=1))
    ya = (rmsnorm(ya, prm['diff_norm_g']) * (1.0 - lam_init)).reshape(B, T, BRANCH_W)

    u, conv_new = causal_conv(jnp.concatenate([mq, mk], axis=-1), conv0, prm['mlstm_conv_w'], prm['mlstm_conv_b'])
    qm = heads(u[..., :BRANCH_W], HEAD_DIM).astype(f32)
    km = heads(u[..., BRANCH_W:], HEAD_DIM).astype(f32) * HEAD_DIM ** -0.5
    vm = heads(mv, HEAD_DIM).astype(f32)
    b_if = prm['mlstm_b_if'].astype(f32)
    ig = mi.astype(f32) + b_if[0]
    lf = jax.nn.log_sigmoid(mf.astype(f32) + b_if[1])
    hm, C1, n1, m1 = mlstm_chunked(qm, km, vm, ig, lf, C0.astype(f32), n0.astype(f32), m0.astype(f32))
    ym = rmsnorm(hm, prm['mlstm_norm_g']).reshape(B, T, BRANCH_W) * jax.nn.sigmoid(mo.astype(f32))

    qg = heads(gq, GLA_DK).astype(f32) * GLA_DK ** -0.5
    kg = heads(gk, GLA_DK).astype(f32)
    vg = heads(gv, HEAD_DIM).astype(f32)
    la = heads(jax.nn.log_sigmoid((ga @ prm['gla_w_a2'] + prm['gla_b_a']).astype(f32)) / GLA_TAU, GLA_DK)
    og, S1 = gla_chunked(qg, kg, vg, la, S0.astype(f32))
    yg = rmsnorm(og, prm['gla_norm_g']).reshape(B, T, BRANCH_W)

    sq, sk, sv = heads(sq, HEAD_DIM), heads(sk, HEAD_DIM), heads(sv, HEAD_DIM)
    ys = blocked_attention(stick_breaking_block, sq,
                           jnp.concatenate([sk_past.astype(dt), sk], axis=1),
                           jnp.concatenate([sv_past.astype(dt), sv], axis=1)).reshape(B, T, BRANCH_W)

    branches = jnp.stack([ya, ym, yg, ys], axis=2).astype(dt) * jax.nn.silu(z.reshape(B, T, N_BRANCH, BRANCH_W))
    proj_b = jnp.einsum('btnc,ncd->btnd', branches, prm['w_branch'])
    gates = jax.nn.sigmoid(h @ prm['w_gate']).reshape(B, T, N_BRANCH, D_MODEL)
    mixed = jnp.sum(gates * proj_b, axis=2) @ prm['w_out']
    x = x + rmsnorm(mixed, prm['g_post'])
    x = x + jax.nn.sigmoid(x @ prm['w_ple_gate']) * (p_l @ prm['w_ple'])
    new_state = (dk, dv, sk, sv, C1.astype(dt), n1.astype(dt), m1.astype(dt), conv_new.astype(dt), S1.astype(dt))
    return x, new_state


def setup_inputs(seed: int = 0) -> dict:
    key = jax.random.key(seed)
    ks = jax.random.split(key, 40)
    f32 = jnp.float32
    nrm = lambda k, shape: jax.random.normal(k, shape, f32)
    n_pages = PAST_LEN // PAGE_SIZE
    n_used = DEC_BATCH * n_pages
    n_pool = (5 * n_used) // 4
    page_table = jax.random.permutation(ks[0], n_pool)[:n_used].reshape(DEC_BATCH, n_pages).astype(jnp.int32)
    cache_shape = (DEPTH, n_pool, PAGE_SIZE, N_HEADS, HEAD_DIM)
    return {
        'x_prompt': nrm(ks[1], (BATCH, SEQ, D_MODEL)),
        'x_sample': nrm(ks[2], (DEC_BATCH, DEC_SEQ, D_MODEL)),
        'cache_diff_k': nrm(ks[3], cache_shape),
        'cache_diff_v': nrm(ks[4], cache_shape),
        'cache_sb_k': nrm(ks[5], cache_shape),
        'cache_sb_v': nrm(ks[6], cache_shape),
        'state_mlstm_C': 0.3 * nrm(ks[7], (DEPTH, DEC_BATCH, N_HEADS, HEAD_DIM, HEAD_DIM)),
        'state_mlstm_n': 0.5 * jnp.abs(nrm(ks[8], (DEPTH, DEC_BATCH, N_HEADS, HEAD_DIM))),
        'state_mlstm_m': 0.5 * nrm(ks[9], (DEPTH, DEC_BATCH, N_HEADS)),
        'state_mlstm_conv': nrm(ks[10], (DEPTH, DEC_BATCH, CONV_W - 1, 2 * BRANCH_W)),
        'state_gla_S': 0.3 * nrm(ks[11], (DEPTH, DEC_BATCH, N_HEADS, GLA_DK, HEAD_DIM)),
        'page_table': page_table,
        'p_prompt': nrm(ks[12], (DEPTH, BATCH, SEQ, P_DIM)),
        'p_sample': nrm(ks[13], (DEPTH, DEC_BATCH, DEC_SEQ, P_DIM)),
        'g_pre': 1.0 + 0.05 * nrm(ks[14], (DEPTH, D_MODEL)),
        'g_post': 1.0 + 0.05 * nrm(ks[15], (DEPTH, D_MODEL)),
        'w_in': nrm(ks[16], (DEPTH, D_MODEL, N_IN)) * D_MODEL ** -0.5,
        'diff_lambda': 0.1 * nrm(ks[17], (DEPTH, 4, DIFF_D)),
        'diff_norm_g': 1.0 + 0.05 * nrm(ks[18], (DEPTH, N_HEADS, HEAD_DIM)),
        'mlstm_conv_w': nrm(ks[19], (DEPTH, CONV_W, 2 * BRANCH_W)) * CONV_W ** -0.5,
        'mlstm_conv_b': 0.02 * nrm(ks[20], (DEPTH, 2 * BRANCH_W)),
        'mlstm_b_if': jnp.stack([0.1 * nrm(ks[21], (DEPTH, N_HEADS)),
                                 FORGET_BIAS + 0.1 * nrm(ks[22], (DEPTH, N_HEADS))], axis=1),
        'mlstm_norm_g': 1.0 + 0.05 * nrm(ks[23], (DEPTH, N_HEADS, HEAD_DIM)),
        'gla_w_a2': nrm(ks[24], (DEPTH, GLA_LOWRANK, N_HEADS * GLA_DK)) * GLA_LOWRANK ** -0.5,
        'gla_b_a': 0.1 * nrm(ks[25], (DEPTH, N_HEADS * GLA_DK)),
        'gla_norm_g': 1.0 + 0.05 * nrm(ks[26], (DEPTH, N_HEADS, HEAD_DIM)),
        'w_branch': nrm(ks[27], (DEPTH, N_BRANCH, BRANCH_W, D_MODEL)) * BRANCH_W ** -0.5,
        'w_gate': nrm(ks[28], (DEPTH, D_MODEL, N_BRANCH * D_MODEL)) * D_MODEL ** -0.5,
        'w_out': nrm(ks[29], (DEPTH, D_MODEL, D_MODEL)) * D_MODEL ** -0.5,
        'w_ple': nrm(ks[30], (DEPTH, P_DIM, D_MODEL)) * P_DIM ** -0.5,
        'w_ple_gate': nrm(ks[31], (DEPTH, D_MODEL, D_MODEL)) * D_MODEL ** -0.5,
    }


def reference(x_prompt, x_sample, cache_diff_k, cache_diff_v, cache_sb_k, cache_sb_v,
              state_mlstm_C, state_mlstm_n, state_mlstm_m, state_mlstm_conv, state_gla_S,
              page_table, p_prompt, p_sample,
              g_pre, g_post, w_in, diff_lambda, diff_norm_g, mlstm_conv_w, mlstm_conv_b, mlstm_b_if,
              mlstm_norm_g, gla_w_a2, gla_b_a, gla_norm_g, w_branch, w_gate, w_out, w_ple, w_ple_gate):
    f32 = jnp.float32
    bp = x_prompt.shape[0]
    dt = x_prompt.dtype
    empty_kv = jnp.zeros((bp, 0, N_HEADS, HEAD_DIM), dt)
    C_init = jnp.zeros((bp, N_HEADS, HEAD_DIM, HEAD_DIM), f32)
    n_init = jnp.zeros((bp, N_HEADS, HEAD_DIM), f32)
    m_init = jnp.zeros((bp, N_HEADS), f32)
    conv_init = jnp.zeros((bp, CONV_W - 1, 2 * BRANCH_W), dt)
    S_init = jnp.zeros((bp, N_HEADS, GLA_DK, HEAD_DIM), f32)

    y_p, y_s = x_prompt, x_sample
    st_p, st_s = [], []
    for l in range(DEPTH):
        prm = {'g_pre': g_pre[l], 'g_post': g_post[l], 'w_in': w_in[l], 'diff_lambda': diff_lambda[l],
               'diff_norm_g': diff_norm_g[l], 'mlstm_conv_w': mlstm_conv_w[l], 'mlstm_conv_b': mlstm_conv_b[l],
               'mlstm_b_if': mlstm_b_if[l], 'mlstm_norm_g': mlstm_norm_g[l], 'gla_w_a2': gla_w_a2[l],
               'gla_b_a': gla_b_a[l], 'gla_norm_g': gla_norm_g[l], 'w_branch': w_branch[l], 'w_gate': w_gate[l],
               'w_out': w_out[l], 'w_ple': w_ple[l], 'w_ple_gate': w_ple_gate[l]}
        y_p, s_p = trunk_layer(l, y_p, p_prompt[l], empty_kv, empty_kv, empty_kv, empty_kv,
                               C_init, n_init, m_init, conv_init, S_init, prm)
        y_s, s_s = trunk_layer(l, y_s, p_sample[l],
                               gather_pages(cache_diff_k[l], page_table), gather_pages(cache_diff_v[l], page_table),
                               gather_pages(cache_sb_k[l], page_table), gather_pages(cache_sb_v[l], page_table),
                               state_mlstm_C[l], state_mlstm_n[l], state_mlstm_m[l], state_mlstm_conv[l],
                               state_gla_S[l], prm)
        st_p.append(s_p)
        st_s.append(s_s)

    (p_dk, p_dv, p_sk, p_sv, p_C, p_n, p_m, p_conv, p_S) = [jnp.stack(t) for t in zip(*st_p)]
    (s_dk, s_dv, s_sk, s_sv, s_C, s_n, s_m, s_conv, s_S) = [jnp.stack(t) for t in zip(*st_s)]
    return (y_p, y_s, p_dk, p_dv, p_sk, p_sv, p_C, p_n, p_m, p_conv, p_S,
            s_dk, s_dv, s_sk, s_sv, s_C, s_n, s_m, s_conv, s_S)
```

```python
import functools
import math

import numpy as np
import jax
import jax.numpy as jnp
from jax import lax
from jax.experimental import pallas as pl
from jax.experimental.pallas import tpu as pltpu

f32 = jnp.float32
MXU_DTYPE = jnp.bfloat16

D_MODEL = 1024
N_BRANCH = 4
BRANCH_W = 256
N_HEADS = 4
HEAD_DIM = 64
DIFF_D = 32
GLA_DK = 32
GLA_LOWRANK = 16
GLA_TAU = 16.0
CONV_W = 4
EPS = 1e-6
P_DIM = 256

LANES = 128
SUBLANES = 8
VMEM_LIMIT = 56 * 1024 * 1024

_ORIG = dict(dq=0, dk=256, dv=512, mq=768, mk=1024, mv=1280, mi=1536, mf=1540, mo=1544, gq=1800,
             gk=1928, gv=2056, ga=2312, sq=2328, sk=2584, sv=2840, z=3096, end=4120)
_PERM = np.concatenate([np.arange(0, 1536), np.arange(1544, 2312), np.arange(2328, 4120),
                        np.arange(1536, 1544), np.arange(2312, 2328)])
N_SMALL = 24
N_PROJ = 4224
C_DQ, C_DK, C_DV, C_MQK, C_MV, C_MO, C_GQK, C_GV, C_SQ, C_SK, C_SV, C_Z, C_SM = (
    0, 256, 512, 768, 1280, 1536, 1792, 2048, 2304, 2560, 2816, 3072, 4096)
SM_I, SM_F, SM_A = 0, 4, 8

ATT_TQ = 512
DIFF_TK = 256
SB_TK = 128
REC_TB = 512
REC_L = 128
GLA_R = 32
PAGES_PER_STEP = 4


def _cparams(sem):
    return pltpu.CompilerParams(dimension_semantics=sem, vmem_limit_bytes=VMEM_LIMIT)


def _mm(a, b):
    return jnp.dot(a.astype(MXU_DTYPE), b.astype(MXU_DTYPE), preferred_element_type=f32)


def _mm_nt(a, b):
    return lax.dot_general(a.astype(MXU_DTYPE), b.astype(MXU_DTYPE), (((1,), (1,)), ((), ())),
                           preferred_element_type=f32)


def _mm_tn(a, b):
    return lax.dot_general(a.astype(MXU_DTYPE), b.astype(MXU_DTYPE), (((0,), (0,)), ((), ())),
                           preferred_element_type=f32)


def _split_terms(a, n):
    if MXU_DTYPE == f32:
        return [a]
    out, r = [], a
    for i in range(n):
        p = r.astype(MXU_DTYPE)
        out.append(p)
        if i + 1 < n:
            r = r - p.astype(f32)
    return out


def _sel_mm(a, sel, n):
    acc = None
    for p in _split_terms(a, n):
        t = jnp.dot(p, sel, preferred_element_type=f32)
        acc = t if acc is None else acc + t
    return acc


def _mm_sel(sel, a, n):
    acc = None
    for p in _split_terms(a, n):
        t = jnp.dot(sel, p, preferred_element_type=f32)
        acc = t if acc is None else acc + t
    return acc


def _rms(x, g):
    return x * lax.rsqrt(jnp.mean(x * x, axis=-1, keepdims=True) + EPS) * g


def _sigmoid(x):
    return jax.nn.sigmoid(x)


def _softplus_neg_abs(x):
    return jnp.log1p(jnp.exp(-jnp.abs(x)))


def _log_sigmoid(x):
    return jnp.minimum(x, 0.0) - _softplus_neg_abs(x)


def _head_norm(y, g_ref):
    outs = []
    for h in range(N_HEADS):
        yh = y[:, h * HEAD_DIM:(h + 1) * HEAD_DIM]
        outs.append(_rms(yh, g_ref[:, h * HEAD_DIM:(h + 1) * HEAD_DIM]))
    return jnp.concatenate(outs, axis=1)


def _in_proj_body(x_ref, g_ref, w_ref, dqb, dk, dkb, dv, dvb, mqk, mv, mo, gqk, gv, sqb, sk, skb, sv, svb,
                  z, small):
    hb = _rms(x_ref[...], g_ref[...]).astype(MXU_DTYPE)

    def proj(a, b):
        return jnp.dot(hb, w_ref[:, a:b], preferred_element_type=f32)

    dqb[...] = (proj(C_DQ, C_DK) * DIFF_D ** -0.5).astype(dqb.dtype)
    t = proj(C_DK, C_DV)
    dk[...] = t
    dkb[...] = t.astype(dkb.dtype)
    t = proj(C_DV, C_MQK)
    dv[...] = t
    dvb[...] = t.astype(dvb.dtype)
    mqk[...] = proj(C_MQK, C_MV)
    mv[...] = proj(C_MV, C_MO)
    mo[...] = proj(C_MO, C_GQK)
    gqk[...] = proj(C_GQK, C_GV)
    gv[...] = proj(C_GV, C_SQ)
    sqb[...] = (proj(C_SQ, C_SK) * HEAD_DIM ** -0.5).astype(sqb.dtype)
    t = proj(C_SK, C_SV)
    sk[...] = t
    skb[...] = t.astype(skb.dtype)
    t = proj(C_SV, C_Z)
    sv[...] = t
    svb[...] = t.astype(svb.dtype)
    z[...] = proj(C_Z, C_SM)
    small[...] = proj(C_SM, N_PROJ)


def in_proj(x, g_pre, w_perm):
    R = x.shape[0]
    tm = min(R, 512)
    widths = [(256, MXU_DTYPE), (256, f32), (256, MXU_DTYPE), (256, f32), (256, MXU_DTYPE), (512, f32), (256, f32),
              (256, f32), (256, f32), (256, f32), (256, MXU_DTYPE), (256, f32), (256, MXU_DTYPE), (256, f32),
              (256, MXU_DTYPE), (1024, f32), (LANES, f32)]
    row = lambda n: pl.BlockSpec((tm, n), lambda i: (i, 0))
    const = lambda s: pl.BlockSpec(s, lambda i: (0, 0), pipeline_mode=pl.Buffered(1))
    return pl.pallas_call(
        _in_proj_body,
        grid=(R // tm,),
        in_specs=[row(D_MODEL), const((1, D_MODEL)), const((D_MODEL, N_PROJ))],
        out_specs=[row(n) for n, _ in widths],
        out_shape=[jax.ShapeDtypeStruct((R, n), d) for n, d in widths],
        compiler_params=_cparams(("arbitrary",)),
        name="in_proj",
    )(x, g_pre.reshape(1, D_MODEL), w_perm)


def _out_proj_body(x_ref, p_ref, ya, ym, yg, ys, z_ref, gpre, gpost, wg, wb, wo, wple, wpg, o_ref):
    x = x_ref[...]
    hb = _rms(x, gpre[...]).astype(MXU_DTYPE)
    acc = None
    for n, y in enumerate((ya, ym, yg, ys)):
        zz = z_ref[:, n * BRANCH_W:(n + 1) * BRANCH_W]
        br = y[...] * (zz * _sigmoid(zz))
        pb = _mm(br, wb[n])
        gt = _sigmoid(jnp.dot(hb, wg[:, n * D_MODEL:(n + 1) * D_MODEL], preferred_element_type=f32))
        acc = gt * pb if acc is None else acc + gt * pb
    x1 = x + _rms(_mm(acc, wo[...]), gpost[...])
    o_ref[...] = x1 + _sigmoid(_mm(x1, wpg[...])) * _mm(p_ref[...], wple[...])


def out_proj(x, p, ya, ym, yg, ys, z, g_pre, g_post, w_gate, w_branch, w_out, w_ple, w_ple_gate):
    R = x.shape[0]
    tm = min(R, 256)
    row = lambda n: pl.BlockSpec((tm, n), lambda i: (i, 0))
    const = lambda s: pl.BlockSpec(s, lambda i: (0,) * len(s), pipeline_mode=pl.Buffered(1))
    return pl.pallas_call(
        _out_proj_body,
        grid=(R // tm,),
        in_specs=[row(D_MODEL), row(P_DIM), row(BRANCH_W), row(BRANCH_W), row(BRANCH_W), row(BRANCH_W), row(D_MODEL),
                  const((1, D_MODEL)), const((1, D_MODEL)), const((D_MODEL, N_BRANCH * D_MODEL)),
                  const((N_BRANCH, BRANCH_W, D_MODEL)), const((D_MODEL, D_MODEL)), const((P_DIM, D_MODEL)),
                  const((D_MODEL, D_MODEL))],
        out_specs=row(D_MODEL),
        out_shape=jax.ShapeDtypeStruct((R, D_MODEL), f32),
        compiler_params=_cparams(("arbitrary",)),
        name="out_proj",
    )(x, p, ya, ym, yg, ys, z, g_pre.reshape(1, D_MODEL), g_post.reshape(1, D_MODEL), w_gate, w_branch, w_out,
      w_ple, w_ple_gate)


def _diff_lambda(dl_ref, lam_init):
    dl = dl_ref[...]
    return (jnp.exp(jnp.sum(dl[0:1] * dl[1:2], keepdims=True)) - jnp.exp(jnp.sum(dl[2:3] * dl[3:4], keepdims=True))
            + lam_init)


def _diff_masks(q):
    lane = lax.broadcasted_iota(jnp.int32, (1, BRANCH_W), 1)
    out = []
    for h in range(N_HEADS):
        for m in range(2):
            lo = h * HEAD_DIM + m * DIFF_D
            out.append(jnp.where((lane >= lo) & (lane < lo + DIFF_D), q, jnp.zeros_like(q)))
    return out


def _diff_attn_body(lam_init, q_ref, k_ref, v_ref, dl_ref, g_ref, o_ref, m_scr, l_scr, acc_scr):
    i = pl.program_id(1)
    tq, tk = ATT_TQ, DIFF_TK
    qm = _diff_masks(q_ref[0])
    m_scr[...] = jnp.full(m_scr.shape, -jnp.inf, f32)
    l_scr[...] = jnp.zeros(l_scr.shape, f32)
    acc_scr[...] = jnp.zeros(acc_scr.shape, f32)
    row = lax.broadcasted_iota(jnp.int32, (tq, tk), 0) + i * tq
    col = lax.broadcasted_iota(jnp.int32, (tq, tk), 1)

    def chunk(j, masked):
        k0 = pl.multiple_of(j * tk, tk)
        k = k_ref[0, pl.ds(k0, tk), :]
        for idx in range(2 * N_HEADS):
            h = idx // 2
            s = _mm_nt(qm[idx], k)
            if masked:
                s = jnp.where(col + k0 <= row, s, -jnp.inf)
            m_old = m_scr[idx]
            m_new = jnp.maximum(m_old, jnp.max(s, axis=1, keepdims=True))
            alpha = jnp.exp(m_old - m_new)
            p = jnp.exp(s - m_new)
            l_scr[idx] = alpha * l_scr[idx] + jnp.sum(p, axis=1, keepdims=True)
            v = v_ref[0, pl.ds(k0, tk), h * HEAD_DIM:(h + 1) * HEAD_DIM]
            acc_scr[idx] = alpha * acc_scr[idx] + _mm(p, v)
            m_scr[idx] = m_new

    n_full = i * (tq // tk)
    lax.fori_loop(0, n_full, lambda j, c: (chunk(j, False), c)[1], 0)
    for d in range(tq // tk):
        chunk(n_full + d, True)

    lam = _diff_lambda(dl_ref, lam_init)
    outs = []
    for h in range(N_HEADS):
        o1 = acc_scr[2 * h] * (1.0 / l_scr[2 * h])
        o2 = acc_scr[2 * h + 1] * (1.0 / l_scr[2 * h + 1])
        outs.append(o1 - lam * o2)
    o_ref[0] = _head_norm(jnp.concatenate(outs, axis=1), g_ref) * (1.0 - lam_init)


def diff_attn(lam_init, qb, kb, vb, diff_lambda, norm_g):
    B, T, _ = qb.shape
    tq = ATT_TQ
    return pl.pallas_call(
        functools.partial(_diff_attn_body, lam_init),
        grid=(B, T // tq),
        in_specs=[pl.BlockSpec((1, tq, BRANCH_W), lambda b, i: (b, i, 0)),
                  pl.BlockSpec((1, T, BRANCH_W), lambda b, i: (b, 0, 0)),
                  pl.BlockSpec((1, T, BRANCH_W), lambda b, i: (b, 0, 0)),
                  pl.BlockSpec((4, DIFF_D), lambda b, i: (0, 0)),
                  pl.BlockSpec((1, BRANCH_W), lambda b, i: (0, 0))],
        out_specs=pl.BlockSpec((1, tq, BRANCH_W), lambda b, i: (b, i, 0)),
        out_shape=jax.ShapeDtypeStruct((B, T, BRANCH_W), f32),
        scratch_shapes=[pltpu.VMEM((2 * N_HEADS, tq, 1), f32), pltpu.VMEM((2 * N_HEADS, tq, 1), f32),
                        pltpu.VMEM((2 * N_HEADS, tq, HEAD_DIM), f32)],
        compiler_params=_cparams(("arbitrary", "arbitrary")),
        name="diff_attn",
    )(qb, kb, vb, diff_lambda, norm_g.reshape(1, BRANCH_W))


def _head_masks(q):
    lane = lax.broadcasted_iota(jnp.int32, (1, BRANCH_W), 1)
    return [jnp.where((lane >= h * HEAD_DIM) & (lane < (h + 1) * HEAD_DIM), q, jnp.zeros_like(q))
            for h in range(N_HEADS)]


def _suffix_matrix():
    s = np.arange(LANES)[:, None]
    j = np.arange(LANES)[None, :]
    return np.concatenate([(s > j).astype(np.float32), np.ones((LANES, LANES), np.float32)], axis=1)


def _sb_group(z, c, u, valid):
    sp = _softplus_neg_abs(z)
    ls = jnp.minimum(z, 0.0) - sp
    lf = -jnp.maximum(z, 0.0) - sp
    if valid is not None:
        lf = jnp.where(valid, lf, 0.0)
    r = _sel_mm(lf, u, 2)
    a = jnp.exp(ls + r[:, :LANES] + c)
    if valid is not None:
        a = jnp.where(valid, a, 0.0)
    return a, c + r[:, LANES:]


def _sb_attn_body(q_ref, k_ref, v_ref, u_ref, o_ref, c_scr, acc_scr):
    i = pl.program_id(1)
    tq, tk = ATT_TQ, SB_TK
    qm = _head_masks(q_ref[0])
    c_scr[...] = jnp.zeros(c_scr.shape, f32)
    acc_scr[...] = jnp.zeros(acc_scr.shape, f32)
    row = lax.broadcasted_iota(jnp.int32, (tq, tk), 0) + i * tq
    col = lax.broadcasted_iota(jnp.int32, (tq, tk), 1)
    u = u_ref[...]

    def group(g, masked):
        k0 = pl.multiple_of(g * tk, tk)
        k = k_ref[0, pl.ds(k0, tk), :]
        valid = (col + k0 < row) if masked else None
        for h in range(N_HEADS):
            a, c_new = _sb_group(_mm_nt(qm[h], k), c_scr[h], u, valid)
            v = v_ref[0, pl.ds(k0, tk), h * HEAD_DIM:(h + 1) * HEAD_DIM]
            acc_scr[h] = acc_scr[h] + _mm(a, v)
            c_scr[h] = c_new

    n_diag = tq // tk
    top = (i + 1) * n_diag
    for d in range(n_diag):
        group(top - 1 - d, True)
    lax.fori_loop(0, i * n_diag, lambda n, c: (group(i * n_diag - 1 - n, False), c)[1], 0)
    o_ref[0] = jnp.concatenate([acc_scr[h] for h in range(N_HEADS)], axis=1)


def sb_attn(qb, kb, vb, u):
    B, T, _ = qb.shape
    tq = ATT_TQ
    return pl.pallas_call(
        _sb_attn_body,
        grid=(B, T // tq),
        in_specs=[pl.BlockSpec((1, tq, BRANCH_W), lambda b, i: (b, i, 0)),
                  pl.BlockSpec((1, T, BRANCH_W), lambda b, i: (b, 0, 0)),
                  pl.BlockSpec((1, T, BRANCH_W), lambda b, i: (b, 0, 0)),
                  pl.BlockSpec((LANES, 2 * LANES), lambda b, i: (0, 0))],
        out_specs=pl.BlockSpec((1, tq, BRANCH_W), lambda b, i: (b, i, 0)),
        out_shape=jax.ShapeDtypeStruct((B, T, BRANCH_W), f32),
        scratch_shapes=[pltpu.VMEM((N_HEADS, tq, LANES), f32), pltpu.VMEM((N_HEADS, tq, HEAD_DIM), f32)],
        compiler_params=_cparams(("arbitrary", "arbitrary")),
        name="sb_attn",
    )(qb, kb, vb, u)


def _conv_taps(cw_ref, cb_ref, rows):
    y = cb_ref[...]
    for j in range(CONV_W):
        y = y + rows[j] * cw_ref[j:j + 1, :]
    return y * _sigmoid(y)


def _mlstm_body(mqk_ref, mv_ref, mo_ref, sm_ref, cw_ref, cb_ref, bif_ref, g_ref, tril_ref,
                ym_ref, C_ref, n_ref, m_ref, conv_ref, xc_scr, q_scr, k_scr, C_scr, n_scr, m_scr):
    t = pl.program_id(1)
    TB, L = REC_TB, REC_L
    pad = SUBLANES

    @pl.when(t == 0)
    def _():
        xc_scr[0:pad, :] = jnp.zeros((pad, 2 * BRANCH_W), f32)
        C_scr[...] = jnp.zeros(C_scr.shape, f32)
        n_scr[...] = jnp.zeros(n_scr.shape, f32)
        m_scr[...] = jnp.zeros(m_scr.shape, f32)

    xc_scr[pad:pad + TB, :] = mqk_ref[0]
    act = _conv_taps(cw_ref, cb_ref, [xc_scr[pad - (CONV_W - 1) + j:pad - (CONV_W - 1) + j + TB, :]
                                      for j in range(CONV_W)])
    q_scr[...] = act[:, :BRANCH_W]
    k_scr[...] = act[:, BRANCH_W:] * HEAD_DIM ** -0.5
    conv_ref[0] = xc_scr[pad + TB - (CONV_W - 1):pad + TB, :]
    xc_scr[0:pad, :] = xc_scr[TB:TB + pad, :]

    tril = tril_ref[...]
    tri_mask = lax.broadcasted_iota(jnp.int32, (L, L), 1) <= lax.broadcasted_iota(jnp.int32, (L, L), 0)
    for c in range(TB // L):
        rs = slice(c * L, (c + 1) * L)
        gi = sm_ref[0, rs, :] + bif_ref[...]
        bcum = _mm_sel(tril, _log_sigmoid(gi), 3)
        bT = bcum.T
        iT = gi.T
        outs = []
        for h in range(N_HEADS):
            hs = slice(h * HEAD_DIM, (h + 1) * HEAD_DIM)
            bcol = bcum[:, SM_F + h:SM_F + h + 1]
            brow = bT[SM_F + h:SM_F + h + 1, :]
            irow = iT[SM_I + h:SM_I + h + 1, :]
            icol = gi[:, SM_I + h:SM_I + h + 1]
            m_h = m_scr[h]
            dmat = jnp.where(tri_mask, bcol - brow + irow, -jnp.inf)
            inter = bcol + m_h
            mt = jnp.maximum(inter, jnp.max(dmat, axis=1, keepdims=True))
            qh = q_scr[rs, hs]
            kh = k_scr[rs, hs]
            vh = mv_ref[0, rs, hs]
            w = _mm_nt(qh, kh) * jnp.exp(dmat - mt)
            si = jnp.exp(inter - mt)
            num = si * _mm(qh, C_scr[h]) + _mm(w, vh)
            den = si * jnp.sum(qh * n_scr[h], axis=1, keepdims=True) + jnp.sum(w, axis=1, keepdims=True)
            outs.append(num / jnp.maximum(jnp.abs(den), jnp.exp(-mt)))
            bl = bcol[L - 1:L, :]
            ws = bl - bcol + icol
            m_new = jnp.maximum(bl + m_h, jnp.max(ws, axis=0, keepdims=True))
            sc = jnp.exp(bl + m_h - m_new)
            ek = kh * jnp.exp(ws - m_new)
            C_scr[h] = sc * C_scr[h] + _mm_tn(ek, vh)
            n_scr[h] = sc * n_scr[h] + jnp.sum(ek, axis=0, keepdims=True)
            m_scr[h] = m_new
        ym_ref[0, rs, :] = _head_norm(jnp.concatenate(outs, axis=1), g_ref) * _sigmoid(mo_ref[0, rs, :])

    C_ref[0] = C_scr[...]
    n_ref[0] = jnp.concatenate([n_scr[h] for h in range(N_HEADS)], axis=1)
    m_ref[0] = jnp.concatenate([jnp.broadcast_to(m_scr[h], (1, LANES // N_HEADS)) for h in range(N_HEADS)], axis=1)


def mlstm_prompt(mqk, mv, mo, small, conv_w, conv_b, bif_row, norm_g, tril):
    B, T, _ = mqk.shape
    TB = REC_TB
    blk = lambda n: pl.BlockSpec((1, TB, n), lambda b, t: (b, t, 0))
    const = lambda s: pl.BlockSpec(s, lambda b, t: (0,) * len(s))
    return pl.pallas_call(
        _mlstm_body,
        grid=(B, T // TB),
        in_specs=[blk(2 * BRANCH_W), blk(BRANCH_W), blk(BRANCH_W), blk(LANES), const((CONV_W, 2 * BRANCH_W)),
                  const((1, 2 * BRANCH_W)), const((1, LANES)), const((1, BRANCH_W)), const((REC_L, REC_L))],
        out_specs=[blk(BRANCH_W),
                   pl.BlockSpec((1, N_HEADS, HEAD_DIM, HEAD_DIM), lambda b, t: (b, 0, 0, 0)),
                   pl.BlockSpec((1, 1, BRANCH_W), lambda b, t: (b, 0, 0)),
                   pl.BlockSpec((1, 1, LANES), lambda b, t: (b, 0, 0)),
                   pl.BlockSpec((1, CONV_W - 1, 2 * BRANCH_W), lambda b, t: (b, 0, 0))],
        out_shape=[jax.ShapeDtypeStruct((B, T, BRANCH_W), f32),
                   jax.ShapeDtypeStruct((B, N_HEADS, HEAD_DIM, HEAD_DIM), f32),
                   jax.ShapeDtypeStruct((B, 1, BRANCH_W), f32),
                   jax.ShapeDtypeStruct((B, 1, LANES), f32),
                   jax.ShapeDtypeStruct((B, CONV_W - 1, 2 * BRANCH_W), f32)],
        scratch_shapes=[pltpu.VMEM((TB + SUBLANES, 2 * BRANCH_W), f32), pltpu.VMEM((TB, BRANCH_W), f32),
                        pltpu.VMEM((TB, BRANCH_W), f32), pltpu.VMEM((N_HEADS, HEAD_DIM, HEAD_DIM), f32),
                        pltpu.VMEM((N_HEADS, 1, HEAD_DIM), f32), pltpu.VMEM((N_HEADS, 1, 1), f32)],
        compiler_params=_cparams(("arbitrary", "arbitrary")),
        name="mlstm_prompt",
    )(mqk, mv, mo, small, conv_w, conv_b.reshape(1, -1), bif_row, norm_g.reshape(1, BRANCH_W), tril)


def _gla_log_decay(sm, wa_ref, ba_ref):
    return _log_sigmoid(_mm(sm, wa_ref[...]) + ba_ref[...]) * (1.0 / GLA_TAU)


def _gla_body(gqk_ref, gv_ref, sm_ref, wa_ref, ba_ref, g_ref, lb_ref, ex_ref, yg_ref, S_ref, S_scr):
    t = pl.program_id(1)
    TB, L, R = REC_TB, REC_L, GLA_R
    NK = N_HEADS * GLA_DK

    @pl.when(t == 0)
    def _():
        S_scr[...] = jnp.zeros(S_scr.shape, f32)

    lb = lb_ref[...]
    ex = ex_ref[...]
    s_i = lax.broadcasted_iota(jnp.int32, (R, R, NK), 0)
    t_i = lax.broadcasted_iota(jnp.int32, (R, R, NK), 1)
    causal3 = t_i >= s_i
    lane_t = lax.broadcasted_iota(jnp.int32, (1, L), 1)
    bd_mask = (lax.broadcasted_iota(jnp.int32, (NK, BRANCH_W), 0) // GLA_DK
               == lax.broadcasted_iota(jnp.int32, (NK, BRANCH_W), 1) // HEAD_DIM)
    for c in range(TB // L):
        rs = slice(c * L, (c + 1) * L)
        la = _gla_log_decay(sm_ref[0, rs, :], wa_ref, ba_ref)
        bcl = _mm_sel(lb, la, 3)
        q_c = gqk_ref[0, rs, 0:NK] * GLA_DK ** -0.5
        k_c = gqk_ref[0, rs, NK:2 * NK]
        v_c = gv_ref[0, rs, :]
        bll = jnp.concatenate([jnp.broadcast_to(bcl[(I + 1) * R - 1:(I + 1) * R, :], (R, NK)) for I in range(L // R)],
                              axis=0)
        ktilT = (k_c * jnp.exp(bll - bcl)).T
        dblT = jnp.exp(bll).T
        qin = q_c * jnp.exp(bcl)
        outs = []
        for I in range(L // R):
            sl = slice(I * R, (I + 1) * R)
            S = S_scr[...]
            bb, vv = bcl[sl], v_c[sl]
            d = jnp.where(causal3, bb[None, :, :] - bb[:, None, :], -jnp.inf)
            p = q_c[sl][None, :, :] * k_c[sl][:, None, :] * jnp.exp(d)
            a = _sel_mm(p.reshape(R * R, NK), ex, 2).reshape(R, R, BRANCH_W)
            outs.append(_mm(qin[sl], S) + jnp.sum(a * vv[:, None, :], axis=0))
            in_blk = (lane_t >= I * R) & (lane_t < (I + 1) * R)
            upd = _mm(jnp.where(in_blk, ktilT, 0.0), v_c)
            S_scr[...] = dblT[:, I * R:I * R + 1] * S + jnp.where(bd_mask, upd, 0.0)
        yg_ref[0, rs, :] = _head_norm(jnp.concatenate(outs, axis=0), g_ref)

    for h in range(N_HEADS):
        S_ref[0, h] = S_scr[h * GLA_DK:(h + 1) * GLA_DK, h * HEAD_DIM:(h + 1) * HEAD_DIM]


def gla_prompt(gqk, gv, small, wa, ba, norm_g, lb, ex):
    B, T, _ = gqk.shape
    TB = REC_TB
    blk = lambda n: pl.BlockSpec((1, TB, n), lambda b, t: (b, t, 0))
    const = lambda s: pl.BlockSpec(s, lambda b, t: (0,) * len(s))
    return pl.pallas_call(
        _gla_body,
        grid=(B, T // TB),
        in_specs=[blk(BRANCH_W), blk(BRANCH_W), blk(LANES), const((LANES, LANES)), const((1, LANES)),
                  const((1, BRANCH_W)), const((REC_L, REC_L)), const((LANES, BRANCH_W))],
        out_specs=[blk(BRANCH_W), pl.BlockSpec((1, N_HEADS, GLA_DK, HEAD_DIM), lambda b, t: (b, 0, 0, 0))],
        out_shape=[jax.ShapeDtypeStruct((B, T, BRANCH_W), f32),
                   jax.ShapeDtypeStruct((B, N_HEADS, GLA_DK, HEAD_DIM), f32)],
        scratch_shapes=[pltpu.VMEM((N_HEADS * GLA_DK, BRANCH_W), f32)],
        compiler_params=_cparams(("arbitrary", "arbitrary")),
        name="gla_prompt",
    )(gqk, gv, small, wa, ba.reshape(1, LANES), norm_g.reshape(1, BRANCH_W), lb, ex)


def _dec_attn_body(layer, lam_init, n_steps, pt_ref, dq_ref, dkn_ref, dvn_ref, sq_ref, dl_ref, g_ref, u_ref, *rest):
    nps = PAGES_PER_STEP
    dk_pg, dv_pg, sk_pg, sv_pg = (rest[j * nps:(j + 1) * nps] for j in range(4))
    ya_ref, ys_ref, m_scr, l_scr, acc_scr, c_scr, sacc_scr = rest[4 * nps:]
    s_id = pl.program_id(1)
    NR = 2 * N_HEADS
    lane = lax.broadcasted_iota(jnp.int32, (NR, BRANCH_W), 1)
    rowi = lax.broadcasted_iota(jnp.int32, (NR, BRANCH_W), 0)
    lo = (rowi // 2) * HEAD_DIM + (rowi % 2) * DIFF_D
    dmask = (lane >= lo) & (lane < lo + DIFF_D)
    hmask = (rowi < N_HEADS) & (lane // HEAD_DIM == rowi)
    own = lane // HEAD_DIM == rowi // 2
    qd = jnp.where(dmask, jnp.broadcast_to(dq_ref[0].astype(f32), (NR, BRANCH_W)), 0.0).astype(MXU_DTYPE)
    qs = jnp.where(hmask, jnp.broadcast_to(sq_ref[0].astype(f32), (NR, BRANCH_W)), 0.0).astype(MXU_DTYPE)

    @pl.when(s_id == 0)
    def _():
        m_scr[...] = jnp.full(m_scr.shape, -jnp.inf, f32)
        l_scr[...] = jnp.zeros(l_scr.shape, f32)
        acc_scr[...] = jnp.zeros(acc_scr.shape, f32)
        c_scr[...] = jnp.zeros(c_scr.shape, f32)
        sacc_scr[...] = jnp.zeros(sacc_scr.shape, f32)

    def softmax_step(s, v):
        m_old = m_scr[...]
        m_new = jnp.maximum(m_old, jnp.max(s, axis=1, keepdims=True))
        alpha = jnp.exp(m_old - m_new)
        p = jnp.exp(s - m_new)
        l_scr[...] = alpha * l_scr[...] + jnp.sum(p, axis=1, keepdims=True)
        acc_scr[...] = alpha * acc_scr[...] + _mm(p, v)
        m_scr[...] = m_new

    u = u_ref[...]
    for j in range(nps):
        softmax_step(_mm_nt(qd, dk_pg[j][...]), dv_pg[j][...])
        a, c_new = _sb_group(_mm_nt(qs, sk_pg[j][...]), c_scr[...], u, None)
        sacc_scr[...] = sacc_scr[...] + _mm(a, sv_pg[j][...])
        c_scr[...] = c_new

    @pl.when(s_id == n_steps - 1)
    def _():
        kn = dkn_ref[0].astype(MXU_DTYPE).astype(f32)
        vn = dvn_ref[0].astype(MXU_DTYPE).astype(f32)
        s = jnp.sum(qd.astype(f32) * kn, axis=1, keepdims=True)
        m_old = m_scr[...]
        m_new = jnp.maximum(m_old, s)
        alpha = jnp.exp(m_old - m_new)
        p = jnp.exp(s - m_new)
        l = alpha * l_scr[...] + p
        acc = alpha * acc_scr[...] + p.astype(MXU_DTYPE).astype(f32) * vn
        lam = _diff_lambda(dl_ref, lam_init)
        coef = jnp.where(rowi % 2 == 0, 1.0, -lam)
        y = jnp.sum(jnp.where(own, coef * acc * (1.0 / l), 0.0), axis=0, keepdims=True)
        ya_ref[0] = _head_norm(y, g_ref) * (1.0 - lam_init)
        ys_ref[0] = jnp.sum(jnp.where(hmask, sacc_scr[...], 0.0), axis=0, keepdims=True)


def dec_attn(layer, lam_init, page_table, dqb, dk_new, dv_new, sqb, diff_lambda, norm_g, u,
             cache_dk, cache_dv, cache_sk, cache_sv):
    B, n_pages = page_table.shape
    nps = PAGES_PER_STEP
    n_steps = n_pages // nps
    page = cache_dk.shape[2]
    row3 = lambda a: a.reshape(B, 1, BRANCH_W)
    rspec = pl.BlockSpec((1, 1, BRANCH_W), lambda b, s, pt: (b, 0, 0))
    const = lambda shp: pl.BlockSpec(shp, lambda b, s, pt: (0,) * len(shp))

    def pg(j, reverse):
        def imap(b, s, pt):
            p = s * nps + j
            if reverse:
                p = n_pages - 1 - p
            return (layer, pt[b * n_pages + p], 0, 0)
        return pl.BlockSpec((None, None, page, BRANCH_W), imap)

    caches = [c.reshape(c.shape[0], c.shape[1], page, BRANCH_W) for c in (cache_dk, cache_dv, cache_sk, cache_sv)]
    in_specs = [rspec, rspec, rspec, rspec, const((4, DIFF_D)), const((1, BRANCH_W)), const((LANES, 2 * LANES))]
    args = [row3(dqb), row3(dk_new), row3(dv_new), row3(sqb), diff_lambda, norm_g.reshape(1, BRANCH_W), u]
    for ci, c in enumerate(caches):
        for j in range(nps):
            in_specs.append(pg(j, reverse=ci >= 2))
            args.append(c)
    NR = 2 * N_HEADS
    ya, ys = pl.pallas_call(
        functools.partial(_dec_attn_body, layer, lam_init, n_steps),
        grid_spec=pltpu.PrefetchScalarGridSpec(
            num_scalar_prefetch=1, grid=(B, n_steps), in_specs=in_specs,
            out_specs=[pl.BlockSpec((1, 1, BRANCH_W), lambda b, s, pt: (b, 0, 0))] * 2,
            scratch_shapes=[pltpu.VMEM((NR, 1), f32), pltpu.VMEM((NR, 1), f32), pltpu.VMEM((NR, BRANCH_W), f32),
                            pltpu.VMEM((NR, LANES), f32), pltpu.VMEM((NR, BRANCH_W), f32)]),
        out_shape=[jax.ShapeDtypeStruct((B, 1, BRANCH_W), f32)] * 2,
        compiler_params=_cparams(("arbitrary", "arbitrary")),
        name="dec_attn",
    )(page_table.reshape(-1), *args)
    return ya.reshape(B, BRANCH_W), ys.reshape(B, BRANCH_W)


def _expand_heads(x4, width):
    return jnp.concatenate([jnp.broadcast_to(x4[:, h:h + 1], (x4.shape[0], width)) for h in range(N_HEADS)], axis=1)


def _dec_rec_body(mqk_ref, mv_ref, mo_ref, sm_ref, gqk_ref, gv_ref, conv_ref, C_ref, n_ref, m_ref, S_ref,
                  cw_ref, cb_ref, bif_ref, mg_ref, wa_ref, ba_ref, gg_ref,
                  ym_ref, yg_ref, conv_o, C_o, n_o, m_o, S_o, CT_scr, ST_scr, numT_scr, oT_scr):
    h = pl.program_id(0)
    W = 2 * BRANCH_W
    NK = N_HEADS * GLA_DK
    u = mqk_ref[...]
    rows = [conv_ref[:, j * W:(j + 1) * W] for j in range(CONV_W - 1)] + [u]
    act = _conv_taps(cw_ref, cb_ref, rows)
    q = act[:, :BRANCH_W]
    k = act[:, BRANCH_W:] * HEAD_DIM ** -0.5
    conv_o[...] = jnp.concatenate(rows[1:], axis=1)

    gi = sm_ref[...] + bif_ref[...]
    ig = gi[:, SM_I:SM_I + N_HEADS]
    lf = _log_sigmoid(gi)[:, SM_F:SM_F + N_HEADS]
    m_old = m_ref[...]
    m_new = jnp.maximum(lf + m_old, ig)
    sc = _expand_heads(jnp.exp(lf + m_old - m_new), HEAD_DIM)
    ek = _expand_heads(jnp.exp(ig - m_new), HEAD_DIM) * k
    n_new = sc * n_ref[...] + ek
    m_o[...] = m_new
    n_o[...] = n_new

    la = _gla_log_decay(sm_ref[...], wa_ref, ba_ref)
    gq = gqk_ref[:, 0:NK] * GLA_DK ** -0.5
    gk = gqk_ref[:, NK:2 * NK]

    r64 = pl.ds(pl.multiple_of(h * HEAD_DIM, HEAD_DIM), HEAD_DIM)
    r32 = pl.ds(pl.multiple_of(h * GLA_DK, GLA_DK), GLA_DK)
    numT_scr[0] = q.T
    numT_scr[1] = ek.T
    numT_scr[2] = mv_ref[...].T
    numT_scr[3] = sc.T
    qT, ekT, vT = numT_scr[0, r64, :], numT_scr[1, r64, :], numT_scr[2, r64, :]
    scT = numT_scr[3, pl.ds(h * HEAD_DIM, 1), :]
    CT_scr[...] = C_ref[...].T
    num = jnp.zeros((HEAD_DIM, LANES), f32)
    for kk in range(HEAD_DIM):
        blk = slice(kk * HEAD_DIM, (kk + 1) * HEAD_DIM)
        new = scT * CT_scr[blk, :] + ekT[kk:kk + 1, :] * vT
        CT_scr[blk, :] = new
        num = num + qT[kk:kk + 1, :] * new
    C_o[...] = CT_scr[...].T
    numT_scr[4, r64, :] = num

    oT_scr[0, 0:NK, :] = gq.T
    oT_scr[0, NK:2 * NK, :] = gk.T
    oT_scr[1, 0:NK, :] = jnp.exp(la).T
    oT_scr[2] = gv_ref[...].T
    gqT, gkT, decT = oT_scr[0, r32, :], oT_scr[0, pl.ds(pl.multiple_of(NK + h * GLA_DK, GLA_DK), GLA_DK), :], \
        oT_scr[1, r32, :]
    gvT = oT_scr[2, r64, :]
    ST_scr[...] = S_ref[...].T
    o = jnp.zeros((HEAD_DIM, LANES), f32)
    for kk in range(GLA_DK):
        blk = slice(kk * HEAD_DIM, (kk + 1) * HEAD_DIM)
        new = decT[kk:kk + 1, :] * ST_scr[blk, :] + gkT[kk:kk + 1, :] * gvT
        ST_scr[blk, :] = new
        o = o + gqT[kk:kk + 1, :] * new
    S_o[...] = ST_scr[...].T
    oT_scr[3, r64, :] = o

    @pl.when(h == N_HEADS - 1)
    def _():
        qn = jnp.concatenate([jnp.sum((q * n_new)[:, g * HEAD_DIM:(g + 1) * HEAD_DIM], axis=1, keepdims=True)
                              for g in range(N_HEADS)], axis=1)
        den = jnp.maximum(jnp.abs(qn), jnp.exp(-m_new))
        hm = numT_scr[4].T / _expand_heads(den, HEAD_DIM)
        ym_ref[...] = _head_norm(hm, mg_ref) * _sigmoid(mo_ref[...])
        yg_ref[...] = _head_norm(oT_scr[3].T, gg_ref)


def dec_rec(mqk, mv, mo, small, gqk, gv, conv, C, n, m, S, conv_w, conv_b, bif_row, mnorm_g, wa, ba, gnorm_g):
    B = mqk.shape[0]
    CW = HEAD_DIM * HEAD_DIM
    SW = GLA_DK * HEAD_DIM
    full = lambda a: pl.BlockSpec(a.shape, lambda h: (0,) * a.ndim)
    ins = [mqk, mv, mo, small, gqk, gv, conv.reshape(B, -1), C.reshape(B, N_HEADS * CW), n.reshape(B, BRANCH_W), m,
           S.reshape(B, N_HEADS * SW), conv_w, conv_b.reshape(1, -1), bif_row, mnorm_g.reshape(1, BRANCH_W), wa,
           ba.reshape(1, LANES), gnorm_g.reshape(1, BRANCH_W)]
    in_specs = [full(a) for a in ins]
    in_specs[7] = pl.BlockSpec((B, CW), lambda h: (0, h))
    in_specs[10] = pl.BlockSpec((B, SW), lambda h: (0, h))
    out_shape = [jax.ShapeDtypeStruct((B, BRANCH_W), f32), jax.ShapeDtypeStruct((B, BRANCH_W), f32),
                 jax.ShapeDtypeStruct((B, (CONV_W - 1) * 2 * BRANCH_W), f32),
                 jax.ShapeDtypeStruct((B, N_HEADS * CW), f32), jax.ShapeDtypeStruct((B, BRANCH_W), f32),
                 jax.ShapeDtypeStruct((B, N_HEADS), f32), jax.ShapeDtypeStruct((B, N_HEADS * SW), f32)]
    out_specs = [pl.BlockSpec(s.shape, lambda h: (0, 0)) for s in out_shape]
    out_specs[3] = pl.BlockSpec((B, CW), lambda h: (0, h))
    out_specs[6] = pl.BlockSpec((B, SW), lambda h: (0, h))
    ym, yg, conv_n, C_n, n_n, m_n, S_n = pl.pallas_call(
        _dec_rec_body,
        grid=(N_HEADS,),
        in_specs=in_specs, out_specs=out_specs, out_shape=out_shape,
        scratch_shapes=[pltpu.VMEM((CW, B), f32), pltpu.VMEM((SW, B), f32), pltpu.VMEM((5, BRANCH_W, B), f32),
                        pltpu.VMEM((4, BRANCH_W, B), f32)],
        compiler_params=_cparams(("arbitrary",)),
        name="dec_rec",
    )(*ins)
    return (ym, yg, conv_n.reshape(B, CONV_W - 1, 2 * BRANCH_W), C_n.reshape(B, N_HEADS, HEAD_DIM, HEAD_DIM),
            n_n.reshape(B, N_HEADS, HEAD_DIM), m_n, S_n.reshape(B, N_HEADS, GLA_DK, HEAD_DIM))


def _constants():
    t = np.arange(REC_L)
    tril = (t[None, :] <= t[:, None]).astype(np.float32)
    lb = tril * (t[None, :] // GLA_R == t[:, None] // GLA_R)
    ex = (np.arange(N_HEADS * GLA_DK)[:, None] // GLA_DK == np.arange(BRANCH_W)[None, :] // HEAD_DIM)
    cast = lambda a: jnp.asarray(a, f32).astype(MXU_DTYPE)
    return dict(tril=cast(tril), lb=cast(lb), ex=cast(ex), u=cast(_suffix_matrix()))


def _layer_params(l, prm):
    bif = prm['mlstm_b_if'][l].astype(f32).reshape(1, 2 * N_HEADS)
    wa = jnp.zeros((LANES, LANES), f32).at[SM_A:SM_A + GLA_LOWRANK].set(prm['gla_w_a2'][l])
    return dict(
        lam_init=0.8 - 0.6 * math.exp(-0.3 * l),
        bif_row=jnp.pad(bif, ((0, 0), (0, LANES - 2 * N_HEADS))),
        wa=wa.astype(MXU_DTYPE),
    )


def _prep_weights(w_in, w_gate, w_branch, w_out, w_ple, w_ple_gate):
    w_perm = jnp.take(w_in, jnp.asarray(_PERM), axis=2)
    w_perm = jnp.pad(w_perm, ((0, 0), (0, 0), (0, N_PROJ - w_perm.shape[2])))
    c = lambda a: a.astype(MXU_DTYPE)
    return c(w_perm), c(w_gate), c(w_branch), c(w_out), c(w_ple), c(w_ple_gate)


def _prompt_layer(l, x, p_l, prm, wts, consts):
    B, T, _ = x.shape
    lp = _layer_params(l, prm)
    w_perm, w_gate, w_branch, w_out, w_ple, w_ple_gate = wts
    x2 = x.reshape(B * T, D_MODEL)
    (dqb, dk, dkb, dv, dvb, mqk, mv, mo, gqk, gv, sqb, sk, skb, sv, svb, z, small) = in_proj(
        x2, prm['g_pre'][l], w_perm[l])
    r3 = lambda a: a.reshape(B, T, a.shape[-1])
    ya = diff_attn(lp['lam_init'], r3(dqb), r3(dkb), r3(dvb), prm['diff_lambda'][l], prm['diff_norm_g'][l])
    ys = sb_attn(r3(sqb), r3(skb), r3(svb), consts['u'])
    ym, C1, n1, m1, conv1 = mlstm_prompt(r3(mqk), r3(mv), r3(mo), r3(small), prm['mlstm_conv_w'][l],
                                         prm['mlstm_conv_b'][l], lp['bif_row'], prm['mlstm_norm_g'][l],
                                         consts['tril'])
    yg, S1 = gla_prompt(r3(gqk), r3(gv), r3(small), lp['wa'], _pad_ba(prm['gla_b_a'][l]), prm['gla_norm_g'][l],
                        consts['lb'], consts['ex'])
    flat = lambda a: a.reshape(B * T, BRANCH_W)
    y = out_proj(x2, p_l.reshape(B * T, P_DIM), flat(ya), flat(ym), flat(yg), flat(ys), z, prm['g_pre'][l],
                 prm['g_post'][l], w_gate[l], w_branch[l], w_out[l], w_ple[l], w_ple_gate[l])
    hd = lambda a: a.reshape(B, T, N_HEADS, HEAD_DIM)
    state = (hd(dk), hd(dv), hd(sk), hd(sv), C1, n1.reshape(B, N_HEADS, HEAD_DIM),
             m1[:, 0, ::LANES // N_HEADS], conv1, S1)
    return y.reshape(B, T, D_MODEL), state


def _pad_ba(ba):
    return ba.astype(f32)


def _decode_layer(l, x, p_l, prm, wts, consts, caches, states, page_table):
    B = x.shape[0]
    lp = _layer_params(l, prm)
    w_perm, w_gate, w_branch, w_out, w_ple, w_ple_gate = wts
    x2 = x.reshape(B, D_MODEL)
    (dqb, dk, dkb, dv, dvb, mqk, mv, mo, gqk, gv, sqb, sk, skb, sv, svb, z, small) = in_proj(
        x2, prm['g_pre'][l], w_perm[l])
    ya, ys = dec_attn(l, lp['lam_init'], page_table, dqb, dk, dv, sqb, prm['diff_lambda'][l], prm['diff_norm_g'][l],
                      consts['u'], *caches)
    C0, n0, m0, conv0, S0 = states
    ym, yg, conv1, C1, n1, m1, S1 = dec_rec(mqk, mv, mo, small, gqk, gv, conv0[l], C0[l], n0[l], m0[l], S0[l],
                                            prm['mlstm_conv_w'][l], prm['mlstm_conv_b'][l], lp['bif_row'],
                                            prm['mlstm_norm_g'][l], lp['wa'], _pad_ba(prm['gla_b_a'][l]),
                                            prm['gla_norm_g'][l])
    y = out_proj(x2, p_l.reshape(B, P_DIM), ya, ym, yg, ys, z, prm['g_pre'][l], prm['g_post'][l], w_gate[l],
                 w_branch[l], w_out[l], w_ple[l], w_ple_gate[l])
    hd = lambda a: a.reshape(B, 1, N_HEADS, HEAD_DIM)
    state = (hd(dk), hd(dv), hd(sk), hd(sv), C1, n1, m1, conv1, S1)
    return y.reshape(B, 1, D_MODEL), state


def kernel(x_prompt, x_sample, cache_diff_k, cache_diff_v, cache_sb_k, cache_sb_v, state_mlstm_C, state_mlstm_n,
           state_mlstm_m, state_mlstm_conv, state_gla_S, page_table, p_prompt, p_sample, g_pre, g_post, w_in,
           diff_lambda, diff_norm_g, mlstm_conv_w, mlstm_conv_b, mlstm_b_if, mlstm_norm_g, gla_w_a2, gla_b_a,
           gla_norm_g, w_branch, w_gate, w_out, w_ple, w_ple_gate):
    prm = dict(g_pre=g_pre, g_post=g_post, diff_lambda=diff_lambda, diff_norm_g=diff_norm_g,
               mlstm_conv_w=mlstm_conv_w, mlstm_conv_b=mlstm_conv_b, mlstm_b_if=mlstm_b_if,
               mlstm_norm_g=mlstm_norm_g, gla_w_a2=gla_w_a2, gla_b_a=gla_b_a, gla_norm_g=gla_norm_g)
    wts = _prep_weights(w_in, w_gate, w_branch, w_out, w_ple, w_ple_gate)
    consts = _constants()
    depth = w_in.shape[0]
    caches = (cache_diff_k, cache_diff_v, cache_sb_k, cache_sb_v)
    states = (state_mlstm_C, state_mlstm_n, state_mlstm_m, state_mlstm_conv, state_gla_S)
    y_p, y_s = x_prompt, x_sample
    st_p, st_s = [], []
    for l in range(depth):
        y_p, s_p = _prompt_layer(l, y_p, p_prompt[l], prm, wts, consts)
        y_s, s_s = _decode_layer(l, y_s, p_sample[l], prm, wts, consts, caches, states, page_table)
        st_p.append(s_p)
        st_s.append(s_s)
    outs_p = [jnp.stack(t) for t in zip(*st_p)]
    outs_s = [jnp.stack(t) for t in zip(*st_s)]
    return (y_p, y_s, *outs_p, *outs_s)
```

```python
import functools
import math

import numpy as np
import jax
import jax.numpy as jnp
from jax import lax
from jax.experimental import pallas as pl
from jax.experimental.pallas import tpu as pltpu

f32 = jnp.float32
MXU_DTYPE = jnp.bfloat16

D_MODEL = 1024
N_BRANCH = 4
BRANCH_W = 256
N_HEADS = 4
HEAD_DIM = 64
DIFF_D = 32
GLA_DK = 32
GLA_LOWRANK = 16
GLA_TAU = 16.0
CONV_W = 4
EPS = 1e-6
P_DIM = 256

LANES = 128
SUBLANES = 8
VMEM_LIMIT = 56 * 1024 * 1024

_ORIG = dict(dq=0, dk=256, dv=512, mq=768, mk=1024, mv=1280, mi=1536, mf=1540, mo=1544, gq=1800,
             gk=1928, gv=2056, ga=2312, sq=2328, sk=2584, sv=2840, z=3096, end=4120)
_PERM = np.concatenate([np.arange(0, 1536), np.arange(1544, 2312), np.arange(2328, 4120),
                        np.arange(1536, 1544), np.arange(2312, 2328)])
N_SMALL = 24
N_PROJ = 4224
C_DQ, C_DK, C_DV, C_MQK, C_MV, C_MO, C_GQK, C_GV, C_SQ, C_SK, C_SV, C_Z, C_SM = (
    0, 256, 512, 768, 1280, 1536, 1792, 2048, 2304, 2560, 2816, 3072, 4096)
SM_I, SM_F, SM_A = 0, 4, 8

ATT_TQ = 512
DIFF_TK = 1024
SB_TK = 256
REC_TB = 512
REC_L = 128
GLA_R = 32


def _cparams(sem):
    return pltpu.CompilerParams(dimension_semantics=sem, vmem_limit_bytes=VMEM_LIMIT)


def _mm(a, b):
    return jnp.dot(a.astype(MXU_DTYPE), b.astype(MXU_DTYPE), preferred_element_type=f32)


def _mm_nt(a, b):
    return lax.dot_general(a.astype(MXU_DTYPE), b.astype(MXU_DTYPE), (((1,), (1,)), ((), ())),
                           preferred_element_type=f32)


def _mm_tn(a, b):
    return lax.dot_general(a.astype(MXU_DTYPE), b.astype(MXU_DTYPE), (((0,), (0,)), ((), ())),
                           preferred_element_type=f32)


def _split_terms(a, n):
    if MXU_DTYPE == f32:
        return [a]
    out, r = [], a
    for i in range(n - 1):
        top = lax.bitcast_convert_type(lax.bitcast_convert_type(r, jnp.int32) & jnp.int32(-65536), f32)
        out.append(top.astype(MXU_DTYPE))
        r = r - top
    out.append(r.astype(MXU_DTYPE))
    return out


def _sel_mm(a, sel, n):
    acc = None
    for p in _split_terms(a, n):
        t = jnp.dot(p, sel, preferred_element_type=f32)
        acc = t if acc is None else acc + t
    return acc


def _mm_sel(sel, a, n):
    acc = None
    for p in _split_terms(a, n):
        t = jnp.dot(sel, p, preferred_element_type=f32)
        acc = t if acc is None else acc + t
    return acc


def _rms(x, g):
    return x * lax.rsqrt(jnp.mean(x * x, axis=-1, keepdims=True) + EPS) * g


def _sigmoid(x):
    return jax.nn.sigmoid(x)


def _softplus_neg_abs(x):
    return jnp.log(1.0 + jnp.exp(-jnp.abs(x)))


def _log_sigmoid(x):
    return jnp.minimum(x, 0.0) - _softplus_neg_abs(x)


def _head_norm(y, g_ref):
    outs = []
    for h in range(N_HEADS):
        yh = y[:, h * HEAD_DIM:(h + 1) * HEAD_DIM]
        outs.append(_rms(yh, g_ref[:, h * HEAD_DIM:(h + 1) * HEAD_DIM]))
    return jnp.concatenate(outs, axis=1)


def _store_heads(o_ref, t, fill):
    rows = t.shape[0]
    tail = jnp.where(lax.broadcasted_iota(jnp.int32, (rows, LANES - HEAD_DIM), 1) == 0, fill, 0.0)
    for h in range(N_HEADS):
        o_ref[h] = jnp.concatenate([t[:, h * HEAD_DIM:(h + 1) * HEAD_DIM], tail], axis=1).astype(o_ref.dtype)


def _in_proj_body(x_ref, g_ref, w_ref, dqb, dk, dkb, dv, dvh, mqk, mv, mo, gqk, gv, sqb, sk, skb, sv, svh,
                  z, small):
    hb = _rms(x_ref[...], g_ref[...]).astype(MXU_DTYPE)

    def proj(a, b):
        return jnp.dot(hb, w_ref[:, a:b], preferred_element_type=f32)

    dqb[...] = (proj(C_DQ, C_DK) * DIFF_D ** -0.5).astype(dqb.dtype)
    t = proj(C_DK, C_DV)
    dk[...] = t
    dkb[...] = t.astype(dkb.dtype)
    t = proj(C_DV, C_MQK)
    dv[...] = t
    _store_heads(dvh, t, 1.0)
    mqk[...] = proj(C_MQK, C_MV)
    mv[...] = proj(C_MV, C_MO)
    mo[...] = proj(C_MO, C_GQK)
    gqk[...] = proj(C_GQK, C_GV)
    gv[...] = proj(C_GV, C_SQ)
    sqb[...] = (proj(C_SQ, C_SK) * HEAD_DIM ** -0.5).astype(sqb.dtype)
    t = proj(C_SK, C_SV)
    sk[...] = t
    skb[...] = t.astype(skb.dtype)
    t = proj(C_SV, C_Z)
    sv[...] = t
    _store_heads(svh, t, 0.0)
    z[...] = proj(C_Z, C_SM)
    small[...] = proj(C_SM, N_PROJ)


def in_proj(x, g_pre, w_perm):
    R = x.shape[0]
    tm = min(R, 512)
    HM = "head-major"
    widths = [(256, MXU_DTYPE), (256, f32), (256, MXU_DTYPE), (256, f32), (HM, MXU_DTYPE), (512, f32), (256, f32),
              (256, f32), (256, f32), (256, f32), (256, MXU_DTYPE), (256, f32), (256, MXU_DTYPE), (256, f32),
              (HM, MXU_DTYPE), (1024, f32), (LANES, f32)]
    row = lambda n: (pl.BlockSpec((N_HEADS, tm, LANES), lambda i: (0, i, 0)) if n == HM
                     else pl.BlockSpec((tm, n), lambda i: (i, 0)))
    shape = lambda n: (N_HEADS, R, LANES) if n == HM else (R, n)
    const = lambda s: pl.BlockSpec(s, lambda i: (0, 0), pipeline_mode=pl.Buffered(1))
    return pl.pallas_call(
        _in_proj_body,
        grid=(R // tm,),
        in_specs=[row(D_MODEL), const((1, D_MODEL)), const((D_MODEL, N_PROJ))],
        out_specs=[row(n) for n, _ in widths],
        out_shape=[jax.ShapeDtypeStruct(shape(n), d) for n, d in widths],
        compiler_params=_cparams(("arbitrary",)),
        name="in_proj",
    )(x, g_pre.reshape(1, D_MODEL), w_perm)


def _out_proj_body(x_ref, p_ref, ya, ym, yg, ys, z_ref, gpre, gpost, wg, wb, wo, wple, wpg, o_ref):
    x = x_ref[...]
    hb = _rms(x, gpre[...]).astype(MXU_DTYPE)
    acc = None
    for n, y in enumerate((ya, ym, yg, ys)):
        zz = z_ref[:, n * BRANCH_W:(n + 1) * BRANCH_W]
        br = y[...] * (zz * _sigmoid(zz))
        pb = _mm(br, wb[n])
        gt = _sigmoid(jnp.dot(hb, wg[:, n * D_MODEL:(n + 1) * D_MODEL], preferred_element_type=f32))
        acc = gt * pb if acc is None else acc + gt * pb
    x1 = x + _rms(_mm(acc, wo[...]), gpost[...])
    o_ref[...] = x1 + _sigmoid(_mm(x1, wpg[...])) * _mm(p_ref[...], wple[...])


def out_proj(x, p, ya, ym, yg, ys, z, g_pre, g_post, w_gate, w_branch, w_out, w_ple, w_ple_gate):
    R = x.shape[0]
    tm = min(R, 256)
    row = lambda n: pl.BlockSpec((tm, n), lambda i: (i, 0))
    const = lambda s: pl.BlockSpec(s, lambda i: (0,) * len(s), pipeline_mode=pl.Buffered(1))
    return pl.pallas_call(
        _out_proj_body,
        grid=(R // tm,),
        in_specs=[row(D_MODEL), row(P_DIM), row(BRANCH_W), row(BRANCH_W), row(BRANCH_W), row(BRANCH_W), row(D_MODEL),
                  const((1, D_MODEL)), const((1, D_MODEL)), const((D_MODEL, N_BRANCH * D_MODEL)),
                  const((N_BRANCH, BRANCH_W, D_MODEL)), const((D_MODEL, D_MODEL)), const((P_DIM, D_MODEL)),
                  const((D_MODEL, D_MODEL))],
        out_specs=row(D_MODEL),
        out_shape=jax.ShapeDtypeStruct((R, D_MODEL), f32),
        compiler_params=_cparams(("arbitrary",)),
        name="out_proj",
    )(x, p, ya, ym, yg, ys, z, g_pre.reshape(1, D_MODEL), g_post.reshape(1, D_MODEL), w_gate, w_branch, w_out,
      w_ple, w_ple_gate)


def _diff_lambda(dl_ref, lam_init):
    dl = dl_ref[...]
    return (jnp.exp(jnp.sum(dl[0:1] * dl[1:2], keepdims=True)) - jnp.exp(jnp.sum(dl[2:3] * dl[3:4], keepdims=True))
            + lam_init)


def _diff_masks(q):
    lane = lax.broadcasted_iota(jnp.int32, (1, BRANCH_W), 1)
    out = []
    for h in range(N_HEADS):
        for m in range(2):
            lo = h * HEAD_DIM + m * DIFF_D
            out.append(jnp.where((lane >= lo) & (lane < lo + DIFF_D), q, jnp.zeros_like(q)))
    return out


def _diff_attn_body(lam_init, q_ref, k_ref, v_ref, dl_ref, g_ref, o_ref, qm_scr, m_scr, acc_scr):
    i = pl.program_id(1)
    tq, big = ATT_TQ, DIFF_TK // ATT_TQ
    for idx, qq in enumerate(_diff_masks(q_ref[0])):
        qm_scr[idx] = qq
    m_scr[...] = jnp.full(m_scr.shape, -jnp.inf, f32)
    acc_scr[...] = jnp.zeros(acc_scr.shape, f32)

    def chunk(k0, width, masked):
        k = k_ref[0, pl.ds(k0, width), :]
        if masked:
            valid = (lax.broadcasted_iota(jnp.int32, (tq, width), 1) + k0
                     <= lax.broadcasted_iota(jnp.int32, (tq, width), 0) + i * tq)

        def head(h, carry):
            v = v_ref[h, pl.ds(k0, width), :]
            for m in range(2):
                idx = 2 * h + m
                s = _mm_nt(qm_scr[idx], k)
                if masked:
                    s = jnp.where(valid, s, -jnp.inf)
                m_old = m_scr[idx]
                m_new = jnp.maximum(m_old, jnp.max(s, axis=1, keepdims=True))
                acc_scr[idx] = jnp.exp(m_old - m_new) * acc_scr[idx] + _mm(jnp.exp(s - m_new), v)
                m_scr[idx] = m_new
            return carry

        lax.fori_loop(0, N_HEADS, head, 0)

    n_big = i // big
    lax.fori_loop(0, n_big, lambda j, c: (chunk(pl.multiple_of(j * DIFF_TK, DIFF_TK), DIFF_TK, False), c)[1], 0)
    for r in range(big - 1):
        @pl.when(i % big > r)
        def _():
            chunk(pl.multiple_of((n_big * big + r) * tq, tq), tq, False)
    chunk(pl.multiple_of(i * tq, tq), tq, True)

    lam = _diff_lambda(dl_ref, lam_init)
    outs = []
    for h in range(N_HEADS):
        a1, a2 = acc_scr[2 * h], acc_scr[2 * h + 1]
        o1 = a1[:, :HEAD_DIM] * (1.0 / a1[:, HEAD_DIM:HEAD_DIM + 1])
        o2 = a2[:, :HEAD_DIM] * (1.0 / a2[:, HEAD_DIM:HEAD_DIM + 1])
        outs.append(o1 - lam * o2)
    o_ref[0] = _head_norm(jnp.concatenate(outs, axis=1), g_ref) * (1.0 - lam_init)


def diff_attn(lam_init, qb, kb, vh, diff_lambda, norm_g):
    B, T, _ = qb.shape
    tq = ATT_TQ
    return pl.pallas_call(
        functools.partial(_diff_attn_body, lam_init),
        grid=(B, T // tq),
        in_specs=[pl.BlockSpec((1, tq, BRANCH_W), lambda b, i: (b, i, 0)),
                  pl.BlockSpec((1, T, BRANCH_W), lambda b, i: (b, 0, 0)),
                  pl.BlockSpec((N_HEADS, None, T, LANES), lambda b, i: (0, b, 0, 0)),
                  pl.BlockSpec((4, DIFF_D), lambda b, i: (0, 0)),
                  pl.BlockSpec((1, BRANCH_W), lambda b, i: (0, 0))],
        out_specs=pl.BlockSpec((1, tq, BRANCH_W), lambda b, i: (b, i, 0)),
        out_shape=jax.ShapeDtypeStruct((B, T, BRANCH_W), f32),
        scratch_shapes=[pltpu.VMEM((2 * N_HEADS, tq, BRANCH_W), MXU_DTYPE), pltpu.VMEM((2 * N_HEADS, tq, 1), f32),
                        pltpu.VMEM((2 * N_HEADS, tq, LANES), f32)],
        compiler_params=_cparams(("arbitrary", "arbitrary")),
        name="diff_attn",
    )(qb, kb, vh, diff_lambda, norm_g.reshape(1, BRANCH_W))


def _head_masks(q):
    lane = lax.broadcasted_iota(jnp.int32, (1, BRANCH_W), 1)
    return [jnp.where((lane >= h * HEAD_DIM) & (lane < (h + 1) * HEAD_DIM), q, jnp.zeros_like(q))
            for h in range(N_HEADS)]


def _suffix_matrix(n):
    s = np.arange(n)[:, None]
    j = np.arange(n)[None, :]
    return np.concatenate([(s > j).astype(np.float32), np.ones((n, LANES), np.float32)], axis=1)


def _sb_group(z, c, u, valid):
    n = z.shape[1]
    sp = _softplus_neg_abs(z)
    ls = jnp.minimum(z, 0.0) - sp
    lf = -jnp.maximum(z, 0.0) - sp
    if valid is not None:
        lf = jnp.where(valid, lf, 0.0)
    r = _sel_mm(lf, u, 2)
    a = jnp.exp(ls + r[:, :n] + jnp.concatenate([c] * (n // LANES), axis=1))
    if valid is not None:
        a = jnp.where(valid, a, 0.0)
    return a, c + r[:, n:]


def _sb_attn_body(q_ref, k_ref, v_ref, u_ref, o_ref, qm_scr, c_scr, acc_scr):
    i = pl.program_id(1)
    tq, tk = ATT_TQ, SB_TK
    for h, qq in enumerate(_head_masks(q_ref[0])):
        qm_scr[h] = qq
    c_scr[...] = jnp.zeros(c_scr.shape, f32)
    acc_scr[...] = jnp.zeros(acc_scr.shape, f32)
    row = lax.broadcasted_iota(jnp.int32, (tq, tk), 0) + i * tq
    col = lax.broadcasted_iota(jnp.int32, (tq, tk), 1)
    u = u_ref[...]

    def group(g, masked):
        k0 = pl.multiple_of(g * tk, tk)
        k = k_ref[0, pl.ds(k0, tk), :]
        valid = (col + k0 < row) if masked else None

        def head(h, carry):
            a, c_new = _sb_group(_mm_nt(qm_scr[h], k), c_scr[h], u, valid)
            acc_scr[h] = acc_scr[h] + _mm(a, v_ref[h, pl.ds(k0, tk), :])
            c_scr[h] = c_new
            return carry

        lax.fori_loop(0, N_HEADS, head, 0)

    n_diag = tq // tk
    for d in range(n_diag):
        group((i + 1) * n_diag - 1 - d, True)
    lax.fori_loop(0, i * n_diag, lambda n, c: (group(i * n_diag - 1 - n, False), c)[1], 0)
    o_ref[0] = jnp.concatenate([acc_scr[h][:, :HEAD_DIM] for h in range(N_HEADS)], axis=1)


def sb_attn(qb, kb, vh, u):
    B, T, _ = qb.shape
    tq = ATT_TQ
    return pl.pallas_call(
        _sb_attn_body,
        grid=(B, T // tq),
        in_specs=[pl.BlockSpec((1, tq, BRANCH_W), lambda b, i: (b, i, 0)),
                  pl.BlockSpec((1, T, BRANCH_W), lambda b, i: (b, 0, 0)),
                  pl.BlockSpec((N_HEADS, None, T, LANES), lambda b, i: (0, b, 0, 0)),
                  pl.BlockSpec((SB_TK, SB_TK + LANES), lambda b, i: (0, 0))],
        out_specs=pl.BlockSpec((1, tq, BRANCH_W), lambda b, i: (b, i, 0)),
        out_shape=jax.ShapeDtypeStruct((B, T, BRANCH_W), f32),
        scratch_shapes=[pltpu.VMEM((N_HEADS, tq, BRANCH_W), MXU_DTYPE), pltpu.VMEM((N_HEADS, tq, LANES), f32),
                        pltpu.VMEM((N_HEADS, tq, LANES), f32)],
        compiler_params=_cparams(("arbitrary", "arbitrary")),
        name="sb_attn",
    )(qb, kb, vh, u)


def _conv_taps(cw_ref, cb_ref, rows):
    y = cb_ref[...]
    for j in range(CONV_W):
        y = y + rows[j] * cw_ref[j:j + 1, :]
    return y * _sigmoid(y)


def _mlstm_body(mqk_ref, mv_ref, mo_ref, sm_ref, cw_ref, cb_ref, bif_ref, g_ref, tril_ref,
                ym_ref, C_ref, n_ref, m_ref, conv_ref, xc_scr, q_scr, k_scr, C_scr, n_scr, m_scr):
    t = pl.program_id(1)
    TB, L = REC_TB, REC_L
    pad = SUBLANES

    @pl.when(t == 0)
    def _():
        xc_scr[0:pad, :] = jnp.zeros((pad, 2 * BRANCH_W), f32)
        C_scr[...] = jnp.zeros(C_scr.shape, f32)
        n_scr[...] = jnp.zeros(n_scr.shape, f32)
        m_scr[...] = jnp.zeros(m_scr.shape, f32)

    xc_scr[pad:pad + TB, :] = mqk_ref[0]
    act = _conv_taps(cw_ref, cb_ref, [xc_scr[pad - (CONV_W - 1) + j:pad - (CONV_W - 1) + j + TB, :]
                                      for j in range(CONV_W)])
    q_scr[...] = act[:, :BRANCH_W]
    k_scr[...] = act[:, BRANCH_W:] * HEAD_DIM ** -0.5
    conv_ref[0] = xc_scr[pad + TB - (CONV_W - 1):pad + TB, :]
    xc_scr[0:pad, :] = xc_scr[TB:TB + pad, :]

    tril = tril_ref[...]
    tri_mask = lax.broadcasted_iota(jnp.int32, (L, L), 1) <= lax.broadcasted_iota(jnp.int32, (L, L), 0)
    for c in range(TB // L):
        rs = slice(c * L, (c + 1) * L)
        gi = sm_ref[0, rs, :] + bif_ref[...]
        bcum = _mm_sel(tril, _log_sigmoid(gi), 3)
        bT = bcum.T
        iT = gi.T
        outs = []
        for h in range(N_HEADS):
            hs = slice(h * HEAD_DIM, (h + 1) * HEAD_DIM)
            bcol = bcum[:, SM_F + h:SM_F + h + 1]
            brow = bT[SM_F + h:SM_F + h + 1, :]
            irow = iT[SM_I + h:SM_I + h + 1, :]
            icol = gi[:, SM_I + h:SM_I + h + 1]
            m_h = m_scr[h]
            dmat = jnp.where(tri_mask, bcol - brow + irow, -jnp.inf)
            inter = bcol + m_h
            mt = jnp.maximum(inter, jnp.max(dmat, axis=1, keepdims=True))
            qh = q_scr[rs, hs]
            kh = k_scr[rs, hs]
            vh = mv_ref[0, rs, hs]
            w = _mm_nt(qh, kh) * jnp.exp(dmat - mt)
            si = jnp.exp(inter - mt)
            num = si * _mm(qh, C_scr[h]) + _mm(w, vh)
            den = si * jnp.sum(qh * n_scr[h], axis=1, keepdims=True) + jnp.sum(w, axis=1, keepdims=True)
            outs.append(num / jnp.maximum(jnp.abs(den), jnp.exp(-mt)))
            bl = bcol[L - 1:L, :]
            ws = bl - bcol + icol
            m_new = jnp.maximum(bl + m_h, jnp.max(ws, axis=0, keepdims=True))
            sc = jnp.exp(bl + m_h - m_new)
            ek = kh * jnp.exp(ws - m_new)
            C_scr[h] = sc * C_scr[h] + _mm_tn(ek, vh)
            n_scr[h] = sc * n_scr[h] + jnp.sum(ek, axis=0, keepdims=True)
            m_scr[h] = m_new
        ym_ref[0, rs, :] = _head_norm(jnp.concatenate(outs, axis=1), g_ref) * _sigmoid(mo_ref[0, rs, :])

    C_ref[0] = C_scr[...]
    n_ref[0] = jnp.concatenate([n_scr[h] for h in range(N_HEADS)], axis=1)
    m_ref[0] = jnp.concatenate([jnp.broadcast_to(m_scr[h], (1, LANES // N_HEADS)) for h in range(N_HEADS)], axis=1)


def mlstm_prompt(mqk, mv, mo, small, conv_w, conv_b, bif_row, norm_g, tril):
    B, T, _ = mqk.shape
    TB = REC_TB
    blk = lambda n: pl.BlockSpec((1, TB, n), lambda b, t: (b, t, 0))
    const = lambda s: pl.BlockSpec(s, lambda b, t: (0,) * len(s))
    return pl.pallas_call(
        _mlstm_body,
        grid=(B, T // TB),
        in_specs=[blk(2 * BRANCH_W), blk(BRANCH_W), blk(BRANCH_W), blk(LANES), const((CONV_W, 2 * BRANCH_W)),
                  const((1, 2 * BRANCH_W)), const((1, LANES)), const((1, BRANCH_W)), const((REC_L, REC_L))],
        out_specs=[blk(BRANCH_W),
                   pl.BlockSpec((1, N_HEADS, HEAD_DIM, HEAD_DIM), lambda b, t: (b, 0, 0, 0)),
                   pl.BlockSpec((1, 1, BRANCH_W), lambda b, t: (b, 0, 0)),
                   pl.BlockSpec((1, 1, LANES), lambda b, t: (b, 0, 0)),
                   pl.BlockSpec((1, CONV_W - 1, 2 * BRANCH_W), lambda b, t: (b, 0, 0))],
        out_shape=[jax.ShapeDtypeStruct((B, T, BRANCH_W), f32),
                   jax.ShapeDtypeStruct((B, N_HEADS, HEAD_DIM, HEAD_DIM), f32),
                   jax.ShapeDtypeStruct((B, 1, BRANCH_W), f32),
                   jax.ShapeDtypeStruct((B, 1, LANES), f32),
                   jax.ShapeDtypeStruct((B, CONV_W - 1, 2 * BRANCH_W), f32)],
        scratch_shapes=[pltpu.VMEM((TB + SUBLANES, 2 * BRANCH_W), f32), pltpu.VMEM((TB, BRANCH_W), f32),
                        pltpu.VMEM((TB, BRANCH_W), f32), pltpu.VMEM((N_HEADS, HEAD_DIM, HEAD_DIM), f32),
                        pltpu.VMEM((N_HEADS, 1, HEAD_DIM), f32), pltpu.VMEM((N_HEADS, 1, 1), f32)],
        compiler_params=_cparams(("arbitrary", "arbitrary")),
        name="mlstm_prompt",
    )(mqk, mv, mo, small, conv_w, conv_b.reshape(1, -1), bif_row, norm_g.reshape(1, BRANCH_W), tril)


def _gla_log_decay(sm, wa_ref, ba_ref):
    return _log_sigmoid(_mm(sm, wa_ref[...]) + ba_ref[...]) * (1.0 / GLA_TAU)


def _gla_body(gqk_ref, gv_ref, sm_ref, wa_ref, ba_ref, g_ref, lb_ref, ex_ref, yg_ref, S_ref, S_scr):
    t = pl.program_id(1)
    TB, L, R = REC_TB, REC_L, GLA_R
    NK = N_HEADS * GLA_DK

    @pl.when(t == 0)
    def _():
        S_scr[...] = jnp.zeros(S_scr.shape, f32)

    lb = lb_ref[...]
    ex = ex_ref[...]
    s_i = lax.broadcasted_iota(jnp.int32, (R, R, NK), 0)
    t_i = lax.broadcasted_iota(jnp.int32, (R, R, NK), 1)
    causal3 = t_i >= s_i
    lane_t = lax.broadcasted_iota(jnp.int32, (1, L), 1)
    bd_mask = (lax.broadcasted_iota(jnp.int32, (NK, BRANCH_W), 0) // GLA_DK
               == lax.broadcasted_iota(jnp.int32, (NK, BRANCH_W), 1) // HEAD_DIM)
    for c in range(TB // L):
        rs = slice(c * L, (c + 1) * L)
        la = _gla_log_decay(sm_ref[0, rs, :], wa_ref, ba_ref)
        bcl = _mm_sel(lb, la, 3)
        q_c = gqk_ref[0, rs, 0:NK] * GLA_DK ** -0.5
        k_c = gqk_ref[0, rs, NK:2 * NK]
        v_c = gv_ref[0, rs, :]
        bll = jnp.concatenate([jnp.broadcast_to(bcl[(I + 1) * R - 1:(I + 1) * R, :], (R, NK)) for I in range(L // R)],
                              axis=0)
        ktilT = (k_c * jnp.exp(bll - bcl)).T
        dblT = jnp.exp(bll).T
        qin = q_c * jnp.exp(bcl)
        outs = []
        for I in range(L // R):
            sl = slice(I * R, (I + 1) * R)
            S = S_scr[...]
            bb, vv = bcl[sl], v_c[sl]
            d = jnp.where(causal3, bb[None, :, :] - bb[:, None, :], -jnp.inf)
            p = q_c[sl][None, :, :] * k_c[sl][:, None, :] * jnp.exp(d)
            a = _sel_mm(p.reshape(R * R, NK), ex, 2).reshape(R, R, BRANCH_W)
            outs.append(_mm(qin[sl], S) + jnp.sum(a * vv[:, None, :], axis=0))
            in_blk = (lane_t >= I * R) & (lane_t < (I + 1) * R)
            upd = _mm(jnp.where(in_blk, ktilT, 0.0), v_c)
            S_scr[...] = dblT[:, I * R:I * R + 1] * S + jnp.where(bd_mask, upd, 0.0)
        yg_ref[0, rs, :] = _head_norm(jnp.concatenate(outs, axis=0), g_ref)

    for h in range(N_HEADS):
        S_ref[0, h] = S_scr[h * GLA_DK:(h + 1) * GLA_DK, h * HEAD_DIM:(h + 1) * HEAD_DIM]


def gla_prompt(gqk, gv, small, wa, ba, norm_g, lb, ex):
    B, T, _ = gqk.shape
    TB = REC_TB
    blk = lambda n: pl.BlockSpec((1, TB, n), lambda b, t: (b, t, 0))
    const = lambda s: pl.BlockSpec(s, lambda b, t: (0,) * len(s))
    return pl.pallas_call(
        _gla_body,
        grid=(B, T // TB),
        in_specs=[blk(BRANCH_W), blk(BRANCH_W), blk(LANES), const((LANES, LANES)), const((1, LANES)),
                  const((1, BRANCH_W)), const((REC_L, REC_L)), const((LANES, BRANCH_W))],
        out_specs=[blk(BRANCH_W), pl.BlockSpec((1, N_HEADS, GLA_DK, HEAD_DIM), lambda b, t: (b, 0, 0, 0))],
        out_shape=[jax.ShapeDtypeStruct((B, T, BRANCH_W), f32),
                   jax.ShapeDtypeStruct((B, N_HEADS, GLA_DK, HEAD_DIM), f32)],
        scratch_shapes=[pltpu.VMEM((N_HEADS * GLA_DK, BRANCH_W), f32)],
        compiler_params=_cparams(("arbitrary", "arbitrary")),
        name="gla_prompt",
    )(gqk, gv, small, wa, ba.reshape(1, LANES), norm_g.reshape(1, BRANCH_W), lb, ex)


def _lane_replicated_column(row):
    n = row.shape[1]
    eye = lax.broadcasted_iota(jnp.int32, (n, n), 0) == lax.broadcasted_iota(jnp.int32, (n, n), 1)
    col = jnp.sum(jnp.where(eye, jnp.broadcast_to(row, (n, n)), 0.0), axis=1, keepdims=True)
    return jnp.broadcast_to(col, (n, LANES))


def _rows_per_head(x4):
    return jnp.broadcast_to(x4[:, None, :], (N_HEADS, HEAD_DIM, LANES)).reshape(BRANCH_W, LANES)


def _lanes_per_head(col4):
    sel = (lax.broadcasted_iota(jnp.int32, (N_HEADS, BRANCH_W), 1) // HEAD_DIM
           == lax.broadcasted_iota(jnp.int32, (N_HEADS, BRANCH_W), 0))
    return jnp.sum(jnp.where(sel, col4, 0.0), axis=0, keepdims=True)


def _dec_attn_body(lam_init, n_pages, pt_ref, dq_ref, dkn_ref, dvn_ref, sq_ref, dl_ref, g_ref, u_ref, *rest):
    dk_pg, dv_pg, sk_pg, sv_pg = (rest[j * n_pages:(j + 1) * n_pages] for j in range(4))
    ya_ref, ys_ref = rest[4 * n_pages:]
    lam = _diff_lambda(dl_ref, lam_init)

    q_row = dq_ref[0].astype(f32)
    qx = _lane_replicated_column(q_row)
    s1, s2 = [], []
    for p in range(n_pages):
        r = jnp.sum((dk_pg[p][...] * qx).reshape(N_HEADS, 2, DIFF_D, LANES), axis=2)
        s1.append(r[:, 0, :])
        s2.append(r[:, 1, :])
    lane = lax.broadcasted_iota(jnp.int32, (N_HEADS, BRANCH_W), 1)
    head = lax.broadcasted_iota(jnp.int32, (N_HEADS, BRANCH_W), 0)
    qk_new = jnp.broadcast_to(q_row * dkn_ref[0], (N_HEADS, BRANCH_W))
    w_pages, w_new = [], []
    for m, s in enumerate((s1, s2)):
        lo = head * HEAD_DIM + m * DIFF_D
        s_new = jnp.sum(jnp.where((lane >= lo) & (lane < lo + DIFF_D), qk_new, 0.0), axis=1, keepdims=True)
        smax = s[0]
        for p in range(1, n_pages):
            smax = jnp.maximum(smax, s[p])
        mx = jnp.maximum(jnp.max(smax, axis=1, keepdims=True), s_new)
        e = [jnp.exp(sp - mx) for sp in s]
        e_new = jnp.exp(s_new - mx)
        tot = e[0]
        for p in range(1, n_pages):
            tot = tot + e[p]
        inv = 1.0 / (jnp.sum(tot, axis=1, keepdims=True) + e_new)
        w_pages.append([ep * inv for ep in e])
        w_new.append(e_new * inv)
    acc = jnp.zeros((BRANCH_W, LANES), f32)
    for p in range(n_pages):
        acc = acc + _rows_per_head(w_pages[0][p] - lam * w_pages[1][p]) * dv_pg[p][...]
    y = jnp.sum(acc.T, axis=0, keepdims=True) + _lanes_per_head(w_new[0] - lam * w_new[1]) * dvn_ref[0]
    ya_ref[0] = _head_norm(y, g_ref) * (1.0 - lam_init)

    qx = _lane_replicated_column(sq_ref[0].astype(f32))
    z = jnp.concatenate([jnp.sum((sk_pg[p][...] * qx).reshape(N_HEADS, HEAD_DIM, LANES), axis=1)
                         for p in range(n_pages)], axis=0)
    sp = _softplus_neg_abs(z)
    ls = jnp.minimum(z, 0.0) - sp
    r = _sel_mm(-jnp.maximum(z, 0.0) - sp, u_ref[...], 2)
    base = ls + r[:, :LANES]
    carry = jnp.zeros((N_HEADS, LANES), f32)
    acc = jnp.zeros((BRANCH_W, LANES), f32)
    for p in reversed(range(n_pages)):
        rows = slice(p * N_HEADS, (p + 1) * N_HEADS)
        acc = acc + _rows_per_head(jnp.exp(base[rows] + carry)) * sv_pg[p][...]
        carry = carry + r[rows, LANES:]
    ys_ref[0] = jnp.sum(acc.T, axis=0, keepdims=True)


def _pages_as_feature_by_position(cache):
    d, n, pg, h, hd = cache.shape
    return jnp.transpose(cache, (0, 1, 3, 4, 2)).reshape(d, n, h * hd, pg)


def dec_attn(layer, lam_init, page_table, dqb, dk_new, dv_new, sqb, diff_lambda, norm_g, u, caches_t):
    B, n_pages = page_table.shape
    page = caches_t[0].shape[3]
    assert page == LANES
    row3 = lambda a: a.reshape(B, 1, BRANCH_W)
    rspec = pl.BlockSpec((1, 1, BRANCH_W), lambda b, pt: (b, 0, 0))
    const = lambda shp: pl.BlockSpec(shp, lambda b, pt: (0,) * len(shp))

    def pg(p):
        return pl.BlockSpec((None, None, BRANCH_W, page), lambda b, pt: (layer, pt[b * n_pages + p], 0, 0))

    in_specs = [rspec, rspec, rspec, rspec, const((4, DIFF_D)), const((1, BRANCH_W)), const((LANES, 2 * LANES))]
    args = [row3(dqb), row3(dk_new), row3(dv_new), row3(sqb), diff_lambda, norm_g.reshape(1, BRANCH_W), u]
    for c in caches_t:
        for p in range(n_pages):
            in_specs.append(pg(p))
            args.append(c)
    ya, ys = pl.pallas_call(
        functools.partial(_dec_attn_body, lam_init, n_pages),
        grid_spec=pltpu.PrefetchScalarGridSpec(
            num_scalar_prefetch=1, grid=(B,), in_specs=in_specs,
            out_specs=[pl.BlockSpec((1, 1, BRANCH_W), lambda b, pt: (b, 0, 0))] * 2),
        out_shape=[jax.ShapeDtypeStruct((B, 1, BRANCH_W), f32)] * 2,
        compiler_params=_cparams(("arbitrary",)),
        name="dec_attn",
    )(page_table.reshape(-1), *args)
    return ya.reshape(B, BRANCH_W), ys.reshape(B, BRANCH_W)


def _expand_heads(x4, width):
    return jnp.concatenate([jnp.broadcast_to(x4[:, h:h + 1], (x4.shape[0], width)) for h in range(N_HEADS)], axis=1)


def _dec_rec_body(mqk_ref, mv_ref, mo_ref, sm_ref, gqk_ref, gv_ref, conv_ref, C_ref, n_ref, m_ref, S_ref,
                  cw_ref, cb_ref, bif_ref, mg_ref, wa_ref, ba_ref, gg_ref,
                  ym_ref, yg_ref, conv_o, C_o, n_o, m_o, S_o, CT_scr, ST_scr, numT_scr, oT_scr):
    h = pl.program_id(0)
    W = 2 * BRANCH_W
    NK = N_HEADS * GLA_DK
    u = mqk_ref[...]
    rows = [conv_ref[:, j * W:(j + 1) * W] for j in range(CONV_W - 1)] + [u]
    act = _conv_taps(cw_ref, cb_ref, rows)
    q = act[:, :BRANCH_W]
    k = act[:, BRANCH_W:] * HEAD_DIM ** -0.5
    conv_o[...] = jnp.concatenate(rows[1:], axis=1)

    gi = sm_ref[...] + bif_ref[...]
    ig = gi[:, SM_I:SM_I + N_HEADS]
    lf = _log_sigmoid(gi)[:, SM_F:SM_F + N_HEADS]
    m_old = m_ref[...]
    m_new = jnp.maximum(lf + m_old, ig)
    sc = _expand_heads(jnp.exp(lf + m_old - m_new), HEAD_DIM)
    ek = _expand_heads(jnp.exp(ig - m_new), HEAD_DIM) * k
    n_new = sc * n_ref[...] + ek
    m_o[...] = m_new
    n_o[...] = n_new

    la = _gla_log_decay(sm_ref[...], wa_ref, ba_ref)
    gq = gqk_ref[:, 0:NK] * GLA_DK ** -0.5
    gk = gqk_ref[:, NK:2 * NK]

    r64 = pl.ds(pl.multiple_of(h * HEAD_DIM, HEAD_DIM), HEAD_DIM)
    r32 = pl.ds(pl.multiple_of(h * GLA_DK, GLA_DK), GLA_DK)
    numT_scr[0] = q.T
    numT_scr[1] = ek.T
    numT_scr[2] = mv_ref[...].T
    numT_scr[3] = sc.T
    qT, ekT, vT = numT_scr[0, r64, :], numT_scr[1, r64, :], numT_scr[2, r64, :]
    scT = numT_scr[3, pl.ds(h * HEAD_DIM, 1), :]
    CT_scr[...] = C_ref[...].T
    num = jnp.zeros((HEAD_DIM, LANES), f32)
    for kk in range(HEAD_DIM):
        blk = slice(kk * HEAD_DIM, (kk + 1) * HEAD_DIM)
        new = scT * CT_scr[blk, :] + ekT[kk:kk + 1, :] * vT
        CT_scr[blk, :] = new
        num = num + qT[kk:kk + 1, :] * new
    C_o[...] = CT_scr[...].T
    numT_scr[4, r64, :] = num

    oT_scr[0, 0:NK, :] = gq.T
    oT_scr[0, NK:2 * NK, :] = gk.T
    oT_scr[1, 0:NK, :] = jnp.exp(la).T
    oT_scr[2] = gv_ref[...].T
    gqT, gkT, decT = oT_scr[0, r32, :], oT_scr[0, pl.ds(pl.multiple_of(NK + h * GLA_DK, GLA_DK), GLA_DK), :], \
        oT_scr[1, r32, :]
    gvT = oT_scr[2, r64, :]
    ST_scr[...] = S_ref[...].T
    o = jnp.zeros((HEAD_DIM, LANES), f32)
    for kk in range(GLA_DK):
        blk = slice(kk * HEAD_DIM, (kk + 1) * HEAD_DIM)
        new = decT[kk:kk + 1, :] * ST_scr[blk, :] + gkT[kk:kk + 1, :] * gvT
        ST_scr[blk, :] = new
        o = o + gqT[kk:kk + 1, :] * new
    S_o[...] = ST_scr[...].T
    oT_scr[3, r64, :] = o

    @pl.when(h == N_HEADS - 1)
    def _():
        qn = jnp.concatenate([jnp.sum((q * n_new)[:, g * HEAD_DIM:(g + 1) * HEAD_DIM], axis=1, keepdims=True)
                              for g in range(N_HEADS)], axis=1)
        den = jnp.maximum(jnp.abs(qn), jnp.exp(-m_new))
        hm = numT_scr[4].T / _expand_heads(den, HEAD_DIM)
        ym_ref[...] = _head_norm(hm, mg_ref) * _sigmoid(mo_ref[...])
        yg_ref[...] = _head_norm(oT_scr[3].T, gg_ref)


def dec_rec(mqk, mv, mo, small, gqk, gv, conv, C, n, m, S, conv_w, conv_b, bif_row, mnorm_g, wa, ba, gnorm_g):
    B = mqk.shape[0]
    CW = HEAD_DIM * HEAD_DIM
    SW = GLA_DK * HEAD_DIM
    full = lambda a: pl.BlockSpec(a.shape, lambda h: (0,) * a.ndim)
    ins = [mqk, mv, mo, small, gqk, gv, conv.reshape(B, -1), C.reshape(B, N_HEADS * CW), n.reshape(B, BRANCH_W), m,
           S.reshape(B, N_HEADS * SW), conv_w, conv_b.reshape(1, -1), bif_row, mnorm_g.reshape(1, BRANCH_W), wa,
           ba.reshape(1, LANES), gnorm_g.reshape(1, BRANCH_W)]
    in_specs = [full(a) for a in ins]
    in_specs[7] = pl.BlockSpec((B, CW), lambda h: (0, h))
    in_specs[10] = pl.BlockSpec((B, SW), lambda h: (0, h))
    out_shape = [jax.ShapeDtypeStruct((B, BRANCH_W), f32), jax.ShapeDtypeStruct((B, BRANCH_W), f32),
                 jax.ShapeDtypeStruct((B, (CONV_W - 1) * 2 * BRANCH_W), f32),
                 jax.ShapeDtypeStruct((B, N_HEADS * CW), f32), jax.ShapeDtypeStruct((B, BRANCH_W), f32),
                 jax.ShapeDtypeStruct((B, N_HEADS), f32), jax.ShapeDtypeStruct((B, N_HEADS * SW), f32)]
    out_specs = [pl.BlockSpec(s.shape, lambda h: (0, 0)) for s in out_shape]
    out_specs[3] = pl.BlockSpec((B, CW), lambda h: (0, h))
    out_specs[6] = pl.BlockSpec((B, SW), lambda h: (0, h))
    ym, yg, conv_n, C_n, n_n, m_n, S_n = pl.pallas_call(
        _dec_rec_body,
        grid=(N_HEADS,),
        in_specs=in_specs, out_specs=out_specs, out_shape=out_shape,
        scratch_shapes=[pltpu.VMEM((CW, B), f32), pltpu.VMEM((SW, B), f32), pltpu.VMEM((5, BRANCH_W, B), f32),
                        pltpu.VMEM((4, BRANCH_W, B), f32)],
        compiler_params=_cparams(("arbitrary",)),
        name="dec_rec",
    )(*ins)
    return (ym, yg, conv_n.reshape(B, CONV_W - 1, 2 * BRANCH_W), C_n.reshape(B, N_HEADS, HEAD_DIM, HEAD_DIM),
            n_n.reshape(B, N_HEADS, HEAD_DIM), m_n, S_n.reshape(B, N_HEADS, GLA_DK, HEAD_DIM))


def _constants():
    t = np.arange(REC_L)
    tril = (t[None, :] <= t[:, None]).astype(np.float32)
    lb = tril * (t[None, :] // GLA_R == t[:, None] // GLA_R)
    ex = (np.arange(N_HEADS * GLA_DK)[:, None] // GLA_DK == np.arange(BRANCH_W)[None, :] // HEAD_DIM)
    cast = lambda a: jnp.asarray(a, f32).astype(MXU_DTYPE)
    return dict(tril=cast(tril), lb=cast(lb), ex=cast(ex), u_page=cast(_suffix_matrix(LANES)),
                u_sb=cast(_suffix_matrix(SB_TK)))


def _layer_params(l, prm):
    bif = prm['mlstm_b_if'][l].astype(f32).reshape(1, 2 * N_HEADS)
    wa = jnp.zeros((LANES, LANES), f32).at[SM_A:SM_A + GLA_LOWRANK].set(prm['gla_w_a2'][l])
    return dict(
        lam_init=0.8 - 0.6 * math.exp(-0.3 * l),
        bif_row=jnp.pad(bif, ((0, 0), (0, LANES - 2 * N_HEADS))),
        wa=wa.astype(MXU_DTYPE),
    )


def _prep_weights(w_in, w_gate, w_branch, w_out, w_ple, w_ple_gate):
    w_perm = jnp.take(w_in, jnp.asarray(_PERM), axis=2)
    w_perm = jnp.pad(w_perm, ((0, 0), (0, 0), (0, N_PROJ - w_perm.shape[2])))
    c = lambda a: a.astype(MXU_DTYPE)
    return c(w_perm), c(w_gate), c(w_branch), c(w_out), c(w_ple), c(w_ple_gate)


def _prompt_layer(l, x, p_l, prm, wts, consts):
    B, T, _ = x.shape
    lp = _layer_params(l, prm)
    w_perm, w_gate, w_branch, w_out, w_ple, w_ple_gate = wts
    x2 = x.reshape(B * T, D_MODEL)
    (dqb, dk, dkb, dv, dvh, mqk, mv, mo, gqk, gv, sqb, sk, skb, sv, svh, z, small) = in_proj(
        x2, prm['g_pre'][l], w_perm[l])
    r3 = lambda a: a.reshape(B, T, a.shape[-1])
    hm = lambda a: a.reshape(N_HEADS, B, T, LANES)
    ya = diff_attn(lp['lam_init'], r3(dqb), r3(dkb), hm(dvh), prm['diff_lambda'][l], prm['diff_norm_g'][l])
    ys = sb_attn(r3(sqb), r3(skb), hm(svh), consts['u_sb'])
    ym, C1, n1, m1, conv1 = mlstm_prompt(r3(mqk), r3(mv), r3(mo), r3(small), prm['mlstm_conv_w'][l],
                                         prm['mlstm_conv_b'][l], lp['bif_row'], prm['mlstm_norm_g'][l],
                                         consts['tril'])
    yg, S1 = gla_prompt(r3(gqk), r3(gv), r3(small), lp['wa'], _pad_ba(prm['gla_b_a'][l]), prm['gla_norm_g'][l],
                        consts['lb'], consts['ex'])
    flat = lambda a: a.reshape(B * T, BRANCH_W)
    y = out_proj(x2, p_l.reshape(B * T, P_DIM), flat(ya), flat(ym), flat(yg), flat(ys), z, prm['g_pre'][l],
                 prm['g_post'][l], w_gate[l], w_branch[l], w_out[l], w_ple[l], w_ple_gate[l])
    hd = lambda a: a.reshape(B, T, N_HEADS, HEAD_DIM)
    state = (hd(dk), hd(dv), hd(sk), hd(sv), C1, n1.reshape(B, N_HEADS, HEAD_DIM),
             m1[:, 0, ::LANES // N_HEADS], conv1, S1)
    return y.reshape(B, T, D_MODEL), state


def _pad_ba(ba):
    return ba.astype(f32)


def _decode_layer(l, x, p_l, prm, wts, consts, caches, states, page_table):
    B = x.shape[0]
    lp = _layer_params(l, prm)
    w_perm, w_gate, w_branch, w_out, w_ple, w_ple_gate = wts
    x2 = x.reshape(B, D_MODEL)
    (dqb, dk, dkb, dv, dvh, mqk, mv, mo, gqk, gv, sqb, sk, skb, sv, svh, z, small) = in_proj(
        x2, prm['g_pre'][l], w_perm[l])
    ya, ys = dec_attn(l, lp['lam_init'], page_table, dqb, dk, dv, sqb, prm['diff_lambda'][l], prm['diff_norm_g'][l],
                      consts['u_page'], caches)
    C0, n0, m0, conv0, S0 = states
    ym, yg, conv1, C1, n1, m1, S1 = dec_rec(mqk, mv, mo, small, gqk, gv, conv0[l], C0[l], n0[l], m0[l], S0[l],
                                            prm['mlstm_conv_w'][l], prm['mlstm_conv_b'][l], lp['bif_row'],
                                            prm['mlstm_norm_g'][l], lp['wa'], _pad_ba(prm['gla_b_a'][l]),
                                            prm['gla_norm_g'][l])
    y = out_proj(x2, p_l.reshape(B, P_DIM), ya, ym, yg, ys, z, prm['g_pre'][l], prm['g_post'][l], w_gate[l],
                 w_branch[l], w_out[l], w_ple[l], w_ple_gate[l])
    hd = lambda a: a.reshape(B, 1, N_HEADS, HEAD_DIM)
    state = (hd(dk), hd(dv), hd(sk), hd(sv), C1, n1, m1, conv1, S1)
    return y.reshape(B, 1, D_MODEL), state


def kernel(x_prompt, x_sample, cache_diff_k, cache_diff_v, cache_sb_k, cache_sb_v, state_mlstm_C, state_mlstm_n,
           state_mlstm_m, state_mlstm_conv, state_gla_S, page_table, p_prompt, p_sample, g_pre, g_post, w_in,
           diff_lambda, diff_norm_g, mlstm_conv_w, mlstm_conv_b, mlstm_b_if, mlstm_norm_g, gla_w_a2, gla_b_a,
           gla_norm_g, w_branch, w_gate, w_out, w_ple, w_ple_gate):
    prm = dict(g_pre=g_pre, g_post=g_post, diff_lambda=diff_lambda, diff_norm_g=diff_norm_g,
               mlstm_conv_w=mlstm_conv_w, mlstm_conv_b=mlstm_conv_b, mlstm_b_if=mlstm_b_if,
               mlstm_norm_g=mlstm_norm_g, gla_w_a2=gla_w_a2, gla_b_a=gla_b_a, gla_norm_g=gla_norm_g)
    wts = _prep_weights(w_in, w_gate, w_branch, w_out, w_ple, w_ple_gate)
    consts = _constants()
    depth = w_in.shape[0]
    caches = tuple(_pages_as_feature_by_position(c) for c in (cache_diff_k, cache_diff_v, cache_sb_k, cache_sb_v))
    states = (state_mlstm_C, state_mlstm_n, state_mlstm_m, state_mlstm_conv, state_gla_S)
    y_p, y_s = x_prompt, x_sample
    st_p, st_s = [], []
    for l in range(depth):
        y_p, s_p = _prompt_layer(l, y_p, p_prompt[l], prm, wts, consts)
        y_s, s_s = _decode_layer(l, y_s, p_sample[l], prm, wts, consts, caches, states, page_table)
        st_p.append(s_p)
        st_s.append(s_s)
    outs_p = [jnp.stack(t) for t in zip(*st_p)]
    outs_s = [jnp.stack(t) for t in zip(*st_s)]
    return (y_p, y_s, *outs_p, *outs_s)
```

```python
import functools
import math

import numpy as np
import jax
import jax.numpy as jnp
from jax import lax
from jax.experimental import pallas as pl
from jax.experimental.pallas import tpu as pltpu

f32 = jnp.float32
MXU_DTYPE = jnp.bfloat16

D_MODEL = 1024
N_BRANCH = 4
BRANCH_W = 256
N_HEADS = 4
HEAD_DIM = 64
DIFF_D = 32
GLA_DK = 32
GLA_LOWRANK = 16
GLA_TAU = 16.0
CONV_W = 4
EPS = 1e-6
P_DIM = 256

LANES = 128
SUBLANES = 8
VMEM_LIMIT = 56 * 1024 * 1024

_ORIG = dict(dq=0, dk=256, dv=512, mq=768, mk=1024, mv=1280, mi=1536, mf=1540, mo=1544, gq=1800,
             gk=1928, gv=2056, ga=2312, sq=2328, sk=2584, sv=2840, z=3096, end=4120)
N_PROJ = 4224
C_DQ, C_DK, C_DV, C_MQK, C_MV, C_MO, C_GQK, C_GV, C_SQ, C_SK, C_SV, C_Z, C_SM = (
    0, 256, 512, 768, 1280, 1536, 1792, 2048, 2304, 2560, 2816, 3072, 4096)
SM_I, SM_F, SM_A = 0, 4, 8

ATT_TQ = 512
DIFF_TK = 1024
SB_TK = 256
REC_TB = 512
MLSTM_TB = 256
REC_L = 128
GLA_R = 32


def _cparams(sem):
    return pltpu.CompilerParams(dimension_semantics=sem, vmem_limit_bytes=VMEM_LIMIT)


def _mm(a, b):
    return jnp.dot(a.astype(MXU_DTYPE), b.astype(MXU_DTYPE), preferred_element_type=f32)


def _mm_nt(a, b):
    return lax.dot_general(a.astype(MXU_DTYPE), b.astype(MXU_DTYPE), (((1,), (1,)), ((), ())),
                           preferred_element_type=f32)


def _mm_tn(a, b):
    return lax.dot_general(a.astype(MXU_DTYPE), b.astype(MXU_DTYPE), (((0,), (0,)), ((), ())),
                           preferred_element_type=f32)


def _split_terms(a, n):
    if MXU_DTYPE == f32:
        return [a]
    out, r = [], a
    for i in range(n - 1):
        top = lax.bitcast_convert_type(lax.bitcast_convert_type(r, jnp.int32) & jnp.int32(-65536), f32)
        out.append(top.astype(MXU_DTYPE))
        r = r - top
    out.append(r.astype(MXU_DTYPE))
    return out


def _sel_mm(a, sel, n):
    acc = None
    for p in _split_terms(a, n):
        t = jnp.dot(p, sel, preferred_element_type=f32)
        acc = t if acc is None else acc + t
    return acc


def _mm_sel(sel, a, n):
    acc = None
    for p in _split_terms(a, n):
        t = jnp.dot(sel, p, preferred_element_type=f32)
        acc = t if acc is None else acc + t
    return acc


def _rms(x, g):
    return x * lax.rsqrt(jnp.mean(x * x, axis=-1, keepdims=True) + EPS) * g


def _sigmoid(x):
    return jax.nn.sigmoid(x)


def _softplus_neg_abs(x):
    return jnp.log(1.0 + jnp.exp(-jnp.abs(x)))


def _log_sigmoid(x):
    return jnp.minimum(x, 0.0) - _softplus_neg_abs(x)


def _head_norm(y, g_ref):
    outs = []
    for h in range(N_HEADS):
        yh = y[:, h * HEAD_DIM:(h + 1) * HEAD_DIM]
        outs.append(_rms(yh, g_ref[:, h * HEAD_DIM:(h + 1) * HEAD_DIM]))
    return jnp.concatenate(outs, axis=1)


def _store_heads(o_ref, t, fill):
    rows = t.shape[0]
    tail = jnp.where(lax.broadcasted_iota(jnp.int32, (rows, LANES - HEAD_DIM), 1) == 0, fill, 0.0)
    for h in range(N_HEADS):
        o_ref[h] = jnp.concatenate([t[:, h * HEAD_DIM:(h + 1) * HEAD_DIM], tail], axis=1).astype(o_ref.dtype)


def _in_proj_body(x_ref, g_ref, w_ref, dqb, dk, dkb, dv, dvh, mqk, mv, mo, gqk, gv, sqb, sk, skb, sv, svh,
                  z, small):
    hb = _rms(x_ref[...], g_ref[...]).astype(MXU_DTYPE)

    def proj(a, b):
        return jnp.dot(hb, w_ref[:, a:b], preferred_element_type=f32)

    dqb[...] = (proj(C_DQ, C_DK) * DIFF_D ** -0.5).astype(dqb.dtype)
    t = proj(C_DK, C_DV)
    dk[...] = t
    dkb[...] = t.astype(dkb.dtype)
    t = proj(C_DV, C_MQK)
    dv[...] = t
    _store_heads(dvh, t, 1.0)
    mqk[...] = proj(C_MQK, C_MV)
    mv[...] = proj(C_MV, C_MO)
    mo[...] = proj(C_MO, C_GQK)
    gqk[...] = proj(C_GQK, C_GV)
    gv[...] = proj(C_GV, C_SQ)
    sqb[...] = (proj(C_SQ, C_SK) * HEAD_DIM ** -0.5).astype(sqb.dtype)
    t = proj(C_SK, C_SV)
    sk[...] = t
    skb[...] = t.astype(skb.dtype)
    t = proj(C_SV, C_Z)
    sv[...] = t
    _store_heads(svh, t, 0.0)
    z[...] = proj(C_Z, C_SM)
    small[...] = proj(C_SM, N_PROJ)


def in_proj(x, g_pre, w_perm):
    R = x.shape[0]
    tm = min(R, 512)
    HM = "head-major"
    widths = [(256, MXU_DTYPE), (256, f32), (256, MXU_DTYPE), (256, f32), (HM, MXU_DTYPE), (512, f32), (256, f32),
              (256, f32), (256, f32), (256, f32), (256, MXU_DTYPE), (256, f32), (256, MXU_DTYPE), (256, f32),
              (HM, MXU_DTYPE), (1024, f32), (LANES, f32)]
    row = lambda n: (pl.BlockSpec((N_HEADS, tm, LANES), lambda i: (0, i, 0)) if n == HM
                     else pl.BlockSpec((tm, n), lambda i: (i, 0)))
    shape = lambda n: (N_HEADS, R, LANES) if n == HM else (R, n)
    const = lambda s: pl.BlockSpec(s, lambda i: (0, 0), pipeline_mode=pl.Buffered(1))
    return pl.pallas_call(
        _in_proj_body,
        grid=(R // tm,),
        in_specs=[row(D_MODEL), const((1, D_MODEL)), const((D_MODEL, N_PROJ))],
        out_specs=[row(n) for n, _ in widths],
        out_shape=[jax.ShapeDtypeStruct(shape(n), d) for n, d in widths],
        compiler_params=_cparams(("arbitrary",)),
        name="in_proj",
    )(x, g_pre.reshape(1, D_MODEL), w_perm)


def _out_proj_body(x_ref, p_ref, ya, ym, yg, ys, z_ref, gpre, gpost, wg, wb, wo, wple, wpg, o_ref):
    x = x_ref[...]
    hb = _rms(x, gpre[...]).astype(MXU_DTYPE)
    acc = None
    for n, y in enumerate((ya, ym, yg, ys)):
        zz = z_ref[:, n * BRANCH_W:(n + 1) * BRANCH_W]
        br = y[...] * (zz * _sigmoid(zz))
        pb = _mm(br, wb[n])
        gt = _sigmoid(jnp.dot(hb, wg[:, n * D_MODEL:(n + 1) * D_MODEL], preferred_element_type=f32))
        acc = gt * pb if acc is None else acc + gt * pb
    x1 = x + _rms(_mm(acc, wo[...]), gpost[...])
    o_ref[...] = x1 + _sigmoid(_mm(x1, wpg[...])) * _mm(p_ref[...], wple[...])


def out_proj(x, p, ya, ym, yg, ys, z, g_pre, g_post, w_gate, w_branch, w_out, w_ple, w_ple_gate):
    R = x.shape[0]
    tm = min(R, 256)
    row = lambda n: pl.BlockSpec((tm, n), lambda i: (i, 0))
    const = lambda s: pl.BlockSpec(s, lambda i: (0,) * len(s), pipeline_mode=pl.Buffered(1))
    return pl.pallas_call(
        _out_proj_body,
        grid=(R // tm,),
        in_specs=[row(D_MODEL), row(P_DIM), row(BRANCH_W), row(BRANCH_W), row(BRANCH_W), row(BRANCH_W), row(D_MODEL),
                  const((1, D_MODEL)), const((1, D_MODEL)), const((D_MODEL, N_BRANCH * D_MODEL)),
                  const((N_BRANCH, BRANCH_W, D_MODEL)), const((D_MODEL, D_MODEL)), const((P_DIM, D_MODEL)),
                  const((D_MODEL, D_MODEL))],
        out_specs=row(D_MODEL),
        out_shape=jax.ShapeDtypeStruct((R, D_MODEL), f32),
        compiler_params=_cparams(("arbitrary",)),
        name="out_proj",
    )(x, p, ya, ym, yg, ys, z, g_pre.reshape(1, D_MODEL), g_post.reshape(1, D_MODEL), w_gate, w_branch, w_out,
      w_ple, w_ple_gate)


def _diff_lambda(dl_ref, lam_init):
    dl = dl_ref[...]
    return (jnp.exp(jnp.sum(dl[0:1] * dl[1:2], keepdims=True)) - jnp.exp(jnp.sum(dl[2:3] * dl[3:4], keepdims=True))
            + lam_init)


def _diff_masks(q):
    lane = lax.broadcasted_iota(jnp.int32, (1, BRANCH_W), 1)
    out = []
    for h in range(N_HEADS):
        for m in range(2):
            lo = h * HEAD_DIM + m * DIFF_D
            out.append(jnp.where((lane >= lo) & (lane < lo + DIFF_D), q, jnp.zeros_like(q)))
    return out


def _diff_attn_body(lam_init, q_ref, k_ref, v_ref, dl_ref, g_ref, o_ref, qm_scr, m_scr, acc_scr):
    i = pl.program_id(1)
    tq, big = ATT_TQ, DIFF_TK // ATT_TQ
    for idx, qq in enumerate(_diff_masks(q_ref[0])):
        qm_scr[idx] = qq
    m_scr[...] = jnp.full(m_scr.shape, -jnp.inf, f32)
    acc_scr[...] = jnp.zeros(acc_scr.shape, f32)

    def chunk(k0, width, masked):
        k = k_ref[0, pl.ds(k0, width), :]
        if masked:
            valid = (lax.broadcasted_iota(jnp.int32, (tq, width), 1) + k0
                     <= lax.broadcasted_iota(jnp.int32, (tq, width), 0) + i * tq)

        def head(h, carry):
            v = v_ref[h, pl.ds(k0, width), :]
            scores = [_mm_nt(qm_scr[2 * h + m], k) for m in range(2)]
            for m, s in enumerate(scores):
                idx = 2 * h + m
                if masked:
                    s = jnp.where(valid, s, -jnp.inf)
                m_old = m_scr[idx]
                m_new = jnp.maximum(m_old, jnp.max(s, axis=1, keepdims=True))
                acc_scr[idx] = jnp.exp(m_old - m_new) * acc_scr[idx] + _mm(jnp.exp(s - m_new), v)
                m_scr[idx] = m_new
            return carry

        lax.fori_loop(0, N_HEADS, head, 0)

    @pl.when(i < big - 1)
    def _():
        for r in range(big - 1):
            @pl.when(i > r)
            def _():
                chunk(r * tq, tq, False)
        chunk(pl.multiple_of(i * tq, tq), tq, True)

    @pl.when(i >= big - 1)
    def _():
        chunk(pl.multiple_of((i + 1 - big) * tq, tq), DIFF_TK, True)
        below = i + 1 - big
        n_big = below // big
        first = (below % big) * tq
        lax.fori_loop(0, n_big, lambda j, c: (chunk(pl.multiple_of(first + j * DIFF_TK, tq), DIFF_TK, False), c)[1],
                      0)
        for r in range(big - 1):
            @pl.when(below % big > r)
            def _():
                chunk(r * tq, tq, False)

    lam = _diff_lambda(dl_ref, lam_init)
    outs = []
    for h in range(N_HEADS):
        a1, a2 = acc_scr[2 * h], acc_scr[2 * h + 1]
        o1 = a1[:, :HEAD_DIM] * (1.0 / a1[:, HEAD_DIM:HEAD_DIM + 1])
        o2 = a2[:, :HEAD_DIM] * (1.0 / a2[:, HEAD_DIM:HEAD_DIM + 1])
        outs.append(o1 - lam * o2)
    o_ref[0] = _head_norm(jnp.concatenate(outs, axis=1), g_ref) * (1.0 - lam_init)


def diff_attn(lam_init, qb, kb, vh, diff_lambda, norm_g):
    B, T, _ = qb.shape
    tq = ATT_TQ
    return pl.pallas_call(
        functools.partial(_diff_attn_body, lam_init),
        grid=(B, T // tq),
        in_specs=[pl.BlockSpec((1, tq, BRANCH_W), lambda b, i: (b, i, 0)),
                  pl.BlockSpec((1, T, BRANCH_W), lambda b, i: (b, 0, 0)),
                  pl.BlockSpec((N_HEADS, None, T, LANES), lambda b, i: (0, b, 0, 0)),
                  pl.BlockSpec((4, DIFF_D), lambda b, i: (0, 0)),
                  pl.BlockSpec((1, BRANCH_W), lambda b, i: (0, 0))],
        out_specs=pl.BlockSpec((1, tq, BRANCH_W), lambda b, i: (b, i, 0)),
        out_shape=jax.ShapeDtypeStruct((B, T, BRANCH_W), f32),
        scratch_shapes=[pltpu.VMEM((2 * N_HEADS, tq, BRANCH_W), MXU_DTYPE), pltpu.VMEM((2 * N_HEADS, tq, 1), f32),
                        pltpu.VMEM((2 * N_HEADS, tq, LANES), f32)],
        compiler_params=_cparams(("arbitrary", "arbitrary")),
        name="diff_attn",
    )(qb, kb, vh, diff_lambda, norm_g.reshape(1, BRANCH_W))


def _head_masks(q):
    lane = lax.broadcasted_iota(jnp.int32, (1, BRANCH_W), 1)
    return [jnp.where((lane >= h * HEAD_DIM) & (lane < (h + 1) * HEAD_DIM), q, jnp.zeros_like(q))
            for h in range(N_HEADS)]


def _suffix_matrix(n):
    s = np.arange(n)[:, None]
    j = np.arange(n)[None, :]
    return np.concatenate([(s > j).astype(np.float32), np.ones((n, LANES), np.float32)], axis=1)


def _sb_group(z, c, u, valid):
    n = z.shape[1]
    sp = _softplus_neg_abs(z)
    ls = jnp.minimum(z, 0.0) - sp
    lf = -jnp.maximum(z, 0.0) - sp
    if valid is not None:
        lf = jnp.where(valid, lf, 0.0)
    r = _sel_mm(lf, u, 2)
    a = jnp.exp(ls + r[:, :n] + jnp.concatenate([c] * (n // LANES), axis=1))
    if valid is not None:
        a = jnp.where(valid, a, 0.0)
    return a, c + r[:, n:]


def _sb_attn_body(q_ref, k_ref, v_ref, u_ref, o_ref, qm_scr, c_scr, acc_scr):
    i = pl.program_id(1)
    tq, tk = ATT_TQ, SB_TK
    for h, qq in enumerate(_head_masks(q_ref[0])):
        qm_scr[h] = qq
    c_scr[...] = jnp.zeros(c_scr.shape, f32)
    acc_scr[...] = jnp.zeros(acc_scr.shape, f32)
    row = lax.broadcasted_iota(jnp.int32, (tq, tk), 0) + i * tq
    col = lax.broadcasted_iota(jnp.int32, (tq, tk), 1)
    u = u_ref[...]

    def group(g, masked):
        k0 = pl.multiple_of(g * tk, tk)
        k = k_ref[0, pl.ds(k0, tk), :]
        valid = (col + k0 < row) if masked else None

        zs = [_mm_nt(qm_scr[h], k) for h in range(N_HEADS)]
        for h in range(N_HEADS):
            a, c_new = _sb_group(zs[h], c_scr[h], u, valid)
            acc_scr[h] = acc_scr[h] + _mm(a, v_ref[h, pl.ds(k0, tk), :])
            c_scr[h] = c_new

    n_diag = tq // tk
    for d in range(n_diag):
        group((i + 1) * n_diag - 1 - d, True)
    lax.fori_loop(0, i * n_diag, lambda n, c: (group(i * n_diag - 1 - n, False), c)[1], 0)
    o_ref[0] = jnp.concatenate([acc_scr[h][:, :HEAD_DIM] for h in range(N_HEADS)], axis=1)


def sb_attn(qb, kb, vh, u):
    B, T, _ = qb.shape
    tq = ATT_TQ
    return pl.pallas_call(
        _sb_attn_body,
        grid=(B, T // tq),
        in_specs=[pl.BlockSpec((1, tq, BRANCH_W), lambda b, i: (b, i, 0)),
                  pl.BlockSpec((1, T, BRANCH_W), lambda b, i: (b, 0, 0)),
                  pl.BlockSpec((N_HEADS, None, T, LANES), lambda b, i: (0, b, 0, 0)),
                  pl.BlockSpec((SB_TK, SB_TK + LANES), lambda b, i: (0, 0))],
        out_specs=pl.BlockSpec((1, tq, BRANCH_W), lambda b, i: (b, i, 0)),
        out_shape=jax.ShapeDtypeStruct((B, T, BRANCH_W), f32),
        scratch_shapes=[pltpu.VMEM((N_HEADS, tq, BRANCH_W), MXU_DTYPE), pltpu.VMEM((N_HEADS, tq, LANES), f32),
                        pltpu.VMEM((N_HEADS, tq, LANES), f32)],
        compiler_params=_cparams(("arbitrary", "arbitrary")),
        name="sb_attn",
    )(qb, kb, vh, u)


def _conv_taps(cw_ref, cb_ref, rows):
    y = cb_ref[...]
    for j in range(CONV_W):
        y = y + rows[j] * cw_ref[j:j + 1, :]
    return y * _sigmoid(y)


def _mlstm_body(mqk_ref, mv_ref, mo_ref, sm_ref, cw_ref, cb_ref, bif_ref, g_ref, tril_ref,
                ym_ref, C_ref, n_ref, m_ref, conv_ref, xc_scr, q_scr, k_scr, C_scr, n_scr, m_scr):
    t = pl.program_id(0)
    NB, TB, L = mqk_ref.shape[0], MLSTM_TB, REC_L
    pad = SUBLANES

    @pl.when(t == 0)
    def _():
        xc_scr[:, 0:pad, :] = jnp.zeros((NB, pad, 2 * BRANCH_W), f32)
        C_scr[...] = jnp.zeros(C_scr.shape, f32)
        n_scr[...] = jnp.zeros(n_scr.shape, f32)
        m_scr[...] = jnp.zeros(m_scr.shape, f32)

    for b in range(NB):
        xc_scr[b, pad:pad + TB, :] = mqk_ref[b]
        act = _conv_taps(cw_ref, cb_ref, [xc_scr[b, pad - (CONV_W - 1) + j:pad - (CONV_W - 1) + j + TB, :]
                                          for j in range(CONV_W)])
        q_scr[b] = act[:, :BRANCH_W]
        k_scr[b] = act[:, BRANCH_W:] * HEAD_DIM ** -0.5
        conv_ref[b] = xc_scr[b, pad + TB - (CONV_W - 1):pad + TB, :]
        xc_scr[b, 0:pad, :] = xc_scr[b, TB:TB + pad, :]

    tril = tril_ref[...]
    tri_mask = lax.broadcasted_iota(jnp.int32, (L, L), 1) <= lax.broadcasted_iota(jnp.int32, (L, L), 0)
    for c in range(TB // L):
        rs = slice(c * L, (c + 1) * L)
        for b in range(NB):
            gi = sm_ref[b, rs, :] + bif_ref[...]
            bcum = _mm_sel(tril, _log_sigmoid(gi), 3)
            bT = bcum.T
            iT = gi.T
            outs = []
            for h in range(N_HEADS):
                hs = slice(h * HEAD_DIM, (h + 1) * HEAD_DIM)
                bcol = bcum[:, SM_F + h:SM_F + h + 1]
                brow = bT[SM_F + h:SM_F + h + 1, :]
                irow = iT[SM_I + h:SM_I + h + 1, :]
                icol = gi[:, SM_I + h:SM_I + h + 1]
                m_h = m_scr[b, h]
                dmat = jnp.where(tri_mask, bcol - brow + irow, -jnp.inf)
                inter = bcol + m_h
                mt = jnp.maximum(inter, jnp.max(dmat, axis=1, keepdims=True))
                qh = q_scr[b, rs, hs]
                kh = k_scr[b, rs, hs]
                vh = mv_ref[b, rs, hs]
                w = _mm_nt(qh, kh) * jnp.exp(dmat - mt)
                si = jnp.exp(inter - mt)
                num = si * _mm(qh, C_scr[b, h]) + _mm(w, vh)
                den = si * jnp.sum(qh * n_scr[b, h], axis=1, keepdims=True) + jnp.sum(w, axis=1, keepdims=True)
                outs.append(num / jnp.maximum(jnp.abs(den), jnp.exp(-mt)))
                bl = bcol[L - 1:L, :]
                ws = bl - bcol + icol
                m_new = jnp.maximum(bl + m_h, jnp.max(ws, axis=0, keepdims=True))
                sc = jnp.exp(bl + m_h - m_new)
                ek = kh * jnp.exp(ws - m_new)
                C_scr[b, h] = sc * C_scr[b, h] + _mm_tn(ek, vh)
                n_scr[b, h] = sc * n_scr[b, h] + jnp.sum(ek, axis=0, keepdims=True)
                m_scr[b, h] = m_new
            ym_ref[b, rs, :] = _head_norm(jnp.concatenate(outs, axis=1), g_ref) * _sigmoid(mo_ref[b, rs, :])

    C_ref[...] = C_scr[...]
    for b in range(NB):
        n_ref[b] = jnp.concatenate([n_scr[b, h] for h in range(N_HEADS)], axis=1)
        m_ref[b] = jnp.concatenate([jnp.broadcast_to(m_scr[b, h], (1, LANES // N_HEADS)) for h in range(N_HEADS)],
                                   axis=1)


def mlstm_prompt(mqk, mv, mo, small, conv_w, conv_b, bif_row, norm_g, tril):
    B, T, _ = mqk.shape
    TB = MLSTM_TB
    blk = lambda n: pl.BlockSpec((B, TB, n), lambda t: (0, t, 0))
    const = lambda s: pl.BlockSpec(s, lambda t: (0,) * len(s))
    return pl.pallas_call(
        _mlstm_body,
        grid=(T // TB,),
        in_specs=[blk(2 * BRANCH_W), blk(BRANCH_W), blk(BRANCH_W), blk(LANES), const((CONV_W, 2 * BRANCH_W)),
                  const((1, 2 * BRANCH_W)), const((1, LANES)), const((1, BRANCH_W)), const((REC_L, REC_L))],
        out_specs=[blk(BRANCH_W), const((B, N_HEADS, HEAD_DIM, HEAD_DIM)), const((B, 1, BRANCH_W)),
                   const((B, 1, LANES)), const((B, CONV_W - 1, 2 * BRANCH_W))],
        out_shape=[jax.ShapeDtypeStruct((B, T, BRANCH_W), f32),
                   jax.ShapeDtypeStruct((B, N_HEADS, HEAD_DIM, HEAD_DIM), f32),
                   jax.ShapeDtypeStruct((B, 1, BRANCH_W), f32),
                   jax.ShapeDtypeStruct((B, 1, LANES), f32),
                   jax.ShapeDtypeStruct((B, CONV_W - 1, 2 * BRANCH_W), f32)],
        scratch_shapes=[pltpu.VMEM((B, TB + SUBLANES, 2 * BRANCH_W), f32), pltpu.VMEM((B, TB, BRANCH_W), f32),
                        pltpu.VMEM((B, TB, BRANCH_W), f32), pltpu.VMEM((B, N_HEADS, HEAD_DIM, HEAD_DIM), f32),
                        pltpu.VMEM((B, N_HEADS, 1, HEAD_DIM), f32), pltpu.VMEM((B, N_HEADS, 1, 1), f32)],
        compiler_params=_cparams(("arbitrary",)),
        name="mlstm_prompt",
    )(mqk, mv, mo, small, conv_w, conv_b.reshape(1, -1), bif_row, norm_g.reshape(1, BRANCH_W), tril)


def _gla_log_decay(sm, wa_ref, ba_ref):
    return _log_sigmoid(_mm(sm, wa_ref[...]) + ba_ref[...]) * (1.0 / GLA_TAU)


def _gla_body(gqk_ref, gv_ref, sm_ref, wa_ref, ba_ref, g_ref, lb_ref, ex_ref, yg_ref, S_ref, S_scr):
    t = pl.program_id(1)
    TB, L, R = REC_TB, REC_L, GLA_R
    NK = N_HEADS * GLA_DK

    @pl.when(t == 0)
    def _():
        S_scr[...] = jnp.zeros(S_scr.shape, f32)

    lb = lb_ref[...]
    ex = ex_ref[...]
    s_i = lax.broadcasted_iota(jnp.int32, (R, R, NK), 0)
    t_i = lax.broadcasted_iota(jnp.int32, (R, R, NK), 1)
    causal3 = t_i >= s_i
    lane_t = lax.broadcasted_iota(jnp.int32, (1, L), 1)
    bd_mask = (lax.broadcasted_iota(jnp.int32, (NK, BRANCH_W), 0) // GLA_DK
               == lax.broadcasted_iota(jnp.int32, (NK, BRANCH_W), 1) // HEAD_DIM)
    for c in range(TB // L):
        rs = slice(c * L, (c + 1) * L)
        la = _gla_log_decay(sm_ref[0, rs, :], wa_ref, ba_ref)
        bcl = _mm_sel(lb, la, 3)
        q_c = gqk_ref[0, rs, 0:NK] * GLA_DK ** -0.5
        k_c = gqk_ref[0, rs, NK:2 * NK]
        v_c = gv_ref[0, rs, :]
        bll = jnp.concatenate([jnp.broadcast_to(bcl[(I + 1) * R - 1:(I + 1) * R, :], (R, NK)) for I in range(L // R)],
                              axis=0)
        ktilT = (k_c * jnp.exp(bll - bcl)).T
        dblT = jnp.exp(bll).T
        qin = q_c * jnp.exp(bcl)
        outs = []
        for I in range(L // R):
            sl = slice(I * R, (I + 1) * R)
            S = S_scr[...]
            bb, vv = bcl[sl], v_c[sl]
            d = jnp.where(causal3, bb[None, :, :] - bb[:, None, :], -jnp.inf)
            p = q_c[sl][None, :, :] * k_c[sl][:, None, :] * jnp.exp(d)
            a = _sel_mm(p.reshape(R * R, NK), ex, 2).reshape(R, R, BRANCH_W)
            outs.append(_mm(qin[sl], S) + jnp.sum(a * vv[:, None, :], axis=0))
            in_blk = (lane_t >= I * R) & (lane_t < (I + 1) * R)
            upd = _mm(jnp.where(in_blk, ktilT, 0.0), v_c)
            S_scr[...] = dblT[:, I * R:I * R + 1] * S + jnp.where(bd_mask, upd, 0.0)
        yg_ref[0, rs, :] = _head_norm(jnp.concatenate(outs, axis=0), g_ref)

    for h in range(N_HEADS):
        S_ref[0, h] = S_scr[h * GLA_DK:(h + 1) * GLA_DK, h * HEAD_DIM:(h + 1) * HEAD_DIM]


def gla_prompt(gqk, gv, small, wa, ba, norm_g, lb, ex):
    B, T, _ = gqk.shape
    TB = REC_TB
    blk = lambda n: pl.BlockSpec((1, TB, n), lambda b, t: (b, t, 0))
    const = lambda s: pl.BlockSpec(s, lambda b, t: (0,) * len(s))
    return pl.pallas_call(
        _gla_body,
        grid=(B, T // TB),
        in_specs=[blk(BRANCH_W), blk(BRANCH_W), blk(LANES), const((LANES, LANES)), const((1, LANES)),
                  const((1, BRANCH_W)), const((REC_L, REC_L)), const((LANES, BRANCH_W))],
        out_specs=[blk(BRANCH_W), pl.BlockSpec((1, N_HEADS, GLA_DK, HEAD_DIM), lambda b, t: (b, 0, 0, 0))],
        out_shape=[jax.ShapeDtypeStruct((B, T, BRANCH_W), f32),
                   jax.ShapeDtypeStruct((B, N_HEADS, GLA_DK, HEAD_DIM), f32)],
        scratch_shapes=[pltpu.VMEM((N_HEADS * GLA_DK, BRANCH_W), f32)],
        compiler_params=_cparams(("arbitrary", "arbitrary")),
        name="gla_prompt",
    )(gqk, gv, small, wa, ba.reshape(1, LANES), norm_g.reshape(1, BRANCH_W), lb, ex)


def _lane_replicated_column(row):
    n = row.shape[1]
    eye = lax.broadcasted_iota(jnp.int32, (n, n), 0) == lax.broadcasted_iota(jnp.int32, (n, n), 1)
    col = jnp.sum(jnp.where(eye, jnp.broadcast_to(row, (n, n)), 0.0), axis=1, keepdims=True)
    return jnp.broadcast_to(col, (n, LANES))


def _rows_per_head(x4):
    return jnp.broadcast_to(x4[:, None, :], (N_HEADS, HEAD_DIM, LANES)).reshape(BRANCH_W, LANES)


def _lanes_per_head(col4):
    sel = (lax.broadcasted_iota(jnp.int32, (N_HEADS, BRANCH_W), 1) // HEAD_DIM
           == lax.broadcasted_iota(jnp.int32, (N_HEADS, BRANCH_W), 0))
    return jnp.sum(jnp.where(sel, col4, 0.0), axis=0, keepdims=True)


def _dec_attn_body(lam_init, n_pages, pt_ref, dq_ref, dkn_ref, dvn_ref, sq_ref, dl_ref, g_ref, u_ref, *rest):
    dk_pg, dv_pg, sk_pg, sv_pg = (rest[j * n_pages:(j + 1) * n_pages] for j in range(4))
    ya_ref, ys_ref = rest[4 * n_pages:]
    lam = _diff_lambda(dl_ref, lam_init)

    q_row = dq_ref[0].astype(f32)
    qx = _lane_replicated_column(q_row)
    s1, s2 = [], []
    for p in range(n_pages):
        r = jnp.sum((dk_pg[p][...] * qx).reshape(N_HEADS, 2, DIFF_D, LANES), axis=2)
        s1.append(r[:, 0, :])
        s2.append(r[:, 1, :])
    lane = lax.broadcasted_iota(jnp.int32, (N_HEADS, BRANCH_W), 1)
    head = lax.broadcasted_iota(jnp.int32, (N_HEADS, BRANCH_W), 0)
    qk_new = jnp.broadcast_to(q_row * dkn_ref[0], (N_HEADS, BRANCH_W))
    w_pages, w_new = [], []
    for m, s in enumerate((s1, s2)):
        lo = head * HEAD_DIM + m * DIFF_D
        s_new = jnp.sum(jnp.where((lane >= lo) & (lane < lo + DIFF_D), qk_new, 0.0), axis=1, keepdims=True)
        smax = s[0]
        for p in range(1, n_pages):
            smax = jnp.maximum(smax, s[p])
        mx = jnp.maximum(jnp.max(smax, axis=1, keepdims=True), s_new)
        e = [jnp.exp(sp - mx) for sp in s]
        e_new = jnp.exp(s_new - mx)
        tot = e[0]
        for p in range(1, n_pages):
            tot = tot + e[p]
        inv = 1.0 / (jnp.sum(tot, axis=1, keepdims=True) + e_new)
        w_pages.append([ep * inv for ep in e])
        w_new.append(e_new * inv)
    acc = jnp.zeros((BRANCH_W, LANES), f32)
    for p in range(n_pages):
        acc = acc + _rows_per_head(w_pages[0][p] - lam * w_pages[1][p]) * dv_pg[p][...]
    y = jnp.sum(acc.T, axis=0, keepdims=True) + _lanes_per_head(w_new[0] - lam * w_new[1]) * dvn_ref[0]
    ya_ref[0] = _head_norm(y, g_ref) * (1.0 - lam_init)

    qx = _lane_replicated_column(sq_ref[0].astype(f32))
    z = jnp.concatenate([jnp.sum((sk_pg[p][...] * qx).reshape(N_HEADS, HEAD_DIM, LANES), axis=1)
                         for p in range(n_pages)], axis=0)
    sp = _softplus_neg_abs(z)
    ls = jnp.minimum(z, 0.0) - sp
    r = _sel_mm(-jnp.maximum(z, 0.0) - sp, u_ref[...], 2)
    base = ls + r[:, :LANES]
    carry = jnp.zeros((N_HEADS, LANES), f32)
    acc = jnp.zeros((BRANCH_W, LANES), f32)
    for p in reversed(range(n_pages)):
        rows = slice(p * N_HEADS, (p + 1) * N_HEADS)
        acc = acc + _rows_per_head(jnp.exp(base[rows] + carry)) * sv_pg[p][...]
        carry = carry + r[rows, LANES:]
    ys_ref[0] = jnp.sum(acc.T, axis=0, keepdims=True)


def _pages_as_feature_by_position(cache):
    d, n, pg, h, hd = cache.shape
    return jnp.transpose(cache, (0, 1, 3, 4, 2)).reshape(d, n, h * hd, pg)


def dec_attn(layer, lam_init, page_table, dqb, dk_new, dv_new, sqb, diff_lambda, norm_g, u, caches_t):
    B, n_pages = page_table.shape
    page = caches_t[0].shape[3]
    assert page == LANES
    row3 = lambda a: a.reshape(B, 1, BRANCH_W)
    rspec = pl.BlockSpec((1, 1, BRANCH_W), lambda b, pt: (b, 0, 0))
    const = lambda shp: pl.BlockSpec(shp, lambda b, pt: (0,) * len(shp))

    def pg(p):
        return pl.BlockSpec((None, None, BRANCH_W, page), lambda b, pt: (layer, pt[b * n_pages + p], 0, 0))

    in_specs = [rspec, rspec, rspec, rspec, const((4, DIFF_D)), const((1, BRANCH_W)), const((LANES, 2 * LANES))]
    args = [row3(dqb), row3(dk_new), row3(dv_new), row3(sqb), diff_lambda, norm_g.reshape(1, BRANCH_W), u]
    for c in caches_t:
        for p in range(n_pages):
            in_specs.append(pg(p))
            args.append(c)
    ya, ys = pl.pallas_call(
        functools.partial(_dec_attn_body, lam_init, n_pages),
        grid_spec=pltpu.PrefetchScalarGridSpec(
            num_scalar_prefetch=1, grid=(B,), in_specs=in_specs,
            out_specs=[pl.BlockSpec((1, 1, BRANCH_W), lambda b, pt: (b, 0, 0))] * 2),
        out_shape=[jax.ShapeDtypeStruct((B, 1, BRANCH_W), f32)] * 2,
        compiler_params=_cparams(("arbitrary",)),
        name="dec_attn",
    )(page_table.reshape(-1), *args)
    return ya.reshape(B, BRANCH_W), ys.reshape(B, BRANCH_W)


def _expand_heads(x4, width):
    return jnp.concatenate([jnp.broadcast_to(x4[:, h:h + 1], (x4.shape[0], width)) for h in range(N_HEADS)], axis=1)


def _dec_rec_body(mqk_ref, mv_ref, mo_ref, sm_ref, gqk_ref, gv_ref, conv_ref, C_ref, n_ref, m_ref, S_ref,
                  cw_ref, cb_ref, bif_ref, mg_ref, wa_ref, ba_ref, gg_ref,
                  ym_ref, yg_ref, conv_o, C_o, n_o, m_o, S_o, CT_scr, ST_scr, numT_scr, oT_scr):
    h = pl.program_id(0)
    W = 2 * BRANCH_W
    NK = N_HEADS * GLA_DK
    u = mqk_ref[...]
    rows = [conv_ref[:, j * W:(j + 1) * W] for j in range(CONV_W - 1)] + [u]
    act = _conv_taps(cw_ref, cb_ref, rows)
    q = act[:, :BRANCH_W]
    k = act[:, BRANCH_W:] * HEAD_DIM ** -0.5
    conv_o[...] = jnp.concatenate(rows[1:], axis=1)

    gi = sm_ref[...] + bif_ref[...]
    ig = gi[:, SM_I:SM_I + N_HEADS]
    lf = _log_sigmoid(gi)[:, SM_F:SM_F + N_HEADS]
    m_old = m_ref[...]
    m_new = jnp.maximum(lf + m_old, ig)
    sc = _expand_heads(jnp.exp(lf + m_old - m_new), HEAD_DIM)
    ek = _expand_heads(jnp.exp(ig - m_new), HEAD_DIM) * k
    n_new = sc * n_ref[...] + ek
    m_o[...] = m_new
    n_o[...] = n_new

    la = _gla_log_decay(sm_ref[...], wa_ref, ba_ref)
    gq = gqk_ref[:, 0:NK] * GLA_DK ** -0.5
    gk = gqk_ref[:, NK:2 * NK]

    r64 = pl.ds(pl.multiple_of(h * HEAD_DIM, HEAD_DIM), HEAD_DIM)
    r32 = pl.ds(pl.multiple_of(h * GLA_DK, GLA_DK), GLA_DK)
    numT_scr[0] = q.T
    numT_scr[1] = ek.T
    numT_scr[2] = mv_ref[...].T
    numT_scr[3] = sc.T
    qT, ekT, vT = numT_scr[0, r64, :], numT_scr[1, r64, :], numT_scr[2, r64, :]
    scT = numT_scr[3, pl.ds(h * HEAD_DIM, 1), :]
    CT_scr[...] = C_ref[...].T
    num = jnp.zeros((HEAD_DIM, LANES), f32)
    for kk in range(HEAD_DIM):
        blk = slice(kk * HEAD_DIM, (kk + 1) * HEAD_DIM)
        new = scT * CT_scr[blk, :] + ekT[kk:kk + 1, :] * vT
        CT_scr[blk, :] = new
        num = num + qT[kk:kk + 1, :] * new
    C_o[...] = CT_scr[...].T
    numT_scr[4, r64, :] = num

    oT_scr[0, 0:NK, :] = gq.T
    oT_scr[0, NK:2 * NK, :] = gk.T
    oT_scr[1, 0:NK, :] = jnp.exp(la).T
    oT_scr[2] = gv_ref[...].T
    gqT, gkT, decT = oT_scr[0, r32, :], oT_scr[0, pl.ds(pl.multiple_of(NK + h * GLA_DK, GLA_DK), GLA_DK), :], \
        oT_scr[1, r32, :]
    gvT = oT_scr[2, r64, :]
    ST_scr[...] = S_ref[...].T
    o = jnp.zeros((HEAD_DIM, LANES), f32)
    for kk in range(GLA_DK):
        blk = slice(kk * HEAD_DIM, (kk + 1) * HEAD_DIM)
        new = decT[kk:kk + 1, :] * ST_scr[blk, :] + gkT[kk:kk + 1, :] * gvT
        ST_scr[blk, :] = new
        o = o + gqT[kk:kk + 1, :] * new
    S_o[...] = ST_scr[...].T
    oT_scr[3, r64, :] = o

    @pl.when(h == N_HEADS - 1)
    def _():
        qn = jnp.concatenate([jnp.sum((q * n_new)[:, g * HEAD_DIM:(g + 1) * HEAD_DIM], axis=1, keepdims=True)
                              for g in range(N_HEADS)], axis=1)
        den = jnp.maximum(jnp.abs(qn), jnp.exp(-m_new))
        hm = numT_scr[4].T / _expand_heads(den, HEAD_DIM)
        ym_ref[...] = _head_norm(hm, mg_ref) * _sigmoid(mo_ref[...])
        yg_ref[...] = _head_norm(oT_scr[3].T, gg_ref)


def dec_rec(mqk, mv, mo, small, gqk, gv, conv, C, n, m, S, conv_w, conv_b, bif_row, mnorm_g, wa, ba, gnorm_g):
    B = mqk.shape[0]
    CW = HEAD_DIM * HEAD_DIM
    SW = GLA_DK * HEAD_DIM
    full = lambda a: pl.BlockSpec(a.shape, lambda h: (0,) * a.ndim)
    ins = [mqk, mv, mo, small, gqk, gv, conv.reshape(B, -1), C.reshape(B, N_HEADS * CW), n.reshape(B, BRANCH_W), m,
           S.reshape(B, N_HEADS * SW), conv_w, conv_b.reshape(1, -1), bif_row, mnorm_g.reshape(1, BRANCH_W), wa,
           ba.reshape(1, LANES), gnorm_g.reshape(1, BRANCH_W)]
    in_specs = [full(a) for a in ins]
    in_specs[7] = pl.BlockSpec((B, CW), lambda h: (0, h))
    in_specs[10] = pl.BlockSpec((B, SW), lambda h: (0, h))
    out_shape = [jax.ShapeDtypeStruct((B, BRANCH_W), f32), jax.ShapeDtypeStruct((B, BRANCH_W), f32),
                 jax.ShapeDtypeStruct((B, (CONV_W - 1) * 2 * BRANCH_W), f32),
                 jax.ShapeDtypeStruct((B, N_HEADS * CW), f32), jax.ShapeDtypeStruct((B, BRANCH_W), f32),
                 jax.ShapeDtypeStruct((B, N_HEADS), f32), jax.ShapeDtypeStruct((B, N_HEADS * SW), f32)]
    out_specs = [pl.BlockSpec(s.shape, lambda h: (0, 0)) for s in out_shape]
    out_specs[3] = pl.BlockSpec((B, CW), lambda h: (0, h))
    out_specs[6] = pl.BlockSpec((B, SW), lambda h: (0, h))
    ym, yg, conv_n, C_n, n_n, m_n, S_n = pl.pallas_call(
        _dec_rec_body,
        grid=(N_HEADS,),
        in_specs=in_specs, out_specs=out_specs, out_shape=out_shape,
        scratch_shapes=[pltpu.VMEM((CW, B), f32), pltpu.VMEM((SW, B), f32), pltpu.VMEM((5, BRANCH_W, B), f32),
                        pltpu.VMEM((4, BRANCH_W, B), f32)],
        compiler_params=_cparams(("arbitrary",)),
        name="dec_rec",
    )(*ins)
    return (ym, yg, conv_n.reshape(B, CONV_W - 1, 2 * BRANCH_W), C_n.reshape(B, N_HEADS, HEAD_DIM, HEAD_DIM),
            n_n.reshape(B, N_HEADS, HEAD_DIM), m_n, S_n.reshape(B, N_HEADS, GLA_DK, HEAD_DIM))


def _constants():
    t = np.arange(REC_L)
    tril = (t[None, :] <= t[:, None]).astype(np.float32)
    lb = tril * (t[None, :] // GLA_R == t[:, None] // GLA_R)
    ex = (np.arange(N_HEADS * GLA_DK)[:, None] // GLA_DK == np.arange(BRANCH_W)[None, :] // HEAD_DIM)
    cast = lambda a: jnp.asarray(a, f32).astype(MXU_DTYPE)
    return dict(tril=cast(tril), lb=cast(lb), ex=cast(ex), u_page=cast(_suffix_matrix(LANES)),
                u_sb=cast(_suffix_matrix(SB_TK)))


def _layer_params(l, prm):
    bif = prm['mlstm_b_if'][l].astype(f32).reshape(1, 2 * N_HEADS)
    wa = jnp.zeros((LANES, LANES), f32).at[SM_A:SM_A + GLA_LOWRANK].set(prm['gla_w_a2'][l])
    return dict(
        lam_init=0.8 - 0.6 * math.exp(-0.3 * l),
        bif_row=jnp.pad(bif, ((0, 0), (0, LANES - 2 * N_HEADS))),
        wa=wa.astype(MXU_DTYPE),
    )


def _prep_weights(w_in, w_gate, w_branch, w_out, w_ple, w_ple_gate):
    c = lambda a: a.astype(MXU_DTYPE)
    o = _ORIG
    parts = [w_in[:, :, :o['mi']], w_in[:, :, o['mo']:o['ga']], w_in[:, :, o['sq']:], w_in[:, :, o['mi']:o['mo']],
             w_in[:, :, o['ga']:o['sq']]]
    used = sum(p.shape[2] for p in parts)
    w_perm = jnp.concatenate([c(p) for p in parts]
                             + [jnp.zeros(w_in.shape[:2] + (N_PROJ - used,), MXU_DTYPE)], axis=2)
    return w_perm, c(w_gate), c(w_branch), c(w_out), c(w_ple), c(w_ple_gate)


def _prompt_layer(l, x, p_l, prm, wts, consts):
    B, T, _ = x.shape
    lp = _layer_params(l, prm)
    w_perm, w_gate, w_branch, w_out, w_ple, w_ple_gate = wts
    x2 = x.reshape(B * T, D_MODEL)
    (dqb, dk, dkb, dv, dvh, mqk, mv, mo, gqk, gv, sqb, sk, skb, sv, svh, z, small) = in_proj(
        x2, prm['g_pre'][l], w_perm[l])
    r3 = lambda a: a.reshape(B, T, a.shape[-1])
    hm = lambda a: a.reshape(N_HEADS, B, T, LANES)
    ya = diff_attn(lp['lam_init'], r3(dqb), r3(dkb), hm(dvh), prm['diff_lambda'][l], prm['diff_norm_g'][l])
    ys = sb_attn(r3(sqb), r3(skb), hm(svh), consts['u_sb'])
    ym, C1, n1, m1, conv1 = mlstm_prompt(r3(mqk), r3(mv), r3(mo), r3(small), prm['mlstm_conv_w'][l],
                                         prm['mlstm_conv_b'][l], lp['bif_row'], prm['mlstm_norm_g'][l],
                                         consts['tril'])
    yg, S1 = gla_prompt(r3(gqk), r3(gv), r3(small), lp['wa'], _pad_ba(prm['gla_b_a'][l]), prm['gla_norm_g'][l],
                        consts['lb'], consts['ex'])
    flat = lambda a: a.reshape(B * T, BRANCH_W)
    y = out_proj(x2, p_l.reshape(B * T, P_DIM), flat(ya), flat(ym), flat(yg), flat(ys), z, prm['g_pre'][l],
                 prm['g_post'][l], w_gate[l], w_branch[l], w_out[l], w_ple[l], w_ple_gate[l])
    hd = lambda a: a.reshape(B, T, N_HEADS, HEAD_DIM)
    state = (hd(dk), hd(dv), hd(sk), hd(sv), C1, n1.reshape(B, N_HEADS, HEAD_DIM),
             m1[:, 0, ::LANES // N_HEADS], conv1, S1)
    return y.reshape(B, T, D_MODEL), state


def _pad_ba(ba):
    return ba.astype(f32)


def _decode_layer(l, x, p_l, prm, wts, consts, caches, states, page_table):
    B = x.shape[0]
    lp = _layer_params(l, prm)
    w_perm, w_gate, w_branch, w_out, w_ple, w_ple_gate = wts
    x2 = x.reshape(B, D_MODEL)
    (dqb, dk, dkb, dv, dvh, mqk, mv, mo, gqk, gv, sqb, sk, skb, sv, svh, z, small) = in_proj(
        x2, prm['g_pre'][l], w_perm[l])
    ya, ys = dec_attn(l, lp['lam_init'], page_table, dqb, dk, dv, sqb, prm['diff_lambda'][l], prm['diff_norm_g'][l],
                      consts['u_page'], caches)
    C0, n0, m0, conv0, S0 = states
    ym, yg, conv1, C1, n1, m1, S1 = dec_rec(mqk, mv, mo, small, gqk, gv, conv0[l], C0[l], n0[l], m0[l], S0[l],
                                            prm['mlstm_conv_w'][l], prm['mlstm_conv_b'][l], lp['bif_row'],
                                            prm['mlstm_norm_g'][l], lp['wa'], _pad_ba(prm['gla_b_a'][l]),
                                            prm['gla_norm_g'][l])
    y = out_proj(x2, p_l.reshape(B, P_DIM), ya, ym, yg, ys, z, prm['g_pre'][l], prm['g_post'][l], w_gate[l],
                 w_branch[l], w_out[l], w_ple[l], w_ple_gate[l])
    hd = lambda a: a.reshape(B, 1, N_HEADS, HEAD_DIM)
    state = (hd(dk), hd(dv), hd(sk), hd(sv), C1, n1, m1, conv1, S1)
    return y.reshape(B, 1, D_MODEL), state


def kernel(x_prompt, x_sample, cache_diff_k, cache_diff_v, cache_sb_k, cache_sb_v, state_mlstm_C, state_mlstm_n,
           state_mlstm_m, state_mlstm_conv, state_gla_S, page_table, p_prompt, p_sample, g_pre, g_post, w_in,
           diff_lambda, diff_norm_g, mlstm_conv_w, mlstm_conv_b, mlstm_b_if, mlstm_norm_g, gla_w_a2, gla_b_a,
           gla_norm_g, w_branch, w_gate, w_out, w_ple, w_ple_gate):
    prm = dict(g_pre=g_pre, g_post=g_post, diff_lambda=diff_lambda, diff_norm_g=diff_norm_g,
               mlstm_conv_w=mlstm_conv_w, mlstm_conv_b=mlstm_conv_b, mlstm_b_if=mlstm_b_if,
               mlstm_norm_g=mlstm_norm_g, gla_w_a2=gla_w_a2, gla_b_a=gla_b_a, gla_norm_g=gla_norm_g)
    wts = _prep_weights(w_in, w_gate, w_branch, w_out, w_ple, w_ple_gate)
    consts = _constants()
    depth = w_in.shape[0]
    caches = tuple(_pages_as_feature_by_position(c) for c in (cache_diff_k, cache_diff_v, cache_sb_k, cache_sb_v))
    states = (state_mlstm_C, state_mlstm_n, state_mlstm_m, state_mlstm_conv, state_gla_S)
    y_p, y_s = x_prompt, x_sample
    st_p, st_s = [], []
    for l in range(depth):
        y_p, s_p = _prompt_layer(l, y_p, p_prompt[l], prm, wts, consts)
        y_s, s_s = _decode_layer(l, y_s, p_sample[l], prm, wts, consts, caches, states, page_table)
        st_p.append(s_p)
        st_s.append(s_s)
    outs_p = [jnp.stack(t) for t in zip(*st_p)]
    outs_s = [jnp.stack(t) for t in zip(*st_s)]
    return (y_p, y_s, *outs_p, *outs_s)
```

```python
import functools
import math

import numpy as np
import jax
import jax.numpy as jnp
from jax import lax
from jax.experimental import pallas as pl
from jax.experimental.pallas import tpu as pltpu

f32 = jnp.float32
MXU_DTYPE = jnp.bfloat16

D_MODEL = 1024
N_BRANCH = 4
BRANCH_W = 256
N_HEADS = 4
HEAD_DIM = 64
DIFF_D = 32
GLA_DK = 32
GLA_LOWRANK = 16
GLA_TAU = 16.0
CONV_W = 4
EPS = 1e-6
P_DIM = 256

LANES = 128
SUBLANES = 8
VMEM_LIMIT = 56 * 1024 * 1024

_ORIG = dict(dq=0, dk=256, dv=512, mq=768, mk=1024, mv=1280, mi=1536, mf=1540, mo=1544, gq=1800,
             gk=1928, gv=2056, ga=2312, sq=2328, sk=2584, sv=2840, z=3096, end=4120)
N_PROJ = 4224
C_DQ, C_DK, C_DV, C_MQK, C_MV, C_MO, C_GQK, C_GV, C_SQ, C_SK, C_SV, C_Z, C_SM = (
    0, 256, 512, 768, 1280, 1536, 1792, 2048, 2304, 2560, 2816, 3072, 4096)
SM_I, SM_F, SM_A = 0, 4, 8

ATT_TQ = 512
DIFF_TK = 1024
SB_TK = 256
REC_TB = 512
MLSTM_TB = 256
REC_L = 128
GLA_R = 32


def _cparams(sem):
    return pltpu.CompilerParams(dimension_semantics=sem, vmem_limit_bytes=VMEM_LIMIT)


def _mm(a, b):
    return jnp.dot(a.astype(MXU_DTYPE), b.astype(MXU_DTYPE), preferred_element_type=f32)


def _mm_nt(a, b):
    return lax.dot_general(a.astype(MXU_DTYPE), b.astype(MXU_DTYPE), (((1,), (1,)), ((), ())),
                           preferred_element_type=f32)


def _mm_tn(a, b):
    return lax.dot_general(a.astype(MXU_DTYPE), b.astype(MXU_DTYPE), (((0,), (0,)), ((), ())),
                           preferred_element_type=f32)


def _split_terms(a, n):
    if MXU_DTYPE == f32:
        return [a]
    out, r = [], a
    for i in range(n - 1):
        top = lax.bitcast_convert_type(lax.bitcast_convert_type(r, jnp.int32) & jnp.int32(-65536), f32)
        out.append(top.astype(MXU_DTYPE))
        r = r - top
    out.append(r.astype(MXU_DTYPE))
    return out


def _sel_mm(a, sel, n):
    acc = None
    for p in _split_terms(a, n):
        t = jnp.dot(p, sel, preferred_element_type=f32)
        acc = t if acc is None else acc + t
    return acc


def _mm_sel(sel, a, n):
    acc = None
    for p in _split_terms(a, n):
        t = jnp.dot(sel, p, preferred_element_type=f32)
        acc = t if acc is None else acc + t
    return acc


def _rms(x, g):
    return x * lax.rsqrt(jnp.mean(x * x, axis=-1, keepdims=True) + EPS) * g


def _sigmoid(x):
    return jax.nn.sigmoid(x)


def _softplus_neg_abs(x):
    return jnp.log(1.0 + jnp.exp(-jnp.abs(x)))


def _log_sigmoid(x):
    return jnp.minimum(x, 0.0) - _softplus_neg_abs(x)


def _head_norm(y, g_ref):
    outs = []
    for h in range(N_HEADS):
        yh = y[:, h * HEAD_DIM:(h + 1) * HEAD_DIM]
        outs.append(_rms(yh, g_ref[:, h * HEAD_DIM:(h + 1) * HEAD_DIM]))
    return jnp.concatenate(outs, axis=1)


def _store_heads(o_ref, t, fill):
    rows = t.shape[0]
    tail = jnp.where(lax.broadcasted_iota(jnp.int32, (rows, LANES - HEAD_DIM), 1) == 0, fill, 0.0)
    for h in range(N_HEADS):
        o_ref[h] = jnp.concatenate([t[:, h * HEAD_DIM:(h + 1) * HEAD_DIM], tail], axis=1).astype(o_ref.dtype)


def _in_proj_body(rows_by_feature, x_ref, g_ref, w_ref, dqb, dk, dkb, dv, dvh, mqk, mv, mo, gqk, gv, sqb, sk, skb, sv,
                  svh, z, small):
    hb = _rms(x_ref[...], g_ref[...]).astype(MXU_DTYPE)

    def proj(a, b):
        return jnp.dot(hb, w_ref[:, a:b], preferred_element_type=f32)

    def store_rows(o_ref, t):
        o_ref[...] = t.T if rows_by_feature else t

    dqb[...] = (proj(C_DQ, C_DK) * DIFF_D ** -0.5).astype(dqb.dtype)
    t = proj(C_DK, C_DV)
    store_rows(dk, t)
    dkb[...] = t.astype(dkb.dtype)
    t = proj(C_DV, C_MQK)
    store_rows(dv, t)
    _store_heads(dvh, t, 1.0)
    mqk[...] = proj(C_MQK, C_MV)
    mv[...] = proj(C_MV, C_MO)
    mo[...] = proj(C_MO, C_GQK)
    gqk[...] = proj(C_GQK, C_GV)
    gv[...] = proj(C_GV, C_SQ)
    sqb[...] = (proj(C_SQ, C_SK) * HEAD_DIM ** -0.5).astype(sqb.dtype)
    t = proj(C_SK, C_SV)
    store_rows(sk, t)
    skb[...] = t.astype(skb.dtype)
    t = proj(C_SV, C_Z)
    store_rows(sv, t)
    _store_heads(svh, t, 0.0)
    z[...] = proj(C_Z, C_SM)
    small[...] = proj(C_SM, N_PROJ)


def in_proj(x, g_pre, w_perm, seq_len=None):
    R = x.shape[0]
    tm = min(R, 512)
    HM = "head-major"
    KV = "kv-rows"
    widths = [(256, MXU_DTYPE), (KV, f32), (256, MXU_DTYPE), (KV, f32), (HM, MXU_DTYPE), (512, f32), (256, f32),
              (256, f32), (256, f32), (256, f32), (256, MXU_DTYPE), (KV, f32), (256, MXU_DTYPE), (KV, f32),
              (HM, MXU_DTYPE), (1024, f32), (LANES, f32)]
    if seq_len is not None:
        assert seq_len % tm == 0 and R % seq_len == 0
        nt = seq_len // tm

    def row(n):
        if n == HM:
            return pl.BlockSpec((N_HEADS, tm, LANES), lambda i: (0, i, 0))
        if n == KV and seq_len is not None:
            return pl.BlockSpec((None, BRANCH_W, tm), lambda i: (i // nt, 0, i % nt))
        return pl.BlockSpec((tm, BRANCH_W if n == KV else n), lambda i: (i, 0))

    def shape(n):
        if n == HM:
            return (N_HEADS, R, LANES)
        if n == KV:
            return (R // seq_len, BRANCH_W, seq_len) if seq_len is not None else (R, BRANCH_W)
        return (R, n)

    const = lambda s: pl.BlockSpec(s, lambda i: (0, 0), pipeline_mode=pl.Buffered(1))
    return pl.pallas_call(
        functools.partial(_in_proj_body, seq_len is not None),
        grid=(R // tm,),
        in_specs=[row(D_MODEL), const((1, D_MODEL)), const((D_MODEL, N_PROJ))],
        out_specs=[row(n) for n, _ in widths],
        out_shape=[jax.ShapeDtypeStruct(shape(n), d) for n, d in widths],
        compiler_params=_cparams(("arbitrary",)),
        name="in_proj",
    )(x, g_pre.reshape(1, D_MODEL), w_perm)


def _out_proj_body(x_ref, p_ref, ya, ym, yg, ys, z_ref, gpre, gpost, wg, wb, wo, wple, wpg, o_ref):
    x = x_ref[...]
    hb = _rms(x, gpre[...]).astype(MXU_DTYPE)
    acc = None
    for n, y in enumerate((ya, ym, yg, ys)):
        zz = z_ref[:, n * BRANCH_W:(n + 1) * BRANCH_W]
        br = y[...] * (zz * _sigmoid(zz))
        pb = _mm(br, wb[n])
        gt = _sigmoid(jnp.dot(hb, wg[:, n * D_MODEL:(n + 1) * D_MODEL], preferred_element_type=f32))
        acc = gt * pb if acc is None else acc + gt * pb
    x1 = x + _rms(_mm(acc, wo[...]), gpost[...])
    o_ref[...] = x1 + _sigmoid(_mm(x1, wpg[...])) * _mm(p_ref[...], wple[...])


def out_proj(x, p, ya, ym, yg, ys, z, g_pre, g_post, w_gate, w_branch, w_out, w_ple, w_ple_gate):
    R = x.shape[0]
    tm = min(R, 256)
    row = lambda n: pl.BlockSpec((tm, n), lambda i: (i, 0))
    const = lambda s: pl.BlockSpec(s, lambda i: (0,) * len(s), pipeline_mode=pl.Buffered(1))
    return pl.pallas_call(
        _out_proj_body,
        grid=(R // tm,),
        in_specs=[row(D_MODEL), row(P_DIM), row(BRANCH_W), row(BRANCH_W), row(BRANCH_W), row(BRANCH_W), row(D_MODEL),
                  const((1, D_MODEL)), const((1, D_MODEL)), const((D_MODEL, N_BRANCH * D_MODEL)),
                  const((N_BRANCH, BRANCH_W, D_MODEL)), const((D_MODEL, D_MODEL)), const((P_DIM, D_MODEL)),
                  const((D_MODEL, D_MODEL))],
        out_specs=row(D_MODEL),
        out_shape=jax.ShapeDtypeStruct((R, D_MODEL), f32),
        compiler_params=_cparams(("arbitrary",)),
        name="out_proj",
    )(x, p, ya, ym, yg, ys, z, g_pre.reshape(1, D_MODEL), g_post.reshape(1, D_MODEL), w_gate, w_branch, w_out,
      w_ple, w_ple_gate)


def _diff_lambda(dl_ref, lam_init):
    dl = dl_ref[...]
    return (jnp.exp(jnp.sum(dl[0:1] * dl[1:2], keepdims=True)) - jnp.exp(jnp.sum(dl[2:3] * dl[3:4], keepdims=True))
            + lam_init)


def _diff_masks(q):
    lane = lax.broadcasted_iota(jnp.int32, (1, BRANCH_W), 1)
    out = []
    for h in range(N_HEADS):
        for m in range(2):
            lo = h * HEAD_DIM + m * DIFF_D
            out.append(jnp.where((lane >= lo) & (lane < lo + DIFF_D), q, jnp.zeros_like(q)))
    return out


def _diff_attn_body(lam_init, q_ref, k_ref, v_ref, dl_ref, g_ref, o_ref, qm_scr, m_scr, acc_scr):
    i = pl.program_id(1)
    tq, big = ATT_TQ, DIFF_TK // ATT_TQ
    for idx, qq in enumerate(_diff_masks(q_ref[0])):
        qm_scr[idx] = qq
    m_scr[...] = jnp.full(m_scr.shape, -jnp.inf, f32)
    acc_scr[...] = jnp.zeros(acc_scr.shape, f32)

    def chunk(k0, width, masked):
        k = k_ref[0, pl.ds(k0, width), :]
        if masked:
            valid = (lax.broadcasted_iota(jnp.int32, (tq, width), 1) + k0
                     <= lax.broadcasted_iota(jnp.int32, (tq, width), 0) + i * tq)

        def head(h, carry):
            v = v_ref[h, pl.ds(k0, width), :]
            scores = [_mm_nt(qm_scr[2 * h + m], k) for m in range(2)]
            for m, s in enumerate(scores):
                idx = 2 * h + m
                if masked:
                    s = jnp.where(valid, s, -jnp.inf)
                m_old = m_scr[idx]
                m_new = jnp.maximum(m_old, jnp.max(s, axis=1, keepdims=True))
                acc_scr[idx] = jnp.exp(m_old - m_new) * acc_scr[idx] + _mm(jnp.exp(s - m_new), v)
                m_scr[idx] = m_new
            return carry

        lax.fori_loop(0, N_HEADS, head, 0)

    @pl.when(i < big - 1)
    def _():
        for r in range(big - 1):
            @pl.when(i > r)
            def _():
                chunk(r * tq, tq, False)
        chunk(pl.multiple_of(i * tq, tq), tq, True)

    @pl.when(i >= big - 1)
    def _():
        chunk(pl.multiple_of((i + 1 - big) * tq, tq), DIFF_TK, True)
        below = i + 1 - big
        n_big = below // big
        first = (below % big) * tq
        lax.fori_loop(0, n_big, lambda j, c: (chunk(pl.multiple_of(first + j * DIFF_TK, tq), DIFF_TK, False), c)[1],
                      0)
        for r in range(big - 1):
            @pl.when(below % big > r)
            def _():
                chunk(r * tq, tq, False)

    lam = _diff_lambda(dl_ref, lam_init)
    outs = []
    for h in range(N_HEADS):
        a1, a2 = acc_scr[2 * h], acc_scr[2 * h + 1]
        o1 = a1[:, :HEAD_DIM] * (1.0 / a1[:, HEAD_DIM:HEAD_DIM + 1])
        o2 = a2[:, :HEAD_DIM] * (1.0 / a2[:, HEAD_DIM:HEAD_DIM + 1])
        outs.append(o1 - lam * o2)
    o_ref[0] = _head_norm(jnp.concatenate(outs, axis=1), g_ref) * (1.0 - lam_init)


def diff_attn(lam_init, qb, kb, vh, diff_lambda, norm_g):
    B, T, _ = qb.shape
    tq = ATT_TQ
    return pl.pallas_call(
        functools.partial(_diff_attn_body, lam_init),
        grid=(B, T // tq),
        in_specs=[pl.BlockSpec((1, tq, BRANCH_W), lambda b, i: (b, i, 0)),
                  pl.BlockSpec((1, T, BRANCH_W), lambda b, i: (b, 0, 0)),
                  pl.BlockSpec((N_HEADS, None, T, LANES), lambda b, i: (0, b, 0, 0)),
                  pl.BlockSpec((4, DIFF_D), lambda b, i: (0, 0)),
                  pl.BlockSpec((1, BRANCH_W), lambda b, i: (0, 0))],
        out_specs=pl.BlockSpec((1, tq, BRANCH_W), lambda b, i: (b, i, 0)),
        out_shape=jax.ShapeDtypeStruct((B, T, BRANCH_W), f32),
        scratch_shapes=[pltpu.VMEM((2 * N_HEADS, tq, BRANCH_W), MXU_DTYPE), pltpu.VMEM((2 * N_HEADS, tq, 1), f32),
                        pltpu.VMEM((2 * N_HEADS, tq, LANES), f32)],
        compiler_params=_cparams(("arbitrary", "arbitrary")),
        name="diff_attn",
    )(qb, kb, vh, diff_lambda, norm_g.reshape(1, BRANCH_W))


def _head_masks(q):
    lane = lax.broadcasted_iota(jnp.int32, (1, BRANCH_W), 1)
    return [jnp.where((lane >= h * HEAD_DIM) & (lane < (h + 1) * HEAD_DIM), q, jnp.zeros_like(q))
            for h in range(N_HEADS)]


def _suffix_matrix(n):
    s = np.arange(n)[:, None]
    j = np.arange(n)[None, :]
    return np.concatenate([(s > j).astype(np.float32), np.ones((n, LANES), np.float32)], axis=1)


def _sb_group(z, c, u, valid):
    n = z.shape[1]
    sp = _softplus_neg_abs(z)
    ls = jnp.minimum(z, 0.0) - sp
    lf = -jnp.maximum(z, 0.0) - sp
    if valid is not None:
        lf = jnp.where(valid, lf, 0.0)
    r = _sel_mm(lf, u, 2)
    a = jnp.exp(ls + r[:, :n] + jnp.concatenate([c] * (n // LANES), axis=1))
    if valid is not None:
        a = jnp.where(valid, a, 0.0)
    return a, c + r[:, n:]


def _sb_attn_body(q_ref, k_ref, v_ref, u_ref, o_ref, qm_scr, c_scr, acc_scr):
    i = pl.program_id(1)
    tq, tk = ATT_TQ, SB_TK
    for h, qq in enumerate(_head_masks(q_ref[0])):
        qm_scr[h] = qq
    c_scr[...] = jnp.zeros(c_scr.shape, f32)
    acc_scr[...] = jnp.zeros(acc_scr.shape, f32)
    row = lax.broadcasted_iota(jnp.int32, (tq, tk), 0) + i * tq
    col = lax.broadcasted_iota(jnp.int32, (tq, tk), 1)
    u = u_ref[...]

    def group(g, masked):
        k0 = pl.multiple_of(g * tk, tk)
        k = k_ref[0, pl.ds(k0, tk), :]
        valid = (col + k0 < row) if masked else None

        zs = [_mm_nt(qm_scr[h], k) for h in range(N_HEADS)]
        for h in range(N_HEADS):
            a, c_new = _sb_group(zs[h], c_scr[h], u, valid)
            acc_scr[h] = acc_scr[h] + _mm(a, v_ref[h, pl.ds(k0, tk), :])
            c_scr[h] = c_new

    n_diag = tq // tk
    for d in range(n_diag):
        group((i + 1) * n_diag - 1 - d, True)
    lax.fori_loop(0, i * n_diag, lambda n, c: (group(i * n_diag - 1 - n, False), c)[1], 0)
    o_ref[0] = jnp.concatenate([acc_scr[h][:, :HEAD_DIM] for h in range(N_HEADS)], axis=1)


def sb_attn(qb, kb, vh, u):
    B, T, _ = qb.shape
    tq = ATT_TQ
    return pl.pallas_call(
        _sb_attn_body,
        grid=(B, T // tq),
        in_specs=[pl.BlockSpec((1, tq, BRANCH_W), lambda b, i: (b, i, 0)),
                  pl.BlockSpec((1, T, BRANCH_W), lambda b, i: (b, 0, 0)),
                  pl.BlockSpec((N_HEADS, None, T, LANES), lambda b, i: (0, b, 0, 0)),
                  pl.BlockSpec((SB_TK, SB_TK + LANES), lambda b, i: (0, 0))],
        out_specs=pl.BlockSpec((1, tq, BRANCH_W), lambda b, i: (b, i, 0)),
        out_shape=jax.ShapeDtypeStruct((B, T, BRANCH_W), f32),
        scratch_shapes=[pltpu.VMEM((N_HEADS, tq, BRANCH_W), MXU_DTYPE), pltpu.VMEM((N_HEADS, tq, LANES), f32),
                        pltpu.VMEM((N_HEADS, tq, LANES), f32)],
        compiler_params=_cparams(("arbitrary", "arbitrary")),
        name="sb_attn",
    )(qb, kb, vh, u)


def _conv_taps(cw_ref, cb_ref, rows):
    y = cb_ref[...]
    for j in range(CONV_W):
        y = y + rows[j] * cw_ref[j:j + 1, :]
    return y * _sigmoid(y)


def _mlstm_body(mqk_ref, mv_ref, mo_ref, sm_ref, cw_ref, cb_ref, bif_ref, g_ref, tril_ref,
                ym_ref, C_ref, n_ref, m_ref, conv_ref, xc_scr, q_scr, k_scr, C_scr, n_scr, m_scr):
    t = pl.program_id(0)
    NB, TB, L = mqk_ref.shape[0], MLSTM_TB, REC_L
    pad = SUBLANES

    @pl.when(t == 0)
    def _():
        xc_scr[:, 0:pad, :] = jnp.zeros((NB, pad, 2 * BRANCH_W), f32)
        C_scr[...] = jnp.zeros(C_scr.shape, f32)
        n_scr[...] = jnp.zeros(n_scr.shape, f32)
        m_scr[...] = jnp.zeros(m_scr.shape, f32)

    for b in range(NB):
        xc_scr[b, pad:pad + TB, :] = mqk_ref[b]
        act = _conv_taps(cw_ref, cb_ref, [xc_scr[b, pad - (CONV_W - 1) + j:pad - (CONV_W - 1) + j + TB, :]
                                          for j in range(CONV_W)])
        q_scr[b] = act[:, :BRANCH_W]
        k_scr[b] = act[:, BRANCH_W:] * HEAD_DIM ** -0.5
        conv_ref[b] = xc_scr[b, pad + TB - (CONV_W - 1):pad + TB, :]
        xc_scr[b, 0:pad, :] = xc_scr[b, TB:TB + pad, :]

    tril = tril_ref[...]
    tri_mask = lax.broadcasted_iota(jnp.int32, (L, L), 1) <= lax.broadcasted_iota(jnp.int32, (L, L), 0)
    chains = [(b, h) for b in range(NB) for h in range(N_HEADS)]
    hsl = lambda h: slice(h * HEAD_DIM, (h + 1) * HEAD_DIM)
    for c in range(TB // L):
        rs = slice(c * L, (c + 1) * L)
        gi = [sm_ref[b, rs, :] + bif_ref[...] for b in range(NB)]
        bcum = [_mm_sel(tril, _log_sigmoid(g), 3) for g in gi]
        bT = [x.T for x in bcum]
        iT = [g.T for g in gi]
        kT = [k_scr[b, rs, :].T for b in range(NB)]
        qh = {(b, h): q_scr[b, rs, hsl(h)] for b, h in chains}
        kh = {(b, h): k_scr[b, rs, hsl(h)] for b, h in chains}
        vh = {(b, h): mv_ref[b, rs, hsl(h)] for b, h in chains}
        qk = {ch: _mm_nt(qh[ch], kh[ch]) for ch in chains}
        qC = {(b, h): _mm(qh[b, h], C_scr[b, h]) for b, h in chains}
        bcol = {(b, h): bcum[b][:, SM_F + h:SM_F + h + 1] for b, h in chains}
        brow = {(b, h): bT[b][SM_F + h:SM_F + h + 1, :] for b, h in chains}
        irow = {(b, h): iT[b][SM_I + h:SM_I + h + 1, :] for b, h in chains}
        icol = {(b, h): gi[b][:, SM_I + h:SM_I + h + 1] for b, h in chains}
        m_old = {(b, h): m_scr[b, h] for b, h in chains}
        dmat = {ch: jnp.where(tri_mask, bcol[ch] - brow[ch] + irow[ch], -jnp.inf) for ch in chains}
        inter = {ch: bcol[ch] + m_old[ch] for ch in chains}
        mt = {ch: jnp.maximum(inter[ch], jnp.max(dmat[ch], axis=1, keepdims=True)) for ch in chains}
        w = {ch: qk[ch] * jnp.exp(dmat[ch] - mt[ch]) for ch in chains}
        wv = {ch: _mm(w[ch], vh[ch]) for ch in chains}
        bl = {ch: bcol[ch][L - 1:L, :] for ch in chains}
        ws = {ch: bl[ch] - bcol[ch] + icol[ch] for ch in chains}
        m_new = {ch: jnp.maximum(bl[ch] + m_old[ch], jnp.max(ws[ch], axis=0, keepdims=True)) for ch in chains}
        sc = {ch: jnp.exp(bl[ch] + m_old[ch] - m_new[ch]) for ch in chains}
        ek = {ch: kh[ch] * jnp.exp(ws[ch] - m_new[ch]) for ch in chains}
        ekT = {(b, h): kT[b][hsl(h), :] * jnp.exp(bl[b, h] - brow[b, h] + irow[b, h] - m_new[b, h]) for b, h in chains}
        kv = {ch: _mm(ekT[ch], vh[ch]) for ch in chains}
        outs = {}
        for b, h in chains:
            si = jnp.exp(inter[b, h] - mt[b, h])
            num = si * qC[b, h] + wv[b, h]
            den = (si * jnp.sum(qh[b, h] * n_scr[b, h], axis=1, keepdims=True)
                   + jnp.sum(w[b, h], axis=1, keepdims=True))
            outs[b, h] = num / jnp.maximum(jnp.abs(den), jnp.exp(-mt[b, h]))
        for b, h in chains:
            C_scr[b, h] = sc[b, h] * C_scr[b, h] + kv[b, h]
            n_scr[b, h] = sc[b, h] * n_scr[b, h] + jnp.sum(ek[b, h], axis=0, keepdims=True)
            m_scr[b, h] = m_new[b, h]
        for b in range(NB):
            ym_ref[b, rs, :] = (_head_norm(jnp.concatenate([outs[b, h] for h in range(N_HEADS)], axis=1), g_ref)
                                * _sigmoid(mo_ref[b, rs, :]))

    C_ref[...] = C_scr[...]
    for b in range(NB):
        n_ref[b] = jnp.concatenate([n_scr[b, h] for h in range(N_HEADS)], axis=1)
        m_ref[b] = jnp.concatenate([jnp.broadcast_to(m_scr[b, h], (1, LANES // N_HEADS)) for h in range(N_HEADS)],
                                   axis=1)


def mlstm_prompt(mqk, mv, mo, small, conv_w, conv_b, bif_row, norm_g, tril):
    B, T, _ = mqk.shape
    TB = MLSTM_TB
    blk = lambda n: pl.BlockSpec((B, TB, n), lambda t: (0, t, 0))
    const = lambda s: pl.BlockSpec(s, lambda t: (0,) * len(s))
    return pl.pallas_call(
        _mlstm_body,
        grid=(T // TB,),
        in_specs=[blk(2 * BRANCH_W), blk(BRANCH_W), blk(BRANCH_W), blk(LANES), const((CONV_W, 2 * BRANCH_W)),
                  const((1, 2 * BRANCH_W)), const((1, LANES)), const((1, BRANCH_W)), const((REC_L, REC_L))],
        out_specs=[blk(BRANCH_W), const((B, N_HEADS, HEAD_DIM, HEAD_DIM)), const((B, 1, BRANCH_W)),
                   const((B, 1, LANES)), const((B, CONV_W - 1, 2 * BRANCH_W))],
        out_shape=[jax.ShapeDtypeStruct((B, T, BRANCH_W), f32),
                   jax.ShapeDtypeStruct((B, N_HEADS, HEAD_DIM, HEAD_DIM), f32),
                   jax.ShapeDtypeStruct((B, 1, BRANCH_W), f32),
                   jax.ShapeDtypeStruct((B, 1, LANES), f32),
                   jax.ShapeDtypeStruct((B, CONV_W - 1, 2 * BRANCH_W), f32)],
        scratch_shapes=[pltpu.VMEM((B, TB + SUBLANES, 2 * BRANCH_W), f32), pltpu.VMEM((B, TB, BRANCH_W), f32),
                        pltpu.VMEM((B, TB, BRANCH_W), f32), pltpu.VMEM((B, N_HEADS, HEAD_DIM, HEAD_DIM), f32),
                        pltpu.VMEM((B, N_HEADS, 1, HEAD_DIM), f32), pltpu.VMEM((B, N_HEADS, 1, 1), f32)],
        compiler_params=_cparams(("arbitrary",)),
        name="mlstm_prompt",
    )(mqk, mv, mo, small, conv_w, conv_b.reshape(1, -1), bif_row, norm_g.reshape(1, BRANCH_W), tril)


def _gla_log_decay(sm, wa_ref, ba_ref):
    return _log_sigmoid(_mm(sm, wa_ref[...]) + ba_ref[...]) * (1.0 / GLA_TAU)


def _gla_body(gqk_ref, gv_ref, sm_ref, wa_ref, ba_ref, g_ref, lb_ref, ex_ref, yg_ref, S_ref, S_scr):
    t = pl.program_id(1)
    TB, L, R = REC_TB, REC_L, GLA_R
    NK = N_HEADS * GLA_DK

    @pl.when(t == 0)
    def _():
        S_scr[...] = jnp.zeros(S_scr.shape, f32)

    lb = lb_ref[...]
    ex = ex_ref[...]
    s_i = lax.broadcasted_iota(jnp.int32, (R, R, NK), 0)
    t_i = lax.broadcasted_iota(jnp.int32, (R, R, NK), 1)
    causal3 = t_i >= s_i
    lane_t = lax.broadcasted_iota(jnp.int32, (1, L), 1)
    bd_mask = (lax.broadcasted_iota(jnp.int32, (NK, BRANCH_W), 0) // GLA_DK
               == lax.broadcasted_iota(jnp.int32, (NK, BRANCH_W), 1) // HEAD_DIM)
    for c in range(TB // L):
        rs = slice(c * L, (c + 1) * L)
        la = _gla_log_decay(sm_ref[0, rs, :], wa_ref, ba_ref)
        bcl = _mm_sel(lb, la, 3)
        q_c = gqk_ref[0, rs, 0:NK] * GLA_DK ** -0.5
        k_c = gqk_ref[0, rs, NK:2 * NK]
        v_c = gv_ref[0, rs, :]
        bll = jnp.concatenate([jnp.broadcast_to(bcl[(I + 1) * R - 1:(I + 1) * R, :], (R, NK)) for I in range(L // R)],
                              axis=0)
        ktilT = (k_c * jnp.exp(bll - bcl)).T
        dblT = jnp.exp(bll).T
        qin = q_c * jnp.exp(bcl)
        outs = []
        for I in range(L // R):
            sl = slice(I * R, (I + 1) * R)
            S = S_scr[...]
            bb, vv = bcl[sl], v_c[sl]
            d = jnp.where(causal3, bb[None, :, :] - bb[:, None, :], -jnp.inf)
            p = q_c[sl][None, :, :] * k_c[sl][:, None, :] * jnp.exp(d)
            a = _sel_mm(p.reshape(R * R, NK), ex, 2).reshape(R, R, BRANCH_W)
            outs.append(_mm(qin[sl], S) + jnp.sum(a * vv[:, None, :], axis=0))
            in_blk = (lane_t >= I * R) & (lane_t < (I + 1) * R)
            upd = _mm(jnp.where(in_blk, ktilT, 0.0), v_c)
            S_scr[...] = dblT[:, I * R:I * R + 1] * S + jnp.where(bd_mask, upd, 0.0)
        yg_ref[0, rs, :] = _head_norm(jnp.concatenate(outs, axis=0), g_ref)

    for h in range(N_HEADS):
        S_ref[0, h] = S_scr[h * GLA_DK:(h + 1) * GLA_DK, h * HEAD_DIM:(h + 1) * HEAD_DIM]


def gla_prompt(gqk, gv, small, wa, ba, norm_g, lb, ex):
    B, T, _ = gqk.shape
    TB = REC_TB
    blk = lambda n: pl.BlockSpec((1, TB, n), lambda b, t: (b, t, 0))
    const = lambda s: pl.BlockSpec(s, lambda b, t: (0,) * len(s))
    return pl.pallas_call(
        _gla_body,
        grid=(B, T // TB),
        in_specs=[blk(BRANCH_W), blk(BRANCH_W), blk(LANES), const((LANES, LANES)), const((1, LANES)),
                  const((1, BRANCH_W)), const((REC_L, REC_L)), const((LANES, BRANCH_W))],
        out_specs=[blk(BRANCH_W), pl.BlockSpec((1, N_HEADS, GLA_DK, HEAD_DIM), lambda b, t: (b, 0, 0, 0))],
        out_shape=[jax.ShapeDtypeStruct((B, T, BRANCH_W), f32),
                   jax.ShapeDtypeStruct((B, N_HEADS, GLA_DK, HEAD_DIM), f32)],
        scratch_shapes=[pltpu.VMEM((N_HEADS * GLA_DK, BRANCH_W), f32)],
        compiler_params=_cparams(("arbitrary", "arbitrary")),
        name="gla_prompt",
    )(gqk, gv, small, wa, ba.reshape(1, LANES), norm_g.reshape(1, BRANCH_W), lb, ex)


def _lane_replicated_column(row):
    n = row.shape[1]
    eye = lax.broadcasted_iota(jnp.int32, (n, n), 0) == lax.broadcasted_iota(jnp.int32, (n, n), 1)
    col = jnp.sum(jnp.where(eye, jnp.broadcast_to(row, (n, n)), 0.0), axis=1, keepdims=True)
    return jnp.broadcast_to(col, (n, LANES))


def _rows_per_head(x4):
    return jnp.broadcast_to(x4[:, None, :], (N_HEADS, HEAD_DIM, LANES)).reshape(BRANCH_W, LANES)


def _lanes_per_head(col4):
    sel = (lax.broadcasted_iota(jnp.int32, (N_HEADS, BRANCH_W), 1) // HEAD_DIM
           == lax.broadcasted_iota(jnp.int32, (N_HEADS, BRANCH_W), 0))
    return jnp.sum(jnp.where(sel, col4, 0.0), axis=0, keepdims=True)


def _dec_attn_body(lam_init, n_pages, pt_ref, dq_ref, dkn_ref, dvn_ref, sq_ref, dl_ref, g_ref, u_ref, *rest):
    dk_pg, dv_pg, sk_pg, sv_pg = (rest[j * n_pages:(j + 1) * n_pages] for j in range(4))
    ya_ref, ys_ref = rest[4 * n_pages:]
    lam = _diff_lambda(dl_ref, lam_init)

    q_row = dq_ref[0].astype(f32)
    qx = _lane_replicated_column(q_row)
    s1, s2 = [], []
    for p in range(n_pages):
        r = jnp.sum((dk_pg[p][...] * qx).reshape(N_HEADS, 2, DIFF_D, LANES), axis=2)
        s1.append(r[:, 0, :])
        s2.append(r[:, 1, :])
    lane = lax.broadcasted_iota(jnp.int32, (N_HEADS, BRANCH_W), 1)
    head = lax.broadcasted_iota(jnp.int32, (N_HEADS, BRANCH_W), 0)
    qk_new = jnp.broadcast_to(q_row * dkn_ref[0], (N_HEADS, BRANCH_W))
    w_pages, w_new = [], []
    for m, s in enumerate((s1, s2)):
        lo = head * HEAD_DIM + m * DIFF_D
        s_new = jnp.sum(jnp.where((lane >= lo) & (lane < lo + DIFF_D), qk_new, 0.0), axis=1, keepdims=True)
        smax = s[0]
        for p in range(1, n_pages):
            smax = jnp.maximum(smax, s[p])
        mx = jnp.maximum(jnp.max(smax, axis=1, keepdims=True), s_new)
        e = [jnp.exp(sp - mx) for sp in s]
        e_new = jnp.exp(s_new - mx)
        tot = e[0]
        for p in range(1, n_pages):
            tot = tot + e[p]
        inv = 1.0 / (jnp.sum(tot, axis=1, keepdims=True) + e_new)
        w_pages.append([ep * inv for ep in e])
        w_new.append(e_new * inv)
    acc = jnp.zeros((BRANCH_W, LANES), f32)
    for p in range(n_pages):
        acc = acc + _rows_per_head(w_pages[0][p] - lam * w_pages[1][p]) * dv_pg[p][...]
    y = jnp.sum(acc.T, axis=0, keepdims=True) + _lanes_per_head(w_new[0] - lam * w_new[1]) * dvn_ref[0]
    ya_ref[0] = _head_norm(y, g_ref) * (1.0 - lam_init)

    qx = _lane_replicated_column(sq_ref[0].astype(f32))
    z = jnp.concatenate([jnp.sum((sk_pg[p][...] * qx).reshape(N_HEADS, HEAD_DIM, LANES), axis=1)
                         for p in range(n_pages)], axis=0)
    sp = _softplus_neg_abs(z)
    ls = jnp.minimum(z, 0.0) - sp
    r = _sel_mm(-jnp.maximum(z, 0.0) - sp, u_ref[...], 2)
    base = ls + r[:, :LANES]
    carry = jnp.zeros((N_HEADS, LANES), f32)
    acc = jnp.zeros((BRANCH_W, LANES), f32)
    for p in reversed(range(n_pages)):
        rows = slice(p * N_HEADS, (p + 1) * N_HEADS)
        acc = acc + _rows_per_head(jnp.exp(base[rows] + carry)) * sv_pg[p][...]
        carry = carry + r[rows, LANES:]
    ys_ref[0] = jnp.sum(acc.T, axis=0, keepdims=True)


def _pages_as_feature_by_position(cache):
    d, n, pg, h, hd = cache.shape
    return jnp.transpose(cache, (0, 1, 3, 4, 2)).reshape(d, n, h * hd, pg)


def dec_attn(layer, lam_init, page_table, dqb, dk_new, dv_new, sqb, diff_lambda, norm_g, u, caches_t):
    B, n_pages = page_table.shape
    page = caches_t[0].shape[3]
    assert page == LANES
    row3 = lambda a: a.reshape(B, 1, BRANCH_W)
    rspec = pl.BlockSpec((1, 1, BRANCH_W), lambda b, pt: (b, 0, 0))
    const = lambda shp: pl.BlockSpec(shp, lambda b, pt: (0,) * len(shp))

    def pg(p):
        return pl.BlockSpec((None, None, BRANCH_W, page), lambda b, pt: (layer, pt[b * n_pages + p], 0, 0))

    in_specs = [rspec, rspec, rspec, rspec, const((4, DIFF_D)), const((1, BRANCH_W)), const((LANES, 2 * LANES))]
    args = [row3(dqb), row3(dk_new), row3(dv_new), row3(sqb), diff_lambda, norm_g.reshape(1, BRANCH_W), u]
    for c in caches_t:
        for p in range(n_pages):
            in_specs.append(pg(p))
            args.append(c)
    ya, ys = pl.pallas_call(
        functools.partial(_dec_attn_body, lam_init, n_pages),
        grid_spec=pltpu.PrefetchScalarGridSpec(
            num_scalar_prefetch=1, grid=(B,), in_specs=in_specs,
            out_specs=[pl.BlockSpec((1, 1, BRANCH_W), lambda b, pt: (b, 0, 0))] * 2),
        out_shape=[jax.ShapeDtypeStruct((B, 1, BRANCH_W), f32)] * 2,
        compiler_params=_cparams(("arbitrary",)),
        name="dec_attn",
    )(page_table.reshape(-1), *args)
    return ya.reshape(B, BRANCH_W), ys.reshape(B, BRANCH_W)


def _expand_heads(x4, width):
    return jnp.concatenate([jnp.broadcast_to(x4[:, h:h + 1], (x4.shape[0], width)) for h in range(N_HEADS)], axis=1)


def _dec_rec_body(mqk_ref, mv_ref, mo_ref, sm_ref, gqk_ref, gv_ref, conv_ref, C_ref, n_ref, m_ref, S_ref,
                  cw_ref, cb_ref, bif_ref, mg_ref, wa_ref, ba_ref, gg_ref,
                  ym_ref, yg_ref, conv_o, C_o, n_o, m_o, S_o, CT_scr, ST_scr, numT_scr, oT_scr):
    h = pl.program_id(0)
    W = 2 * BRANCH_W
    NK = N_HEADS * GLA_DK
    u = mqk_ref[...]
    rows = [conv_ref[:, j * W:(j + 1) * W] for j in range(CONV_W - 1)] + [u]
    act = _conv_taps(cw_ref, cb_ref, rows)
    q = act[:, :BRANCH_W]
    k = act[:, BRANCH_W:] * HEAD_DIM ** -0.5
    conv_o[...] = jnp.concatenate(rows[1:], axis=1)

    gi = sm_ref[...] + bif_ref[...]
    ig = gi[:, SM_I:SM_I + N_HEADS]
    lf = _log_sigmoid(gi)[:, SM_F:SM_F + N_HEADS]
    m_old = m_ref[...]
    m_new = jnp.maximum(lf + m_old, ig)
    sc = _expand_heads(jnp.exp(lf + m_old - m_new), HEAD_DIM)
    ek = _expand_heads(jnp.exp(ig - m_new), HEAD_DIM) * k
    n_new = sc * n_ref[...] + ek
    m_o[...] = m_new
    n_o[...] = n_new

    la = _gla_log_decay(sm_ref[...], wa_ref, ba_ref)
    gq = gqk_ref[:, 0:NK] * GLA_DK ** -0.5
    gk = gqk_ref[:, NK:2 * NK]

    r64 = pl.ds(pl.multiple_of(h * HEAD_DIM, HEAD_DIM), HEAD_DIM)
    r32 = pl.ds(pl.multiple_of(h * GLA_DK, GLA_DK), GLA_DK)
    numT_scr[0] = q.T
    numT_scr[1] = ek.T
    numT_scr[2] = mv_ref[...].T
    numT_scr[3] = sc.T
    qT, ekT, vT = numT_scr[0, r64, :], numT_scr[1, r64, :], numT_scr[2, r64, :]
    scT = numT_scr[3, pl.ds(h * HEAD_DIM, 1), :]
    CT_scr[...] = C_ref[...].T
    num = jnp.zeros((HEAD_DIM, LANES), f32)
    for kk in range(HEAD_DIM):
        blk = slice(kk * HEAD_DIM, (kk + 1) * HEAD_DIM)
        new = scT * CT_scr[blk, :] + ekT[kk:kk + 1, :] * vT
        CT_scr[blk, :] = new
        num = num + qT[kk:kk + 1, :] * new
    C_o[...] = CT_scr[...].T
    numT_scr[4, r64, :] = num

    oT_scr[0, 0:NK, :] = gq.T
    oT_scr[0, NK:2 * NK, :] = gk.T
    oT_scr[1, 0:NK, :] = jnp.exp(la).T
    oT_scr[2] = gv_ref[...].T
    gqT, gkT, decT = oT_scr[0, r32, :], oT_scr[0, pl.ds(pl.multiple_of(NK + h * GLA_DK, GLA_DK), GLA_DK), :], \
        oT_scr[1, r32, :]
    gvT = oT_scr[2, r64, :]
    ST_scr[...] = S_ref[...].T
    o = jnp.zeros((HEAD_DIM, LANES), f32)
    for kk in range(GLA_DK):
        blk = slice(kk * HEAD_DIM, (kk + 1) * HEAD_DIM)
        new = decT[kk:kk + 1, :] * ST_scr[blk, :] + gkT[kk:kk + 1, :] * gvT
        ST_scr[blk, :] = new
        o = o + gqT[kk:kk + 1, :] * new
    S_o[...] = ST_scr[...].T
    oT_scr[3, r64, :] = o

    @pl.when(h == N_HEADS - 1)
    def _():
        qn = jnp.concatenate([jnp.sum((q * n_new)[:, g * HEAD_DIM:(g + 1) * HEAD_DIM], axis=1, keepdims=True)
                              for g in range(N_HEADS)], axis=1)
        den = jnp.maximum(jnp.abs(qn), jnp.exp(-m_new))
        hm = numT_scr[4].T / _expand_heads(den, HEAD_DIM)
        ym_ref[...] = _head_norm(hm, mg_ref) * _sigmoid(mo_ref[...])
        yg_ref[...] = _head_norm(oT_scr[3].T, gg_ref)


def dec_rec(mqk, mv, mo, small, gqk, gv, conv, C, n, m, S, conv_w, conv_b, bif_row, mnorm_g, wa, ba, gnorm_g):
    B = mqk.shape[0]
    CW = HEAD_DIM * HEAD_DIM
    SW = GLA_DK * HEAD_DIM
    full = lambda a: pl.BlockSpec(a.shape, lambda h: (0,) * a.ndim)
    ins = [mqk, mv, mo, small, gqk, gv, conv.reshape(B, -1), C.reshape(B, N_HEADS * CW), n.reshape(B, BRANCH_W), m,
           S.reshape(B, N_HEADS * SW), conv_w, conv_b.reshape(1, -1), bif_row, mnorm_g.reshape(1, BRANCH_W), wa,
           ba.reshape(1, LANES), gnorm_g.reshape(1, BRANCH_W)]
    in_specs = [full(a) for a in ins]
    in_specs[7] = pl.BlockSpec((B, CW), lambda h: (0, h))
    in_specs[10] = pl.BlockSpec((B, SW), lambda h: (0, h))
    out_shape = [jax.ShapeDtypeStruct((B, BRANCH_W), f32), jax.ShapeDtypeStruct((B, BRANCH_W), f32),
                 jax.ShapeDtypeStruct((B, (CONV_W - 1) * 2 * BRANCH_W), f32),
                 jax.ShapeDtypeStruct((B, N_HEADS * CW), f32), jax.ShapeDtypeStruct((B, BRANCH_W), f32),
                 jax.ShapeDtypeStruct((B, N_HEADS), f32), jax.ShapeDtypeStruct((B, N_HEADS * SW), f32)]
    out_specs = [pl.BlockSpec(s.shape, lambda h: (0, 0)) for s in out_shape]
    out_specs[3] = pl.BlockSpec((B, CW), lambda h: (0, h))
    out_specs[6] = pl.BlockSpec((B, SW), lambda h: (0, h))
    ym, yg, conv_n, C_n, n_n, m_n, S_n = pl.pallas_call(
        _dec_rec_body,
        grid=(N_HEADS,),
        in_specs=in_specs, out_specs=out_specs, out_shape=out_shape,
        scratch_shapes=[pltpu.VMEM((CW, B), f32), pltpu.VMEM((SW, B), f32), pltpu.VMEM((5, BRANCH_W, B), f32),
                        pltpu.VMEM((4, BRANCH_W, B), f32)],
        compiler_params=_cparams(("arbitrary",)),
        name="dec_rec",
    )(*ins)
    return (ym, yg, conv_n.reshape(B, CONV_W - 1, 2 * BRANCH_W), C_n.reshape(B, N_HEADS, HEAD_DIM, HEAD_DIM),
            n_n.reshape(B, N_HEADS, HEAD_DIM), m_n, S_n.reshape(B, N_HEADS, GLA_DK, HEAD_DIM))


def _constants():
    t = np.arange(REC_L)
    tril = (t[None, :] <= t[:, None]).astype(np.float32)
    lb = tril * (t[None, :] // GLA_R == t[:, None] // GLA_R)
    ex = (np.arange(N_HEADS * GLA_DK)[:, None] // GLA_DK == np.arange(BRANCH_W)[None, :] // HEAD_DIM)
    cast = lambda a: jnp.asarray(a, f32).astype(MXU_DTYPE)
    return dict(tril=cast(tril), lb=cast(lb), ex=cast(ex), u_page=cast(_suffix_matrix(LANES)),
                u_sb=cast(_suffix_matrix(SB_TK)))


def _layer_params(l, prm):
    bif = prm['mlstm_b_if'][l].astype(f32).reshape(1, 2 * N_HEADS)
    wa = jnp.zeros((LANES, LANES), f32).at[SM_A:SM_A + GLA_LOWRANK].set(prm['gla_w_a2'][l])
    return dict(
        lam_init=0.8 - 0.6 * math.exp(-0.3 * l),
        bif_row=jnp.pad(bif, ((0, 0), (0, LANES - 2 * N_HEADS))),
        wa=wa.astype(MXU_DTYPE),
    )


def _prep_weights(w_in, w_gate, w_branch, w_out, w_ple, w_ple_gate):
    c = lambda a: a.astype(MXU_DTYPE)
    o = _ORIG
    parts = [w_in[:, :, :o['mi']], w_in[:, :, o['mo']:o['ga']], w_in[:, :, o['sq']:], w_in[:, :, o['mi']:o['mo']],
             w_in[:, :, o['ga']:o['sq']]]
    used = sum(p.shape[2] for p in parts)
    w_perm = jnp.concatenate([c(p) for p in parts]
                             + [jnp.zeros(w_in.shape[:2] + (N_PROJ - used,), MXU_DTYPE)], axis=2)
    return w_perm, c(w_gate), c(w_branch), c(w_out), c(w_ple), c(w_ple_gate)


def _prompt_layer(l, x, p_l, prm, wts, consts):
    B, T, _ = x.shape
    lp = _layer_params(l, prm)
    w_perm, w_gate, w_branch, w_out, w_ple, w_ple_gate = wts
    x2 = x.reshape(B * T, D_MODEL)
    (dqb, dk, dkb, dv, dvh, mqk, mv, mo, gqk, gv, sqb, sk, skb, sv, svh, z, small) = in_proj(
        x2, prm['g_pre'][l], w_perm[l], seq_len=T)
    r3 = lambda a: a.reshape(B, T, a.shape[-1])
    hm = lambda a: a.reshape(N_HEADS, B, T, LANES)
    ya = diff_attn(lp['lam_init'], r3(dqb), r3(dkb), hm(dvh), prm['diff_lambda'][l], prm['diff_norm_g'][l])
    ys = sb_attn(r3(sqb), r3(skb), hm(svh), consts['u_sb'])
    ym, C1, n1, m1, conv1 = mlstm_prompt(r3(mqk), r3(mv), r3(mo), r3(small), prm['mlstm_conv_w'][l],
                                         prm['mlstm_conv_b'][l], lp['bif_row'], prm['mlstm_norm_g'][l],
                                         consts['tril'])
    yg, S1 = gla_prompt(r3(gqk), r3(gv), r3(small), lp['wa'], _pad_ba(prm['gla_b_a'][l]), prm['gla_norm_g'][l],
                        consts['lb'], consts['ex'])
    flat = lambda a: a.reshape(B * T, BRANCH_W)
    y = out_proj(x2, p_l.reshape(B * T, P_DIM), flat(ya), flat(ym), flat(yg), flat(ys), z, prm['g_pre'][l],
                 prm['g_post'][l], w_gate[l], w_branch[l], w_out[l], w_ple[l], w_ple_gate[l])
    hd = lambda a: jnp.transpose(a.reshape(B, N_HEADS, HEAD_DIM, T), (0, 3, 1, 2))
    state = (hd(dk), hd(dv), hd(sk), hd(sv), C1, n1.reshape(B, N_HEADS, HEAD_DIM),
             m1[:, 0, ::LANES // N_HEADS], conv1, S1)
    return y.reshape(B, T, D_MODEL), state


def _pad_ba(ba):
    return ba.astype(f32)


def _decode_layer(l, x, p_l, prm, wts, consts, caches, states, page_table):
    B = x.shape[0]
    lp = _layer_params(l, prm)
    w_perm, w_gate, w_branch, w_out, w_ple, w_ple_gate = wts
    x2 = x.reshape(B, D_MODEL)
    (dqb, dk, dkb, dv, dvh, mqk, mv, mo, gqk, gv, sqb, sk, skb, sv, svh, z, small) = in_proj(
        x2, prm['g_pre'][l], w_perm[l])
    ya, ys = dec_attn(l, lp['lam_init'], page_table, dqb, dk, dv, sqb, prm['diff_lambda'][l], prm['diff_norm_g'][l],
                      consts['u_page'], caches)
    C0, n0, m0, conv0, S0 = states
    ym, yg, conv1, C1, n1, m1, S1 = dec_rec(mqk, mv, mo, small, gqk, gv, conv0[l], C0[l], n0[l], m0[l], S0[l],
                                            prm['mlstm_conv_w'][l], prm['mlstm_conv_b'][l], lp['bif_row'],
                                            prm['mlstm_norm_g'][l], lp['wa'], _pad_ba(prm['gla_b_a'][l]),
                                            prm['gla_norm_g'][l])
    y = out_proj(x2, p_l.reshape(B, P_DIM), ya, ym, yg, ys, z, prm['g_pre'][l], prm['g_post'][l], w_gate[l],
                 w_branch[l], w_out[l], w_ple[l], w_ple_gate[l])
    hd = lambda a: a.reshape(B, 1, N_HEADS, HEAD_DIM)
    state = (hd(dk), hd(dv), hd(sk), hd(sv), C1, n1, m1, conv1, S1)
    return y.reshape(B, 1, D_MODEL), state


def kernel(x_prompt, x_sample, cache_diff_k, cache_diff_v, cache_sb_k, cache_sb_v, state_mlstm_C, state_mlstm_n,
           state_mlstm_m, state_mlstm_conv, state_gla_S, page_table, p_prompt, p_sample, g_pre, g_post, w_in,
           diff_lambda, diff_norm_g, mlstm_conv_w, mlstm_conv_b, mlstm_b_if, mlstm_norm_g, gla_w_a2, gla_b_a,
           gla_norm_g, w_branch, w_gate, w_out, w_ple, w_ple_gate):
    prm = dict(g_pre=g_pre, g_post=g_post, diff_lambda=diff_lambda, diff_norm_g=diff_norm_g,
               mlstm_conv_w=mlstm_conv_w, mlstm_conv_b=mlstm_conv_b, mlstm_b_if=mlstm_b_if,
               mlstm_norm_g=mlstm_norm_g, gla_w_a2=gla_w_a2, gla_b_a=gla_b_a, gla_norm_g=gla_norm_g)
    wts = _prep_weights(w_in, w_gate, w_branch, w_out, w_ple, w_ple_gate)
    consts = _constants()
    depth = w_in.shape[0]
    caches = tuple(_pages_as_feature_by_position(c) for c in (cache_diff_k, cache_diff_v, cache_sb_k, cache_sb_v))
    states = (state_mlstm_C, state_mlstm_n, state_mlstm_m, state_mlstm_conv, state_gla_S)
    y_p, y_s = x_prompt, x_sample
    st_p, st_s = [], []
    for l in range(depth):
        y_p, s_p = _prompt_layer(l, y_p, p_prompt[l], prm, wts, consts)
        y_s, s_s = _decode_layer(l, y_s, p_sample[l], prm, wts, consts, caches, states, page_table)
        st_p.append(s_p)
        st_s.append(s_s)
    outs_p = [jnp.stack(t) for t in zip(*st_p)]
    outs_s = [jnp.stack(t) for t in zip(*st_s)]
    return (y_p, y_s, *outs_p, *outs_s)
```

```python
import functools
import math

import numpy as np
import jax
import jax.numpy as jnp
from jax import lax
from jax.experimental import pallas as pl
from jax.experimental.pallas import tpu as pltpu

f32 = jnp.float32
MXU_DTYPE = jnp.bfloat16

D_MODEL = 1024
N_BRANCH = 4
BRANCH_W = 256
N_HEADS = 4
HEAD_DIM = 64
DIFF_D = 32
GLA_DK = 32
GLA_LOWRANK = 16
GLA_TAU = 16.0
CONV_W = 4
EPS = 1e-6
P_DIM = 256
LOG2E = math.log2(math.e)
LN2 = math.log(2.0)

LANES = 128
SUBLANES = 8
VMEM_LIMIT = 56 * 1024 * 1024

_ORIG = dict(dq=0, dk=256, dv=512, mq=768, mk=1024, mv=1280, mi=1536, mf=1540, mo=1544, gq=1800,
             gk=1928, gv=2056, ga=2312, sq=2328, sk=2584, sv=2840, z=3096, end=4120)
N_PROJ = 4224
C_DQ, C_DK, C_DV, C_MQK, C_MV, C_MO, C_GQK, C_GV, C_SQ, C_SK, C_SV, C_Z, C_SM = (
    0, 256, 512, 768, 1280, 1536, 1792, 2048, 2304, 2560, 2816, 3072, 4096)
SM_I, SM_F, SM_A = 0, 4, 8

ATT_TQ = 512
DIFF_TK = 1024
SB_TK = 256
REC_TB = 512
MLSTM_TB = 256
REC_L = 128
GLA_R = 32


def _cparams(sem):
    return pltpu.CompilerParams(dimension_semantics=sem, vmem_limit_bytes=VMEM_LIMIT)


def _mm(a, b):
    return jnp.dot(a.astype(MXU_DTYPE), b.astype(MXU_DTYPE), preferred_element_type=f32)


def _mm_nt(a, b):
    return lax.dot_general(a.astype(MXU_DTYPE), b.astype(MXU_DTYPE), (((1,), (1,)), ((), ())),
                           preferred_element_type=f32)


def _mm_tn(a, b):
    return lax.dot_general(a.astype(MXU_DTYPE), b.astype(MXU_DTYPE), (((0,), (0,)), ((), ())),
                           preferred_element_type=f32)


def _split_terms(a, n):
    if MXU_DTYPE == f32:
        return [a]
    out, r = [], a
    for i in range(n - 1):
        top = lax.bitcast_convert_type(lax.bitcast_convert_type(r, jnp.int32) & jnp.int32(-65536), f32)
        out.append(top.astype(MXU_DTYPE))
        r = r - top
    out.append(r.astype(MXU_DTYPE))
    return out


def _sel_mm(a, sel, n):
    acc = None
    for p in _split_terms(a, n):
        t = jnp.dot(p, sel, preferred_element_type=f32)
        acc = t if acc is None else acc + t
    return acc


def _mm_sel(sel, a, n):
    acc = None
    for p in _split_terms(a, n):
        t = jnp.dot(sel, p, preferred_element_type=f32)
        acc = t if acc is None else acc + t
    return acc


def _rms(x, g):
    return x * lax.rsqrt(jnp.mean(x * x, axis=-1, keepdims=True) + EPS) * g


def _sigmoid(x):
    return jax.nn.sigmoid(x)


def _softplus_neg_abs(x):
    return jnp.log(1.0 + jnp.exp(-jnp.abs(x)))


def _log_sigmoid(x):
    return jnp.minimum(x, 0.0) - _softplus_neg_abs(x)


def _head_norm(y, g_ref):
    outs = []
    for h in range(N_HEADS):
        yh = y[:, h * HEAD_DIM:(h + 1) * HEAD_DIM]
        outs.append(_rms(yh, g_ref[:, h * HEAD_DIM:(h + 1) * HEAD_DIM]))
    return jnp.concatenate(outs, axis=1)


def _store_heads(o_ref, t, fill):
    rows = t.shape[0]
    tail = jnp.where(lax.broadcasted_iota(jnp.int32, (rows, LANES - HEAD_DIM), 1) == 0, fill, 0.0)
    for h in range(N_HEADS):
        o_ref[h] = jnp.concatenate([t[:, h * HEAD_DIM:(h + 1) * HEAD_DIM], tail], axis=1).astype(o_ref.dtype)


def _in_proj_body(rows_by_feature, x_ref, g_ref, w_ref, dqb, dk, dkb, dv, dvh, mqk, mv, mo, gqk, gv, sqb, sk, skb, sv,
                  svh, z, small):
    hb = _rms(x_ref[...], g_ref[...]).astype(MXU_DTYPE)

    def proj(a, b):
        return jnp.dot(hb, w_ref[:, a:b], preferred_element_type=f32)

    def store_rows(o_ref, t):
        o_ref[...] = t.T if rows_by_feature else t

    dqb[...] = (proj(C_DQ, C_DK) * (DIFF_D ** -0.5 * LOG2E)).astype(dqb.dtype)
    t = proj(C_DK, C_DV)
    store_rows(dk, t)
    dkb[...] = t.astype(dkb.dtype)
    t = proj(C_DV, C_MQK)
    store_rows(dv, t)
    _store_heads(dvh, t, 1.0)
    mqk[...] = proj(C_MQK, C_MV)
    mv[...] = proj(C_MV, C_MO)
    mo[...] = proj(C_MO, C_GQK)
    gqk[...] = proj(C_GQK, C_GV)
    gv[...] = proj(C_GV, C_SQ)
    sqb[...] = (proj(C_SQ, C_SK) * (HEAD_DIM ** -0.5 * LOG2E)).astype(sqb.dtype)
    t = proj(C_SK, C_SV)
    store_rows(sk, t)
    skb[...] = t.astype(skb.dtype)
    t = proj(C_SV, C_Z)
    store_rows(sv, t)
    _store_heads(svh, t, 0.0)
    z[...] = proj(C_Z, C_SM)
    small[...] = proj(C_SM, N_PROJ)


def in_proj(x, g_pre, w_perm, seq_len=None):
    R = x.shape[0]
    tm = min(R, 512)
    HM = "head-major"
    KV = "kv-rows"
    widths = [(256, MXU_DTYPE), (KV, f32), (256, MXU_DTYPE), (KV, f32), (HM, MXU_DTYPE), (512, f32), (256, f32),
              (256, f32), (256, f32), (256, f32), (256, MXU_DTYPE), (KV, f32), (256, MXU_DTYPE), (KV, f32),
              (HM, MXU_DTYPE), (1024, f32), (LANES, f32)]
    if seq_len is not None:
        assert seq_len % tm == 0 and R % seq_len == 0
        nt = seq_len // tm

    def row(n):
        if n == HM:
            return pl.BlockSpec((N_HEADS, tm, LANES), lambda i: (0, i, 0))
        if n == KV and seq_len is not None:
            return pl.BlockSpec((None, BRANCH_W, tm), lambda i: (i // nt, 0, i % nt))
        return pl.BlockSpec((tm, BRANCH_W if n == KV else n), lambda i: (i, 0))

    def shape(n):
        if n == HM:
            return (N_HEADS, R, LANES)
        if n == KV:
            return (R // seq_len, BRANCH_W, seq_len) if seq_len is not None else (R, BRANCH_W)
        return (R, n)

    const = lambda s: pl.BlockSpec(s, lambda i: (0, 0), pipeline_mode=pl.Buffered(1))
    return pl.pallas_call(
        functools.partial(_in_proj_body, seq_len is not None),
        grid=(R // tm,),
        in_specs=[row(D_MODEL), const((1, D_MODEL)), const((D_MODEL, N_PROJ))],
        out_specs=[row(n) for n, _ in widths],
        out_shape=[jax.ShapeDtypeStruct(shape(n), d) for n, d in widths],
        compiler_params=_cparams(("arbitrary",)),
        name="in_proj",
    )(x, g_pre.reshape(1, D_MODEL), w_perm)


def _out_proj_body(x_ref, p_ref, ya, ym, yg, ys, z_ref, gpre, gpost, wg, wb, wo, wple, wpg, o_ref):
    x = x_ref[...]
    hb = _rms(x, gpre[...]).astype(MXU_DTYPE)
    acc = None
    for n, y in enumerate((ya, ym, yg, ys)):
        zz = z_ref[:, n * BRANCH_W:(n + 1) * BRANCH_W]
        br = y[...] * (zz * _sigmoid(zz))
        pb = _mm(br, wb[n])
        gt = _sigmoid(jnp.dot(hb, wg[:, n * D_MODEL:(n + 1) * D_MODEL], preferred_element_type=f32))
        acc = gt * pb if acc is None else acc + gt * pb
    x1 = x + _rms(_mm(acc, wo[...]), gpost[...])
    o_ref[...] = x1 + _sigmoid(_mm(x1, wpg[...])) * _mm(p_ref[...], wple[...])


def out_proj(x, p, ya, ym, yg, ys, z, g_pre, g_post, w_gate, w_branch, w_out, w_ple, w_ple_gate):
    R = x.shape[0]
    tm = min(R, 256)
    row = lambda n: pl.BlockSpec((tm, n), lambda i: (i, 0))
    const = lambda s: pl.BlockSpec(s, lambda i: (0,) * len(s), pipeline_mode=pl.Buffered(1))
    return pl.pallas_call(
        _out_proj_body,
        grid=(R // tm,),
        in_specs=[row(D_MODEL), row(P_DIM), row(BRANCH_W), row(BRANCH_W), row(BRANCH_W), row(BRANCH_W), row(D_MODEL),
                  const((1, D_MODEL)), const((1, D_MODEL)), const((D_MODEL, N_BRANCH * D_MODEL)),
                  const((N_BRANCH, BRANCH_W, D_MODEL)), const((D_MODEL, D_MODEL)), const((P_DIM, D_MODEL)),
                  const((D_MODEL, D_MODEL))],
        out_specs=row(D_MODEL),
        out_shape=jax.ShapeDtypeStruct((R, D_MODEL), f32),
        compiler_params=_cparams(("arbitrary",)),
        name="out_proj",
    )(x, p, ya, ym, yg, ys, z, g_pre.reshape(1, D_MODEL), g_post.reshape(1, D_MODEL), w_gate, w_branch, w_out,
      w_ple, w_ple_gate)


def _diff_lambda(dl_ref, lam_init):
    dl = dl_ref[...]
    return (jnp.exp(jnp.sum(dl[0:1] * dl[1:2], keepdims=True)) - jnp.exp(jnp.sum(dl[2:3] * dl[3:4], keepdims=True))
            + lam_init)


def _diff_masks(q):
    lane = lax.broadcasted_iota(jnp.int32, (1, BRANCH_W), 1)
    out = []
    for h in range(N_HEADS):
        for m in range(2):
            lo = h * HEAD_DIM + m * DIFF_D
            out.append(jnp.where((lane >= lo) & (lane < lo + DIFF_D), q, jnp.zeros_like(q)))
    return out


def _diff_attn_body(lam_init, q_ref, k_ref, v_ref, dl_ref, g_ref, o_ref, qm_scr, m_scr, acc_scr):
    i = pl.program_id(1)
    tq, big = ATT_TQ, DIFF_TK // ATT_TQ
    for idx, qq in enumerate(_diff_masks(q_ref[0])):
        qm_scr[idx] = qq
    m_scr[...] = jnp.full(m_scr.shape, -jnp.inf, f32)
    acc_scr[...] = jnp.zeros(acc_scr.shape, f32)

    def chunk(k0, width, masked):
        k = k_ref[0, pl.ds(k0, width), :]
        if masked:
            valid = (lax.broadcasted_iota(jnp.int32, (tq, width), 1) + k0
                     <= lax.broadcasted_iota(jnp.int32, (tq, width), 0) + i * tq)

        def head(h, carry):
            v = v_ref[h, pl.ds(k0, width), :]
            scores = [_mm_nt(qm_scr[2 * h + m], k) for m in range(2)]
            for m, s in enumerate(scores):
                idx = 2 * h + m
                if masked:
                    s = jnp.where(valid, s, -jnp.inf)
                m_old = m_scr[idx]
                m_new = jnp.maximum(m_old, jnp.max(s, axis=1, keepdims=True))
                acc_scr[idx] = jnp.exp2(m_old - m_new) * acc_scr[idx] + _mm(jnp.exp2(s - m_new), v)
                m_scr[idx] = m_new
            return carry

        lax.fori_loop(0, N_HEADS, head, 0)

    @pl.when(i < big - 1)
    def _():
        for r in range(big - 1):
            @pl.when(i > r)
            def _():
                chunk(r * tq, tq, False)
        chunk(pl.multiple_of(i * tq, tq), tq, True)

    @pl.when(i >= big - 1)
    def _():
        chunk(pl.multiple_of((i + 1 - big) * tq, tq), DIFF_TK, True)
        below = i + 1 - big
        n_big = below // big
        first = (below % big) * tq
        lax.fori_loop(0, n_big, lambda j, c: (chunk(pl.multiple_of(first + j * DIFF_TK, tq), DIFF_TK, False), c)[1],
                      0)
        for r in range(big - 1):
            @pl.when(below % big > r)
            def _():
                chunk(r * tq, tq, False)

    lam = _diff_lambda(dl_ref, lam_init)
    outs = []
    for h in range(N_HEADS):
        a1, a2 = acc_scr[2 * h], acc_scr[2 * h + 1]
        o1 = a1[:, :HEAD_DIM] * (1.0 / a1[:, HEAD_DIM:HEAD_DIM + 1])
        o2 = a2[:, :HEAD_DIM] * (1.0 / a2[:, HEAD_DIM:HEAD_DIM + 1])
        outs.append(o1 - lam * o2)
    o_ref[0] = _head_norm(jnp.concatenate(outs, axis=1), g_ref) * (1.0 - lam_init)


def diff_attn(lam_init, qb, kb, vh, diff_lambda, norm_g):
    B, T, _ = qb.shape
    tq = ATT_TQ
    return pl.pallas_call(
        functools.partial(_diff_attn_body, lam_init),
        grid=(B, T // tq),
        in_specs=[pl.BlockSpec((1, tq, BRANCH_W), lambda b, i: (b, i, 0)),
                  pl.BlockSpec((1, T, BRANCH_W), lambda b, i: (b, 0, 0)),
                  pl.BlockSpec((N_HEADS, None, T, LANES), lambda b, i: (0, b, 0, 0)),
                  pl.BlockSpec((4, DIFF_D), lambda b, i: (0, 0)),
                  pl.BlockSpec((1, BRANCH_W), lambda b, i: (0, 0))],
        out_specs=pl.BlockSpec((1, tq, BRANCH_W), lambda b, i: (b, i, 0)),
        out_shape=jax.ShapeDtypeStruct((B, T, BRANCH_W), f32),
        scratch_shapes=[pltpu.VMEM((2 * N_HEADS, tq, BRANCH_W), MXU_DTYPE), pltpu.VMEM((2 * N_HEADS, tq, 1), f32),
                        pltpu.VMEM((2 * N_HEADS, tq, LANES), f32)],
        compiler_params=_cparams(("arbitrary", "arbitrary")),
        name="diff_attn",
    )(qb, kb, vh, diff_lambda, norm_g.reshape(1, BRANCH_W))


def _head_masks(q):
    lane = lax.broadcasted_iota(jnp.int32, (1, BRANCH_W), 1)
    return [jnp.where((lane >= h * HEAD_DIM) & (lane < (h + 1) * HEAD_DIM), q, jnp.zeros_like(q))
            for h in range(N_HEADS)]


def _suffix_matrix(n):
    s = np.arange(n)[:, None]
    j = np.arange(n)[None, :]
    return np.concatenate([(s > j).astype(np.float32), np.ones((n, LANES), np.float32)], axis=1)


def _sb_groups(z2s, cs, u, valid):
    n = z2s[0].shape[1]
    ls, lf = [], []
    for z2 in z2s:
        sp = jnp.log2(1.0 + jnp.exp2(-jnp.abs(z2)))
        ls.append(jnp.minimum(z2, 0.0) - sp)
        f = ls[-1] - z2
        lf.append(f if valid is None else jnp.where(valid, f, 0.0))
    rs = [jnp.dot(jnp.concatenate(_split_terms(f, 2), axis=1), u[:, :n], preferred_element_type=f32) for f in lf]
    out = []
    for l, f, r, c in zip(ls, lf, rs, cs):
        a = jnp.exp2(l + r + jnp.concatenate([c] * (n // LANES), axis=1))
        out.append((a if valid is None else jnp.where(valid, a, 0.0),
                    c + jnp.broadcast_to(r[:, 0:1] + f[:, 0:1], c.shape)))
    return out


def _sb_attn_body(q_ref, k_ref, v_ref, u_ref, o_ref, qm_scr, c_scr, acc_scr):
    i = pl.program_id(1)
    tq, tk = ATT_TQ, SB_TK
    for h, qq in enumerate(_head_masks(q_ref[0])):
        qm_scr[h] = qq
    c_scr[...] = jnp.zeros(c_scr.shape, f32)
    acc_scr[...] = jnp.zeros(acc_scr.shape, f32)
    row = lax.broadcasted_iota(jnp.int32, (tq, tk), 0) + i * tq
    col = lax.broadcasted_iota(jnp.int32, (tq, tk), 1)
    u = u_ref[...]

    def group(g, masked):
        k0 = pl.multiple_of(g * tk, tk)
        k = k_ref[0, pl.ds(k0, tk), :]
        valid = (col + k0 < row) if masked else None

        zs = [_mm_nt(qm_scr[h], k) for h in range(N_HEADS)]
        res = _sb_groups(zs, [c_scr[h] for h in range(N_HEADS)], u, valid)
        pv = [_mm(res[h][0], v_ref[h, pl.ds(k0, tk), :]) for h in range(N_HEADS)]
        for h in range(N_HEADS):
            acc_scr[h] = acc_scr[h] + pv[h]
            c_scr[h] = res[h][1]

    n_diag = tq // tk
    for d in range(n_diag):
        group((i + 1) * n_diag - 1 - d, True)
    lax.fori_loop(0, i * n_diag, lambda n, c: (group(i * n_diag - 1 - n, False), c)[1], 0)
    o_ref[0] = jnp.concatenate([acc_scr[h][:, :HEAD_DIM] for h in range(N_HEADS)], axis=1)


def sb_attn(qb, kb, vh, u):
    B, T, _ = qb.shape
    tq = ATT_TQ
    return pl.pallas_call(
        _sb_attn_body,
        grid=(B, T // tq),
        in_specs=[pl.BlockSpec((1, tq, BRANCH_W), lambda b, i: (b, i, 0)),
                  pl.BlockSpec((1, T, BRANCH_W), lambda b, i: (b, 0, 0)),
                  pl.BlockSpec((N_HEADS, None, T, LANES), lambda b, i: (0, b, 0, 0)),
                  pl.BlockSpec(u.shape, lambda b, i: (0, 0))],
        out_specs=pl.BlockSpec((1, tq, BRANCH_W), lambda b, i: (b, i, 0)),
        out_shape=jax.ShapeDtypeStruct((B, T, BRANCH_W), f32),
        scratch_shapes=[pltpu.VMEM((N_HEADS, tq, BRANCH_W), MXU_DTYPE), pltpu.VMEM((N_HEADS, tq, LANES), f32),
                        pltpu.VMEM((N_HEADS, tq, LANES), f32)],
        compiler_params=_cparams(("arbitrary", "arbitrary")),
        name="sb_attn",
    )(qb, kb, vh, u)


def _conv_taps(cw_ref, cb_ref, rows):
    y = cb_ref[...]
    for j in range(CONV_W):
        y = y + rows[j] * cw_ref[j:j + 1, :]
    return y * _sigmoid(y)


def _mlstm_body(mqk_ref, mv_ref, mo_ref, sm_ref, cw_ref, cb_ref, bif_ref, g_ref, tril_ref,
                ym_ref, C_ref, n_ref, m_ref, conv_ref, xc_scr, q_scr, k_scr, C_scr, n_scr, m_scr):
    t = pl.program_id(0)
    NB, TB, L = mqk_ref.shape[0], MLSTM_TB, REC_L
    pad = SUBLANES

    @pl.when(t == 0)
    def _():
        xc_scr[:, 0:pad, :] = jnp.zeros((NB, pad, 2 * BRANCH_W), f32)
        C_scr[...] = jnp.zeros(C_scr.shape, f32)
        n_scr[...] = jnp.zeros(n_scr.shape, f32)
        m_scr[...] = jnp.zeros(m_scr.shape, f32)

    for b in range(NB):
        xc_scr[b, pad:pad + TB, :] = mqk_ref[b]
        act = _conv_taps(cw_ref, cb_ref, [xc_scr[b, pad - (CONV_W - 1) + j:pad - (CONV_W - 1) + j + TB, :]
                                          for j in range(CONV_W)])
        q_scr[b] = act[:, :BRANCH_W]
        k_scr[b] = act[:, BRANCH_W:] * HEAD_DIM ** -0.5
        conv_ref[b] = xc_scr[b, pad + TB - (CONV_W - 1):pad + TB, :]
        xc_scr[b, 0:pad, :] = xc_scr[b, TB:TB + pad, :]

    tril = tril_ref[...]
    tri_mask = lax.broadcasted_iota(jnp.int32, (L, L), 1) <= lax.broadcasted_iota(jnp.int32, (L, L), 0)
    chains = [(b, h) for b in range(NB) for h in range(N_HEADS)]
    hsl = lambda h: slice(h * HEAD_DIM, (h + 1) * HEAD_DIM)
    for c in range(TB // L):
        rs = slice(c * L, (c + 1) * L)
        gi = [sm_ref[b, rs, :] + bif_ref[...] for b in range(NB)]
        bcum = [_mm_sel(tril, _log_sigmoid(g), 3) for g in gi]
        bT = [x.T for x in bcum]
        iT = [g.T for g in gi]
        kT = [k_scr[b, rs, :].T for b in range(NB)]
        qh = {(b, h): q_scr[b, rs, hsl(h)] for b, h in chains}
        kh = {(b, h): k_scr[b, rs, hsl(h)] for b, h in chains}
        vh = {(b, h): mv_ref[b, rs, hsl(h)] for b, h in chains}
        qk = {ch: _mm_nt(qh[ch], kh[ch]) for ch in chains}
        qC = {(b, h): _mm(qh[b, h], C_scr[b, h]) for b, h in chains}
        bcol = {(b, h): bcum[b][:, SM_F + h:SM_F + h + 1] for b, h in chains}
        brow = {(b, h): bT[b][SM_F + h:SM_F + h + 1, :] for b, h in chains}
        irow = {(b, h): iT[b][SM_I + h:SM_I + h + 1, :] for b, h in chains}
        icol = {(b, h): gi[b][:, SM_I + h:SM_I + h + 1] for b, h in chains}
        m_old = {(b, h): m_scr[b, h] for b, h in chains}
        dmat = {ch: jnp.where(tri_mask, bcol[ch] - brow[ch] + irow[ch], -jnp.inf) for ch in chains}
        inter = {ch: bcol[ch] + m_old[ch] for ch in chains}
        mt = {ch: jnp.maximum(inter[ch], jnp.max(dmat[ch], axis=1, keepdims=True)) for ch in chains}
        w = {ch: qk[ch] * jnp.exp(dmat[ch] - mt[ch]) for ch in chains}
        wv = {ch: _mm(w[ch], vh[ch]) for ch in chains}
        bl = {ch: bcol[ch][L - 1:L, :] for ch in chains}
        ws = {ch: bl[ch] - bcol[ch] + icol[ch] for ch in chains}
        m_new = {ch: jnp.maximum(bl[ch] + m_old[ch], jnp.max(ws[ch], axis=0, keepdims=True)) for ch in chains}
        sc = {ch: jnp.exp(bl[ch] + m_old[ch] - m_new[ch]) for ch in chains}
        ek = {ch: kh[ch] * jnp.exp(ws[ch] - m_new[ch]) for ch in chains}
        ekT = {(b, h): kT[b][hsl(h), :] * jnp.exp(bl[b, h] - brow[b, h] + irow[b, h] - m_new[b, h]) for b, h in chains}
        kv = {ch: _mm(ekT[ch], vh[ch]) for ch in chains}
        outs = {}
        for b, h in chains:
            si = jnp.exp(inter[b, h] - mt[b, h])
            num = si * qC[b, h] + wv[b, h]
            den = (si * jnp.sum(qh[b, h] * n_scr[b, h], axis=1, keepdims=True)
                   + jnp.sum(w[b, h], axis=1, keepdims=True))
            outs[b, h] = num / jnp.maximum(jnp.abs(den), jnp.exp(-mt[b, h]))
        for b, h in chains:
            C_scr[b, h] = sc[b, h] * C_scr[b, h] + kv[b, h]
            n_scr[b, h] = sc[b, h] * n_scr[b, h] + jnp.sum(ek[b, h], axis=0, keepdims=True)
            m_scr[b, h] = m_new[b, h]
        for b in range(NB):
            ym_ref[b, rs, :] = (_head_norm(jnp.concatenate([outs[b, h] for h in range(N_HEADS)], axis=1), g_ref)
                                * _sigmoid(mo_ref[b, rs, :]))

    C_ref[...] = C_scr[...]
    for b in range(NB):
        n_ref[b] = jnp.concatenate([n_scr[b, h] for h in range(N_HEADS)], axis=1)
        m_ref[b] = jnp.concatenate([jnp.broadcast_to(m_scr[b, h], (1, LANES // N_HEADS)) for h in range(N_HEADS)],
                                   axis=1)


def mlstm_prompt(mqk, mv, mo, small, conv_w, conv_b, bif_row, norm_g, tril):
    B, T, _ = mqk.shape
    TB = MLSTM_TB
    blk = lambda n: pl.BlockSpec((B, TB, n), lambda t: (0, t, 0))
    const = lambda s: pl.BlockSpec(s, lambda t: (0,) * len(s))
    return pl.pallas_call(
        _mlstm_body,
        grid=(T // TB,),
        in_specs=[blk(2 * BRANCH_W), blk(BRANCH_W), blk(BRANCH_W), blk(LANES), const((CONV_W, 2 * BRANCH_W)),
                  const((1, 2 * BRANCH_W)), const((1, LANES)), const((1, BRANCH_W)), const((REC_L, REC_L))],
        out_specs=[blk(BRANCH_W), const((B, N_HEADS, HEAD_DIM, HEAD_DIM)), const((B, 1, BRANCH_W)),
                   const((B, 1, LANES)), const((B, CONV_W - 1, 2 * BRANCH_W))],
        out_shape=[jax.ShapeDtypeStruct((B, T, BRANCH_W), f32),
                   jax.ShapeDtypeStruct((B, N_HEADS, HEAD_DIM, HEAD_DIM), f32),
                   jax.ShapeDtypeStruct((B, 1, BRANCH_W), f32),
                   jax.ShapeDtypeStruct((B, 1, LANES), f32),
                   jax.ShapeDtypeStruct((B, CONV_W - 1, 2 * BRANCH_W), f32)],
        scratch_shapes=[pltpu.VMEM((B, TB + SUBLANES, 2 * BRANCH_W), f32), pltpu.VMEM((B, TB, BRANCH_W), f32),
                        pltpu.VMEM((B, TB, BRANCH_W), f32), pltpu.VMEM((B, N_HEADS, HEAD_DIM, HEAD_DIM), f32),
                        pltpu.VMEM((B, N_HEADS, 1, HEAD_DIM), f32), pltpu.VMEM((B, N_HEADS, 1, 1), f32)],
        compiler_params=_cparams(("arbitrary",)),
        name="mlstm_prompt",
    )(mqk, mv, mo, small, conv_w, conv_b.reshape(1, -1), bif_row, norm_g.reshape(1, BRANCH_W), tril)


def _gla_log_decay(sm, wa_ref, ba_ref):
    return _log_sigmoid(_mm(sm, wa_ref[...]) + ba_ref[...]) * (1.0 / GLA_TAU)


def _gla_body(gqk_ref, gv_ref, sm_ref, wa_ref, ba_ref, g_ref, lb_ref, ex_ref, yg_ref, S_ref, S_scr):
    t = pl.program_id(1)
    TB, L, R = REC_TB, REC_L, GLA_R
    NK = N_HEADS * GLA_DK

    @pl.when(t == 0)
    def _():
        S_scr[...] = jnp.zeros(S_scr.shape, f32)

    lb = lb_ref[...]
    ex = ex_ref[...]
    s_i = lax.broadcasted_iota(jnp.int32, (R, R, NK), 0)
    t_i = lax.broadcasted_iota(jnp.int32, (R, R, NK), 1)
    causal3 = t_i >= s_i
    lane_t = lax.broadcasted_iota(jnp.int32, (1, L), 1)
    bd_mask = (lax.broadcasted_iota(jnp.int32, (NK, BRANCH_W), 0) // GLA_DK
               == lax.broadcasted_iota(jnp.int32, (NK, BRANCH_W), 1) // HEAD_DIM)
    for c in range(TB // L):
        rs = slice(c * L, (c + 1) * L)
        la = _gla_log_decay(sm_ref[0, rs, :], wa_ref, ba_ref)
        bcl = _mm_sel(lb, la, 3)
        q_c = gqk_ref[0, rs, 0:NK] * GLA_DK ** -0.5
        k_c = gqk_ref[0, rs, NK:2 * NK]
        v_c = gv_ref[0, rs, :]
        bll = jnp.concatenate([jnp.broadcast_to(bcl[(I + 1) * R - 1:(I + 1) * R, :], (R, NK)) for I in range(L // R)],
                              axis=0)
        ktilT = (k_c * jnp.exp(bll - bcl)).T
        dblT = jnp.exp(bll).T
        qin = q_c * jnp.exp(bcl)
        outs = []
        for I in range(L // R):
            sl = slice(I * R, (I + 1) * R)
            S = S_scr[...]
            bb, vv = bcl[sl], v_c[sl]
            d = jnp.where(causal3, bb[None, :, :] - bb[:, None, :], -jnp.inf)
            p = q_c[sl][None, :, :] * k_c[sl][:, None, :] * jnp.exp(d)
            a = _sel_mm(p.reshape(R * R, NK), ex, 2).reshape(R, R, BRANCH_W)
            outs.append(_mm(qin[sl], S) + jnp.sum(a * vv[:, None, :], axis=0))
            in_blk = (lane_t >= I * R) & (lane_t < (I + 1) * R)
            upd = _mm(jnp.where(in_blk, ktilT, 0.0), v_c)
            S_scr[...] = dblT[:, I * R:I * R + 1] * S + jnp.where(bd_mask, upd, 0.0)
        yg_ref[0, rs, :] = _head_norm(jnp.concatenate(outs, axis=0), g_ref)

    for h in range(N_HEADS):
        S_ref[0, h] = S_scr[h * GLA_DK:(h + 1) * GLA_DK, h * HEAD_DIM:(h + 1) * HEAD_DIM]


def gla_prompt(gqk, gv, small, wa, ba, norm_g, lb, ex):
    B, T, _ = gqk.shape
    TB = REC_TB
    blk = lambda n: pl.BlockSpec((1, TB, n), lambda b, t: (b, t, 0))
    const = lambda s: pl.BlockSpec(s, lambda b, t: (0,) * len(s))
    return pl.pallas_call(
        _gla_body,
        grid=(B, T // TB),
        in_specs=[blk(BRANCH_W), blk(BRANCH_W), blk(LANES), const((LANES, LANES)), const((1, LANES)),
                  const((1, BRANCH_W)), const((REC_L, REC_L)), const((LANES, BRANCH_W))],
        out_specs=[blk(BRANCH_W), pl.BlockSpec((1, N_HEADS, GLA_DK, HEAD_DIM), lambda b, t: (b, 0, 0, 0))],
        out_shape=[jax.ShapeDtypeStruct((B, T, BRANCH_W), f32),
                   jax.ShapeDtypeStruct((B, N_HEADS, GLA_DK, HEAD_DIM), f32)],
        scratch_shapes=[pltpu.VMEM((N_HEADS * GLA_DK, BRANCH_W), f32)],
        compiler_params=_cparams(("arbitrary", "arbitrary")),
        name="gla_prompt",
    )(gqk, gv, small, wa, ba.reshape(1, LANES), norm_g.reshape(1, BRANCH_W), lb, ex)


def _lane_replicated_column(row):
    n = row.shape[1]
    eye = lax.broadcasted_iota(jnp.int32, (n, n), 0) == lax.broadcasted_iota(jnp.int32, (n, n), 1)
    col = jnp.sum(jnp.where(eye, jnp.broadcast_to(row, (n, n)), 0.0), axis=1, keepdims=True)
    return jnp.broadcast_to(col, (n, LANES))


def _rows_per_head(x4):
    return jnp.broadcast_to(x4[:, None, :], (N_HEADS, HEAD_DIM, LANES)).reshape(BRANCH_W, LANES)


def _lanes_per_head(col4):
    sel = (lax.broadcasted_iota(jnp.int32, (N_HEADS, BRANCH_W), 1) // HEAD_DIM
           == lax.broadcasted_iota(jnp.int32, (N_HEADS, BRANCH_W), 0))
    return jnp.sum(jnp.where(sel, col4, 0.0), axis=0, keepdims=True)


def _dec_attn_body(lam_init, n_pages, pt_ref, dq_ref, dkn_ref, dvn_ref, sq_ref, dl_ref, g_ref, u_ref, *rest):
    dk_pg, dv_pg, sk_pg, sv_pg = (rest[j * n_pages:(j + 1) * n_pages] for j in range(4))
    ya_ref, ys_ref = rest[4 * n_pages:]
    lam = _diff_lambda(dl_ref, lam_init)

    q_row = dq_ref[0].astype(f32) * LN2
    qx = _lane_replicated_column(q_row)
    s1, s2 = [], []
    for p in range(n_pages):
        r = jnp.sum((dk_pg[p][...] * qx).reshape(N_HEADS, 2, DIFF_D, LANES), axis=2)
        s1.append(r[:, 0, :])
        s2.append(r[:, 1, :])
    lane = lax.broadcasted_iota(jnp.int32, (N_HEADS, BRANCH_W), 1)
    head = lax.broadcasted_iota(jnp.int32, (N_HEADS, BRANCH_W), 0)
    qk_new = jnp.broadcast_to(q_row * dkn_ref[0], (N_HEADS, BRANCH_W))
    w_pages, w_new = [], []
    for m, s in enumerate((s1, s2)):
        lo = head * HEAD_DIM + m * DIFF_D
        s_new = jnp.sum(jnp.where((lane >= lo) & (lane < lo + DIFF_D), qk_new, 0.0), axis=1, keepdims=True)
        smax = s[0]
        for p in range(1, n_pages):
            smax = jnp.maximum(smax, s[p])
        mx = jnp.maximum(jnp.max(smax, axis=1, keepdims=True), s_new)
        e = [jnp.exp(sp - mx) for sp in s]
        e_new = jnp.exp(s_new - mx)
        tot = e[0]
        for p in range(1, n_pages):
            tot = tot + e[p]
        inv = 1.0 / (jnp.sum(tot, axis=1, keepdims=True) + e_new)
        w_pages.append([ep * inv for ep in e])
        w_new.append(e_new * inv)
    acc = jnp.zeros((BRANCH_W, LANES), f32)
    for p in range(n_pages):
        acc = acc + _rows_per_head(w_pages[0][p] - lam * w_pages[1][p]) * dv_pg[p][...]
    y = jnp.sum(acc.T, axis=0, keepdims=True) + _lanes_per_head(w_new[0] - lam * w_new[1]) * dvn_ref[0]
    ya_ref[0] = _head_norm(y, g_ref) * (1.0 - lam_init)

    qx = _lane_replicated_column(sq_ref[0].astype(f32) * LN2)
    z = jnp.concatenate([jnp.sum((sk_pg[p][...] * qx).reshape(N_HEADS, HEAD_DIM, LANES), axis=1)
                         for p in range(n_pages)], axis=0)
    sp = _softplus_neg_abs(z)
    ls = jnp.minimum(z, 0.0) - sp
    r = _sel_mm(-jnp.maximum(z, 0.0) - sp, u_ref[...], 2)
    base = ls + r[:, :LANES]
    carry = jnp.zeros((N_HEADS, LANES), f32)
    acc = jnp.zeros((BRANCH_W, LANES), f32)
    for p in reversed(range(n_pages)):
        rows = slice(p * N_HEADS, (p + 1) * N_HEADS)
        acc = acc + _rows_per_head(jnp.exp(base[rows] + carry)) * sv_pg[p][...]
        carry = carry + r[rows, LANES:]
    ys_ref[0] = jnp.sum(acc.T, axis=0, keepdims=True)


def _pages_as_feature_by_position(cache):
    d, n, pg, h, hd = cache.shape
    return jnp.transpose(cache, (0, 1, 3, 4, 2)).reshape(d, n, h * hd, pg)


def dec_attn(layer, lam_init, page_table, dqb, dk_new, dv_new, sqb, diff_lambda, norm_g, u, caches_t):
    B, n_pages = page_table.shape
    page = caches_t[0].shape[3]
    assert page == LANES
    row3 = lambda a: a.reshape(B, 1, BRANCH_W)
    rspec = pl.BlockSpec((1, 1, BRANCH_W), lambda b, pt: (b, 0, 0))
    const = lambda shp: pl.BlockSpec(shp, lambda b, pt: (0,) * len(shp))

    def pg(p):
        return pl.BlockSpec((None, None, BRANCH_W, page), lambda b, pt: (layer, pt[b * n_pages + p], 0, 0))

    in_specs = [rspec, rspec, rspec, rspec, const((4, DIFF_D)), const((1, BRANCH_W)), const((LANES, 2 * LANES))]
    args = [row3(dqb), row3(dk_new), row3(dv_new), row3(sqb), diff_lambda, norm_g.reshape(1, BRANCH_W), u]
    for c in caches_t:
        for p in range(n_pages):
            in_specs.append(pg(p))
            args.append(c)
    ya, ys = pl.pallas_call(
        functools.partial(_dec_attn_body, lam_init, n_pages),
        grid_spec=pltpu.PrefetchScalarGridSpec(
            num_scalar_prefetch=1, grid=(B,), in_specs=in_specs,
            out_specs=[pl.BlockSpec((1, 1, BRANCH_W), lambda b, pt: (b, 0, 0))] * 2),
        out_shape=[jax.ShapeDtypeStruct((B, 1, BRANCH_W), f32)] * 2,
        compiler_params=_cparams(("arbitrary",)),
        name="dec_attn",
    )(page_table.reshape(-1), *args)
    return ya.reshape(B, BRANCH_W), ys.reshape(B, BRANCH_W)


def _expand_heads(x4, width):
    return jnp.concatenate([jnp.broadcast_to(x4[:, h:h + 1], (x4.shape[0], width)) for h in range(N_HEADS)], axis=1)


def _dec_rec_body(mqk_ref, mv_ref, mo_ref, sm_ref, gqk_ref, gv_ref, conv_ref, C_ref, n_ref, m_ref, S_ref,
                  cw_ref, cb_ref, bif_ref, mg_ref, wa_ref, ba_ref, gg_ref,
                  ym_ref, yg_ref, conv_o, C_o, n_o, m_o, S_o, CT_scr, ST_scr, numT_scr, oT_scr):
    h = pl.program_id(0)
    W = 2 * BRANCH_W
    NK = N_HEADS * GLA_DK
    u = mqk_ref[...]
    rows = [conv_ref[:, j * W:(j + 1) * W] for j in range(CONV_W - 1)] + [u]
    act = _conv_taps(cw_ref, cb_ref, rows)
    q = act[:, :BRANCH_W]
    k = act[:, BRANCH_W:] * HEAD_DIM ** -0.5
    conv_o[...] = jnp.concatenate(rows[1:], axis=1)

    gi = sm_ref[...] + bif_ref[...]
    ig = gi[:, SM_I:SM_I + N_HEADS]
    lf = _log_sigmoid(gi)[:, SM_F:SM_F + N_HEADS]
    m_old = m_ref[...]
    m_new = jnp.maximum(lf + m_old, ig)
    sc = _expand_heads(jnp.exp(lf + m_old - m_new), HEAD_DIM)
    ek = _expand_heads(jnp.exp(ig - m_new), HEAD_DIM) * k
    n_new = sc * n_ref[...] + ek
    m_o[...] = m_new
    n_o[...] = n_new

    la = _gla_log_decay(sm_ref[...], wa_ref, ba_ref)
    gq = gqk_ref[:, 0:NK] * GLA_DK ** -0.5
    gk = gqk_ref[:, NK:2 * NK]

    r64 = pl.ds(pl.multiple_of(h * HEAD_DIM, HEAD_DIM), HEAD_DIM)
    r32 = pl.ds(pl.multiple_of(h * GLA_DK, GLA_DK), GLA_DK)
    numT_scr[0] = q.T
    numT_scr[1] = ek.T
    numT_scr[2] = mv_ref[...].T
    numT_scr[3] = sc.T
    qT, ekT, vT = numT_scr[0, r64, :], numT_scr[1, r64, :], numT_scr[2, r64, :]
    scT = numT_scr[3, pl.ds(h * HEAD_DIM, 1), :]
    CT_scr[...] = C_ref[...].T
    num = jnp.zeros((HEAD_DIM, LANES), f32)
    for kk in range(HEAD_DIM):
        blk = slice(kk * HEAD_DIM, (kk + 1) * HEAD_DIM)
        new = scT * CT_scr[blk, :] + ekT[kk:kk + 1, :] * vT
        CT_scr[blk, :] = new
        num = num + qT[kk:kk + 1, :] * new
    C_o[...] = CT_scr[...].T
    numT_scr[4, r64, :] = num

    oT_scr[0, 0:NK, :] = gq.T
    oT_scr[0, NK:2 * NK, :] = gk.T
    oT_scr[1, 0:NK, :] = jnp.exp(la).T
    oT_scr[2] = gv_ref[...].T
    gqT, gkT, decT = oT_scr[0, r32, :], oT_scr[0, pl.ds(pl.multiple_of(NK + h * GLA_DK, GLA_DK), GLA_DK), :], \
        oT_scr[1, r32, :]
    gvT = oT_scr[2, r64, :]
    ST_scr[...] = S_ref[...].T
    o = jnp.zeros((HEAD_DIM, LANES), f32)
    for kk in range(GLA_DK):
        blk = slice(kk * HEAD_DIM, (kk + 1) * HEAD_DIM)
        new = decT[kk:kk + 1, :] * ST_scr[blk, :] + gkT[kk:kk + 1, :] * gvT
        ST_scr[blk, :] = new
        o = o + gqT[kk:kk + 1, :] * new
    S_o[...] = ST_scr[...].T
    oT_scr[3, r64, :] = o

    @pl.when(h == N_HEADS - 1)
    def _():
        qn = jnp.concatenate([jnp.sum((q * n_new)[:, g * HEAD_DIM:(g + 1) * HEAD_DIM], axis=1, keepdims=True)
                              for g in range(N_HEADS)], axis=1)
        den = jnp.maximum(jnp.abs(qn), jnp.exp(-m_new))
        hm = numT_scr[4].T / _expand_heads(den, HEAD_DIM)
        ym_ref[...] = _head_norm(hm, mg_ref) * _sigmoid(mo_ref[...])
        yg_ref[...] = _head_norm(oT_scr[3].T, gg_ref)


def dec_rec(mqk, mv, mo, small, gqk, gv, conv, C, n, m, S, conv_w, conv_b, bif_row, mnorm_g, wa, ba, gnorm_g):
    B = mqk.shape[0]
    CW = HEAD_DIM * HEAD_DIM
    SW = GLA_DK * HEAD_DIM
    full = lambda a: pl.BlockSpec(a.shape, lambda h: (0,) * a.ndim)
    ins = [mqk, mv, mo, small, gqk, gv, conv.reshape(B, -1), C.reshape(B, N_HEADS * CW), n.reshape(B, BRANCH_W), m,
           S.reshape(B, N_HEADS * SW), conv_w, conv_b.reshape(1, -1), bif_row, mnorm_g.reshape(1, BRANCH_W), wa,
           ba.reshape(1, LANES), gnorm_g.reshape(1, BRANCH_W)]
    in_specs = [full(a) for a in ins]
    in_specs[7] = pl.BlockSpec((B, CW), lambda h: (0, h))
    in_specs[10] = pl.BlockSpec((B, SW), lambda h: (0, h))
    out_shape = [jax.ShapeDtypeStruct((B, BRANCH_W), f32), jax.ShapeDtypeStruct((B, BRANCH_W), f32),
                 jax.ShapeDtypeStruct((B, (CONV_W - 1) * 2 * BRANCH_W), f32),
                 jax.ShapeDtypeStruct((B, N_HEADS * CW), f32), jax.ShapeDtypeStruct((B, BRANCH_W), f32),
                 jax.ShapeDtypeStruct((B, N_HEADS), f32), jax.ShapeDtypeStruct((B, N_HEADS * SW), f32)]
    out_specs = [pl.BlockSpec(s.shape, lambda h: (0, 0)) for s in out_shape]
    out_specs[3] = pl.BlockSpec((B, CW), lambda h: (0, h))
    out_specs[6] = pl.BlockSpec((B, SW), lambda h: (0, h))
    ym, yg, conv_n, C_n, n_n, m_n, S_n = pl.pallas_call(
        _dec_rec_body,
        grid=(N_HEADS,),
        in_specs=in_specs, out_specs=out_specs, out_shape=out_shape,
        scratch_shapes=[pltpu.VMEM((CW, B), f32), pltpu.VMEM((SW, B), f32), pltpu.VMEM((5, BRANCH_W, B), f32),
                        pltpu.VMEM((4, BRANCH_W, B), f32)],
        compiler_params=_cparams(("arbitrary",)),
        name="dec_rec",
    )(*ins)
    return (ym, yg, conv_n.reshape(B, CONV_W - 1, 2 * BRANCH_W), C_n.reshape(B, N_HEADS, HEAD_DIM, HEAD_DIM),
            n_n.reshape(B, N_HEADS, HEAD_DIM), m_n, S_n.reshape(B, N_HEADS, GLA_DK, HEAD_DIM))


def _constants():
    t = np.arange(REC_L)
    tril = (t[None, :] <= t[:, None]).astype(np.float32)
    lb = tril * (t[None, :] // GLA_R == t[:, None] // GLA_R)
    ex = (np.arange(N_HEADS * GLA_DK)[:, None] // GLA_DK == np.arange(BRANCH_W)[None, :] // HEAD_DIM)
    cast = lambda a: jnp.asarray(a, f32).astype(MXU_DTYPE)
    n_terms = 1 if MXU_DTYPE == f32 else 2
    return dict(tril=cast(tril), lb=cast(lb), ex=cast(ex), u_page=cast(_suffix_matrix(LANES)),
                u_sb=cast(np.concatenate([_suffix_matrix(SB_TK)[:, :SB_TK]] * n_terms, axis=0)))


def _layer_params(l, prm):
    bif = prm['mlstm_b_if'][l].astype(f32).reshape(1, 2 * N_HEADS)
    wa = jnp.zeros((LANES, LANES), f32).at[SM_A:SM_A + GLA_LOWRANK].set(prm['gla_w_a2'][l])
    return dict(
        lam_init=0.8 - 0.6 * math.exp(-0.3 * l),
        bif_row=jnp.pad(bif, ((0, 0), (0, LANES - 2 * N_HEADS))),
        wa=wa.astype(MXU_DTYPE),
    )


def _prep_weights(w_in, w_gate, w_branch, w_out, w_ple, w_ple_gate):
    c = lambda a: a.astype(MXU_DTYPE)
    o = _ORIG
    parts = [w_in[:, :, :o['mi']], w_in[:, :, o['mo']:o['ga']], w_in[:, :, o['sq']:], w_in[:, :, o['mi']:o['mo']],
             w_in[:, :, o['ga']:o['sq']]]
    used = sum(p.shape[2] for p in parts)
    w_perm = jnp.concatenate([c(p) for p in parts]
                             + [jnp.zeros(w_in.shape[:2] + (N_PROJ - used,), MXU_DTYPE)], axis=2)
    return w_perm, c(w_gate), c(w_branch), c(w_out), c(w_ple), c(w_ple_gate)


def _prompt_layer(l, x, p_l, prm, wts, consts):
    B, T, _ = x.shape
    lp = _layer_params(l, prm)
    w_perm, w_gate, w_branch, w_out, w_ple, w_ple_gate = wts
    x2 = x.reshape(B * T, D_MODEL)
    (dqb, dk, dkb, dv, dvh, mqk, mv, mo, gqk, gv, sqb, sk, skb, sv, svh, z, small) = in_proj(
        x2, prm['g_pre'][l], w_perm[l], seq_len=T)
    r3 = lambda a: a.reshape(B, T, a.shape[-1])
    hm = lambda a: a.reshape(N_HEADS, B, T, LANES)
    ya = diff_attn(lp['lam_init'], r3(dqb), r3(dkb), hm(dvh), prm['diff_lambda'][l], prm['diff_norm_g'][l])
    ys = sb_attn(r3(sqb), r3(skb), hm(svh), consts['u_sb'])
    ym, C1, n1, m1, conv1 = mlstm_prompt(r3(mqk), r3(mv), r3(mo), r3(small), prm['mlstm_conv_w'][l],
                                         prm['mlstm_conv_b'][l], lp['bif_row'], prm['mlstm_norm_g'][l],
                                         consts['tril'])
    yg, S1 = gla_prompt(r3(gqk), r3(gv), r3(small), lp['wa'], _pad_ba(prm['gla_b_a'][l]), prm['gla_norm_g'][l],
                        consts['lb'], consts['ex'])
    flat = lambda a: a.reshape(B * T, BRANCH_W)
    y = out_proj(x2, p_l.reshape(B * T, P_DIM), flat(ya), flat(ym), flat(yg), flat(ys), z, prm['g_pre'][l],
                 prm['g_post'][l], w_gate[l], w_branch[l], w_out[l], w_ple[l], w_ple_gate[l])
    hd = lambda a: jnp.transpose(a.reshape(B, N_HEADS, HEAD_DIM, T), (0, 3, 1, 2))
    state = (hd(dk), hd(dv), hd(sk), hd(sv), C1, n1.reshape(B, N_HEADS, HEAD_DIM),
             m1[:, 0, ::LANES // N_HEADS], conv1, S1)
    return y.reshape(B, T, D_MODEL), state


def _pad_ba(ba):
    return ba.astype(f32)


def _decode_layer(l, x, p_l, prm, wts, consts, caches, states, page_table):
    B = x.shape[0]
    lp = _layer_params(l, prm)
    w_perm, w_gate, w_branch, w_out, w_ple, w_ple_gate = wts
    x2 = x.reshape(B, D_MODEL)
    (dqb, dk, dkb, dv, dvh, mqk, mv, mo, gqk, gv, sqb, sk, skb, sv, svh, z, small) = in_proj(
        x2, prm['g_pre'][l], w_perm[l])
    ya, ys = dec_attn(l, lp['lam_init'], page_table, dqb, dk, dv, sqb, prm['diff_lambda'][l], prm['diff_norm_g'][l],
                      consts['u_page'], caches)
    C0, n0, m0, conv0, S0 = states
    ym, yg, conv1, C1, n1, m1, S1 = dec_rec(mqk, mv, mo, small, gqk, gv, conv0[l], C0[l], n0[l], m0[l], S0[l],
                                            prm['mlstm_conv_w'][l], prm['mlstm_conv_b'][l], lp['bif_row'],
                                            prm['mlstm_norm_g'][l], lp['wa'], _pad_ba(prm['gla_b_a'][l]),
                                            prm['gla_norm_g'][l])
    y = out_proj(x2, p_l.reshape(B, P_DIM), ya, ym, yg, ys, z, prm['g_pre'][l], prm['g_post'][l], w_gate[l],
                 w_branch[l], w_out[l], w_ple[l], w_ple_gate[l])
    hd = lambda a: a.reshape(B, 1, N_HEADS, HEAD_DIM)
    state = (hd(dk), hd(dv), hd(sk), hd(sv), C1, n1, m1, conv1, S1)
    return y.reshape(B, 1, D_MODEL), state


def kernel(x_prompt, x_sample, cache_diff_k, cache_diff_v, cache_sb_k, cache_sb_v, state_mlstm_C, state_mlstm_n,
           state_mlstm_m, state_mlstm_conv, state_gla_S, page_table, p_prompt, p_sample, g_pre, g_post, w_in,
           diff_lambda, diff_norm_g, mlstm_conv_w, mlstm_conv_b, mlstm_b_if, mlstm_norm_g, gla_w_a2, gla_b_a,
           gla_norm_g, w_branch, w_gate, w_out, w_ple, w_ple_gate):
    prm = dict(g_pre=g_pre, g_post=g_post, diff_lambda=diff_lambda, diff_norm_g=diff_norm_g,
               mlstm_conv_w=mlstm_conv_w, mlstm_conv_b=mlstm_conv_b, mlstm_b_if=mlstm_b_if,
               mlstm_norm_g=mlstm_norm_g, gla_w_a2=gla_w_a2, gla_b_a=gla_b_a, gla_norm_g=gla_norm_g)
    wts = _prep_weights(w_in, w_gate, w_branch, w_out, w_ple, w_ple_gate)
    consts = _constants()
    depth = w_in.shape[0]
    caches = tuple(_pages_as_feature_by_position(c) for c in (cache_diff_k, cache_diff_v, cache_sb_k, cache_sb_v))
    states = (state_mlstm_C, state_mlstm_n, state_mlstm_m, state_mlstm_conv, state_gla_S)
    y_p, y_s = x_prompt, x_sample
    st_p, st_s = [], []
    for l in range(depth):
        y_p, s_p = _prompt_layer(l, y_p, p_prompt[l], prm, wts, consts)
        y_s, s_s = _decode_layer(l, y_s, p_sample[l], prm, wts, consts, caches, states, page_table)
        st_p.append(s_p)
        st_s.append(s_s)
    outs_p = [jnp.stack(t) for t in zip(*st_p)]
    outs_s = [jnp.stack(t) for t in zip(*st_s)]
    return (y_p, y_s, *outs_p, *outs_s)
```

```python
import functools
import math

import numpy as np
import jax
import jax.numpy as jnp
from jax import lax
from jax.experimental import pallas as pl
from jax.experimental.pallas import tpu as pltpu

f32 = jnp.float32
MXU_DTYPE = jnp.bfloat16

D_MODEL = 1024
N_BRANCH = 4
BRANCH_W = 256
N_HEADS = 4
HEAD_DIM = 64
DIFF_D = 32
GLA_DK = 32
GLA_LOWRANK = 16
GLA_TAU = 16.0
CONV_W = 4
EPS = 1e-6
P_DIM = 256
LOG2E = math.log2(math.e)
LN2 = math.log(2.0)

LANES = 128
SUBLANES = 8
VMEM_LIMIT = 56 * 1024 * 1024

_ORIG = dict(dq=0, dk=256, dv=512, mq=768, mk=1024, mv=1280, mi=1536, mf=1540, mo=1544, gq=1800,
             gk=1928, gv=2056, ga=2312, sq=2328, sk=2584, sv=2840, z=3096, end=4120)
N_PROJ = 4224
C_DQ, C_DK, C_DV, C_MQK, C_MV, C_MO, C_GQK, C_GV, C_SQ, C_SK, C_SV, C_Z, C_SM = (
    0, 256, 512, 768, 1280, 1536, 1792, 2048, 2304, 2560, 2816, 3072, 4096)
SM_I, SM_F, SM_A = 0, 4, 8

ATT_TQ = 512
DIFF_TK = 1024
SB_TK = 256
DIFF_HEADS_PER_STEP = 2
REC_TB = 512
MLSTM_TB = 256
REC_L = 128
GLA_R = 32


def _cparams(sem):
    return pltpu.CompilerParams(dimension_semantics=sem, vmem_limit_bytes=VMEM_LIMIT)


def _mm(a, b):
    return jnp.dot(a.astype(MXU_DTYPE), b.astype(MXU_DTYPE), preferred_element_type=f32)


def _mm_nt(a, b):
    return lax.dot_general(a.astype(MXU_DTYPE), b.astype(MXU_DTYPE), (((1,), (1,)), ((), ())),
                           preferred_element_type=f32)


def _mm_tn(a, b):
    return lax.dot_general(a.astype(MXU_DTYPE), b.astype(MXU_DTYPE), (((0,), (0,)), ((), ())),
                           preferred_element_type=f32)


def _split_terms(a, n):
    if MXU_DTYPE == f32:
        return [a]
    out, r = [], a
    for i in range(n - 1):
        top = lax.bitcast_convert_type(lax.bitcast_convert_type(r, jnp.int32) & jnp.int32(-65536), f32)
        out.append(top.astype(MXU_DTYPE))
        r = r - top
    out.append(r.astype(MXU_DTYPE))
    return out


def _sel_mm(a, sel, n):
    acc = None
    for p in _split_terms(a, n):
        t = jnp.dot(p, sel, preferred_element_type=f32)
        acc = t if acc is None else acc + t
    return acc


def _mm_sel(sel, a, n):
    acc = None
    for p in _split_terms(a, n):
        t = jnp.dot(sel, p, preferred_element_type=f32)
        acc = t if acc is None else acc + t
    return acc


def _rms(x, g):
    return x * lax.rsqrt(jnp.mean(x * x, axis=-1, keepdims=True) + EPS) * g


def _sigmoid(x):
    return jax.nn.sigmoid(x)


def _softplus_neg_abs(x):
    return jnp.log(1.0 + jnp.exp(-jnp.abs(x)))


def _log_sigmoid(x):
    return jnp.minimum(x, 0.0) - _softplus_neg_abs(x)


def _head_norm(y, g_ref):
    outs = []
    for h in range(N_HEADS):
        yh = y[:, h * HEAD_DIM:(h + 1) * HEAD_DIM]
        outs.append(_rms(yh, g_ref[:, h * HEAD_DIM:(h + 1) * HEAD_DIM]))
    return jnp.concatenate(outs, axis=1)


def _store_heads(o_ref, t, fill):
    rows = t.shape[0]
    tail = jnp.where(lax.broadcasted_iota(jnp.int32, (rows, LANES - HEAD_DIM), 1) == 0, fill, 0.0)
    for h in range(N_HEADS):
        o_ref[h] = jnp.concatenate([t[:, h * HEAD_DIM:(h + 1) * HEAD_DIM], tail], axis=1).astype(o_ref.dtype)


def _in_proj_body(rows_by_feature, x_ref, g_ref, w_ref, dqb, dk, dkb, dv, dvh, mqk, mv, mo, gqk, gv, sqb, sk, skb, sv,
                  svh, z, small):
    hb = _rms(x_ref[...], g_ref[...]).astype(MXU_DTYPE)

    def proj(a, b):
        return jnp.dot(hb, w_ref[:, a:b], preferred_element_type=f32)

    def store_rows(o_ref, t):
        o_ref[...] = t.T if rows_by_feature else t

    dqb[...] = (proj(C_DQ, C_DK) * (DIFF_D ** -0.5 * LOG2E)).astype(dqb.dtype)
    t = proj(C_DK, C_DV)
    store_rows(dk, t)
    dkb[...] = t.astype(dkb.dtype)
    t = proj(C_DV, C_MQK)
    store_rows(dv, t)
    _store_heads(dvh, t, 1.0)
    mqk[...] = proj(C_MQK, C_MV)
    mv[...] = proj(C_MV, C_MO)
    mo[...] = proj(C_MO, C_GQK)
    gqk[...] = proj(C_GQK, C_GV)
    gv[...] = proj(C_GV, C_SQ)
    sqb[...] = (proj(C_SQ, C_SK) * (HEAD_DIM ** -0.5 * LOG2E)).astype(sqb.dtype)
    t = proj(C_SK, C_SV)
    store_rows(sk, t)
    skb[...] = t.astype(skb.dtype)
    t = proj(C_SV, C_Z)
    store_rows(sv, t)
    _store_heads(svh, t, 0.0)
    z[...] = proj(C_Z, C_SM)
    small[...] = proj(C_SM, N_PROJ)


def in_proj(x, g_pre, w_perm, seq_len=None):
    R = x.shape[0]
    tm = min(R, 512)
    HM = "head-major"
    KV = "kv-rows"
    widths = [(256, MXU_DTYPE), (KV, f32), (256, MXU_DTYPE), (KV, f32), (HM, MXU_DTYPE), (512, f32), (256, f32),
              (256, f32), (256, f32), (256, f32), (256, MXU_DTYPE), (KV, f32), (256, MXU_DTYPE), (KV, f32),
              (HM, MXU_DTYPE), (1024, f32), (LANES, f32)]
    if seq_len is not None:
        assert seq_len % tm == 0 and R % seq_len == 0
        nt = seq_len // tm

    def row(n):
        if n == HM:
            return pl.BlockSpec((N_HEADS, tm, LANES), lambda i: (0, i, 0))
        if n == KV and seq_len is not None:
            return pl.BlockSpec((None, BRANCH_W, tm), lambda i: (i // nt, 0, i % nt))
        return pl.BlockSpec((tm, BRANCH_W if n == KV else n), lambda i: (i, 0))

    def shape(n):
        if n == HM:
            return (N_HEADS, R, LANES)
        if n == KV:
            return (R // seq_len, BRANCH_W, seq_len) if seq_len is not None else (R, BRANCH_W)
        return (R, n)

    const = lambda s: pl.BlockSpec(s, lambda i: (0, 0), pipeline_mode=pl.Buffered(1))
    return pl.pallas_call(
        functools.partial(_in_proj_body, seq_len is not None),
        grid=(R // tm,),
        in_specs=[row(D_MODEL), const((1, D_MODEL)), const((D_MODEL, N_PROJ))],
        out_specs=[row(n) for n, _ in widths],
        out_shape=[jax.ShapeDtypeStruct(shape(n), d) for n, d in widths],
        compiler_params=_cparams(("arbitrary",)),
        name="in_proj",
    )(x, g_pre.reshape(1, D_MODEL), w_perm)


def _out_proj_body(x_ref, p_ref, ya, ym, yg, ys, z_ref, gpre, gpost, wg, wb, wo, wple, wpg, o_ref):
    x = x_ref[...]
    hb = _rms(x, gpre[...]).astype(MXU_DTYPE)
    acc = None
    for n, y in enumerate((ya, ym, yg, ys)):
        zz = z_ref[:, n * BRANCH_W:(n + 1) * BRANCH_W]
        br = y[...] * (zz * _sigmoid(zz))
        pb = _mm(br, wb[n])
        gt = _sigmoid(jnp.dot(hb, wg[:, n * D_MODEL:(n + 1) * D_MODEL], preferred_element_type=f32))
        acc = gt * pb if acc is None else acc + gt * pb
    x1 = x + _rms(_mm(acc, wo[...]), gpost[...])
    o_ref[...] = x1 + _sigmoid(_mm(x1, wpg[...])) * _mm(p_ref[...], wple[...])


def out_proj(x, p, ya, ym, yg, ys, z, g_pre, g_post, w_gate, w_branch, w_out, w_ple, w_ple_gate):
    R = x.shape[0]
    tm = min(R, 256)
    row = lambda n: pl.BlockSpec((tm, n), lambda i: (i, 0))
    const = lambda s: pl.BlockSpec(s, lambda i: (0,) * len(s), pipeline_mode=pl.Buffered(1))
    return pl.pallas_call(
        _out_proj_body,
        grid=(R // tm,),
        in_specs=[row(D_MODEL), row(P_DIM), row(BRANCH_W), row(BRANCH_W), row(BRANCH_W), row(BRANCH_W), row(D_MODEL),
                  const((1, D_MODEL)), const((1, D_MODEL)), const((D_MODEL, N_BRANCH * D_MODEL)),
                  const((N_BRANCH, BRANCH_W, D_MODEL)), const((D_MODEL, D_MODEL)), const((P_DIM, D_MODEL)),
                  const((D_MODEL, D_MODEL))],
        out_specs=row(D_MODEL),
        out_shape=jax.ShapeDtypeStruct((R, D_MODEL), f32),
        compiler_params=_cparams(("arbitrary",)),
        name="out_proj",
    )(x, p, ya, ym, yg, ys, z, g_pre.reshape(1, D_MODEL), g_post.reshape(1, D_MODEL), w_gate, w_branch, w_out,
      w_ple, w_ple_gate)


def _diff_lambda(dl_ref, lam_init):
    dl = dl_ref[...]
    return (jnp.exp(jnp.sum(dl[0:1] * dl[1:2], keepdims=True)) - jnp.exp(jnp.sum(dl[2:3] * dl[3:4], keepdims=True))
            + lam_init)


def _diff_masks(q):
    lane = lax.broadcasted_iota(jnp.int32, (1, BRANCH_W), 1)
    out = []
    for h in range(N_HEADS):
        for m in range(2):
            lo = h * HEAD_DIM + m * DIFF_D
            out.append(jnp.where((lane >= lo) & (lane < lo + DIFF_D), q, jnp.zeros_like(q)))
    return out


def _diff_attn_body(lam_init, q_ref, k_ref, v_ref, dl_ref, g_ref, o_ref, qm_scr, m_scr, acc_scr):
    i = pl.program_id(1)
    tq, big = ATT_TQ, DIFF_TK // ATT_TQ
    for idx, qq in enumerate(_diff_masks(q_ref[0])):
        qm_scr[idx] = qq
    m_scr[...] = jnp.full(m_scr.shape, -jnp.inf, f32)
    acc_scr[...] = jnp.zeros(acc_scr.shape, f32)

    def chunk(k0, width, masked):
        k = k_ref[0, pl.ds(k0, width), :]
        if masked:
            valid = (lax.broadcasted_iota(jnp.int32, (tq, width), 1) + k0
                     <= lax.broadcasted_iota(jnp.int32, (tq, width), 0) + i * tq)

        def head_pair(hp, carry):
            idxs = [DIFF_HEADS_PER_STEP * 2 * hp + j for j in range(2 * DIFF_HEADS_PER_STEP)]
            vs = [v_ref[DIFF_HEADS_PER_STEP * hp + j, pl.ds(k0, width), :] for j in range(DIFF_HEADS_PER_STEP)]
            scores = [_mm_nt(qm_scr[idx], k) for idx in idxs]
            if masked:
                scores = [jnp.where(valid, s, -jnp.inf) for s in scores]
            m_old = [m_scr[idx] for idx in idxs]
            m_new = [jnp.maximum(mo, jnp.max(s, axis=1, keepdims=True)) for mo, s in zip(m_old, scores)]
            pv = [_mm(jnp.exp2(s - mn), vs[j // 2]) for j, (s, mn) in enumerate(zip(scores, m_new))]
            for j, idx in enumerate(idxs):
                acc_scr[idx] = jnp.exp2(m_old[j] - m_new[j]) * acc_scr[idx] + pv[j]
                m_scr[idx] = m_new[j]
            return carry

        lax.fori_loop(0, N_HEADS // DIFF_HEADS_PER_STEP, head_pair, 0)

    @pl.when(i < big - 1)
    def _():
        for r in range(big - 1):
            @pl.when(i > r)
            def _():
                chunk(r * tq, tq, False)
        chunk(pl.multiple_of(i * tq, tq), tq, True)

    @pl.when(i >= big - 1)
    def _():
        chunk(pl.multiple_of((i + 1 - big) * tq, tq), DIFF_TK, True)
        below = i + 1 - big
        n_big = below // big
        first = (below % big) * tq
        lax.fori_loop(0, n_big, lambda j, c: (chunk(pl.multiple_of(first + j * DIFF_TK, tq), DIFF_TK, False), c)[1],
                      0)
        for r in range(big - 1):
            @pl.when(below % big > r)
            def _():
                chunk(r * tq, tq, False)

    lam = _diff_lambda(dl_ref, lam_init)
    outs = []
    for h in range(N_HEADS):
        a1, a2 = acc_scr[2 * h], acc_scr[2 * h + 1]
        o1 = a1[:, :HEAD_DIM] * (1.0 / a1[:, HEAD_DIM:HEAD_DIM + 1])
        o2 = a2[:, :HEAD_DIM] * (1.0 / a2[:, HEAD_DIM:HEAD_DIM + 1])
        outs.append(o1 - lam * o2)
    o_ref[0] = _head_norm(jnp.concatenate(outs, axis=1), g_ref) * (1.0 - lam_init)


def diff_attn(lam_init, qb, kb, vh, diff_lambda, norm_g):
    B, T, _ = qb.shape
    tq = ATT_TQ
    return pl.pallas_call(
        functools.partial(_diff_attn_body, lam_init),
        grid=(B, T // tq),
        in_specs=[pl.BlockSpec((1, tq, BRANCH_W), lambda b, i: (b, i, 0)),
                  pl.BlockSpec((1, T, BRANCH_W), lambda b, i: (b, 0, 0)),
                  pl.BlockSpec((N_HEADS, None, T, LANES), lambda b, i: (0, b, 0, 0)),
                  pl.BlockSpec((4, DIFF_D), lambda b, i: (0, 0)),
                  pl.BlockSpec((1, BRANCH_W), lambda b, i: (0, 0))],
        out_specs=pl.BlockSpec((1, tq, BRANCH_W), lambda b, i: (b, i, 0)),
        out_shape=jax.ShapeDtypeStruct((B, T, BRANCH_W), f32),
        scratch_shapes=[pltpu.VMEM((2 * N_HEADS, tq, BRANCH_W), MXU_DTYPE), pltpu.VMEM((2 * N_HEADS, tq, 1), f32),
                        pltpu.VMEM((2 * N_HEADS, tq, LANES), f32)],
        compiler_params=_cparams(("arbitrary", "arbitrary")),
        name="diff_attn",
    )(qb, kb, vh, diff_lambda, norm_g.reshape(1, BRANCH_W))


def _head_masks(q):
    lane = lax.broadcasted_iota(jnp.int32, (1, BRANCH_W), 1)
    return [jnp.where((lane >= h * HEAD_DIM) & (lane < (h + 1) * HEAD_DIM), q, jnp.zeros_like(q))
            for h in range(N_HEADS)]


def _suffix_matrix(n):
    s = np.arange(n)[:, None]
    j = np.arange(n)[None, :]
    return np.concatenate([(s > j).astype(np.float32), np.ones((n, LANES), np.float32)], axis=1)


def _sb_groups(z2s, cs, u, valid):
    n = z2s[0].shape[1]
    ls, lf = [], []
    for z2 in z2s:
        sp = jnp.log2(1.0 + jnp.exp2(-jnp.abs(z2)))
        ls.append(jnp.minimum(z2, 0.0) - sp)
        f = ls[-1] - z2
        lf.append(f if valid is None else jnp.where(valid, f, 0.0))
    rs = [_mm(f, u) for f in lf]
    out = []
    for l, f, r, c in zip(ls, lf, rs, cs):
        a = jnp.exp2(l + r + jnp.concatenate([c] * (n // LANES), axis=1))
        out.append((a if valid is None else jnp.where(valid, a, 0.0),
                    c + jnp.broadcast_to(r[:, 0:1] + f[:, 0:1], c.shape)))
    return out


def _sb_attn_body(q_ref, k_ref, v_ref, u_ref, o_ref, qm_scr, c_scr, acc_scr):
    i = pl.program_id(1)
    tq, tk = ATT_TQ, SB_TK
    for h, qq in enumerate(_head_masks(q_ref[0])):
        qm_scr[h] = qq
    c_scr[...] = jnp.zeros(c_scr.shape, f32)
    acc_scr[...] = jnp.zeros(acc_scr.shape, f32)
    row = lax.broadcasted_iota(jnp.int32, (tq, tk), 0) + i * tq
    col = lax.broadcasted_iota(jnp.int32, (tq, tk), 1)
    u = u_ref[...]

    def group(g, masked):
        k0 = pl.multiple_of(g * tk, tk)
        k = k_ref[0, pl.ds(k0, tk), :]
        valid = (col + k0 < row) if masked else None

        zs = [_mm_nt(qm_scr[h], k) for h in range(N_HEADS)]
        res = _sb_groups(zs, [c_scr[h] for h in range(N_HEADS)], u, valid)
        pv = [_mm(res[h][0], v_ref[h, pl.ds(k0, tk), :]) for h in range(N_HEADS)]
        for h in range(N_HEADS):
            acc_scr[h] = acc_scr[h] + pv[h]
            c_scr[h] = res[h][1]

    n_diag = tq // tk
    for d in range(n_diag):
        group((i + 1) * n_diag - 1 - d, True)
    lax.fori_loop(0, i * n_diag, lambda n, c: (group(i * n_diag - 1 - n, False), c)[1], 0)
    o_ref[0] = jnp.concatenate([acc_scr[h][:, :HEAD_DIM] for h in range(N_HEADS)], axis=1)


def sb_attn(qb, kb, vh, u):
    B, T, _ = qb.shape
    tq = ATT_TQ
    return pl.pallas_call(
        _sb_attn_body,
        grid=(B, T // tq),
        in_specs=[pl.BlockSpec((1, tq, BRANCH_W), lambda b, i: (b, i, 0)),
                  pl.BlockSpec((1, T, BRANCH_W), lambda b, i: (b, 0, 0)),
                  pl.BlockSpec((N_HEADS, None, T, LANES), lambda b, i: (0, b, 0, 0)),
                  pl.BlockSpec(u.shape, lambda b, i: (0, 0))],
        out_specs=pl.BlockSpec((1, tq, BRANCH_W), lambda b, i: (b, i, 0)),
        out_shape=jax.ShapeDtypeStruct((B, T, BRANCH_W), f32),
        scratch_shapes=[pltpu.VMEM((N_HEADS, tq, BRANCH_W), MXU_DTYPE), pltpu.VMEM((N_HEADS, tq, LANES), f32),
                        pltpu.VMEM((N_HEADS, tq, LANES), f32)],
        compiler_params=_cparams(("arbitrary", "arbitrary")),
        name="sb_attn",
    )(qb, kb, vh, u)


def _conv_taps(cw_ref, cb_ref, rows):
    y = cb_ref[...]
    for j in range(CONV_W):
        y = y + rows[j] * cw_ref[j:j + 1, :]
    return y * _sigmoid(y)


def _mlstm_body(mqk_ref, mv_ref, mo_ref, sm_ref, cw_ref, cb_ref, bif_ref, g_ref, tril_ref,
                ym_ref, C_ref, n_ref, m_ref, conv_ref, xc_scr, q_scr, k_scr, C_scr, n_scr, m_scr):
    t = pl.program_id(0)
    NB, TB, L = mqk_ref.shape[0], MLSTM_TB, REC_L
    pad = SUBLANES

    @pl.when(t == 0)
    def _():
        xc_scr[:, 0:pad, :] = jnp.zeros((NB, pad, 2 * BRANCH_W), f32)
        C_scr[...] = jnp.zeros(C_scr.shape, f32)
        n_scr[...] = jnp.zeros(n_scr.shape, f32)
        m_scr[...] = jnp.zeros(m_scr.shape, f32)

    for b in range(NB):
        xc_scr[b, pad:pad + TB, :] = mqk_ref[b]
        act = _conv_taps(cw_ref, cb_ref, [xc_scr[b, pad - (CONV_W - 1) + j:pad - (CONV_W - 1) + j + TB, :]
                                          for j in range(CONV_W)])
        q_scr[b] = act[:, :BRANCH_W]
        k_scr[b] = act[:, BRANCH_W:] * HEAD_DIM ** -0.5
        conv_ref[b] = xc_scr[b, pad + TB - (CONV_W - 1):pad + TB, :]
        xc_scr[b, 0:pad, :] = xc_scr[b, TB:TB + pad, :]

    tril = tril_ref[...]
    tri_mask = lax.broadcasted_iota(jnp.int32, (L, L), 1) <= lax.broadcasted_iota(jnp.int32, (L, L), 0)
    chains = [(b, h) for b in range(NB) for h in range(N_HEADS)]
    hsl = lambda h: slice(h * HEAD_DIM, (h + 1) * HEAD_DIM)
    for c in range(TB // L):
        rs = slice(c * L, (c + 1) * L)
        gi = [sm_ref[b, rs, :] + bif_ref[...] for b in range(NB)]
        bcum = [_mm_sel(tril, _log_sigmoid(g), 3) for g in gi]
        bT = [x.T for x in bcum]
        iT = [g.T for g in gi]
        kT = [k_scr[b, rs, :].T for b in range(NB)]
        qh = {(b, h): q_scr[b, rs, hsl(h)] for b, h in chains}
        kh = {(b, h): k_scr[b, rs, hsl(h)] for b, h in chains}
        vh = {(b, h): mv_ref[b, rs, hsl(h)] for b, h in chains}
        qk = {ch: _mm_nt(qh[ch], kh[ch]) for ch in chains}
        qC = {(b, h): _mm(qh[b, h], C_scr[b, h]) for b, h in chains}
        bcol = {(b, h): bcum[b][:, SM_F + h:SM_F + h + 1] for b, h in chains}
        brow = {(b, h): bT[b][SM_F + h:SM_F + h + 1, :] for b, h in chains}
        irow = {(b, h): iT[b][SM_I + h:SM_I + h + 1, :] for b, h in chains}
        icol = {(b, h): gi[b][:, SM_I + h:SM_I + h + 1] for b, h in chains}
        m_old = {(b, h): m_scr[b, h] for b, h in chains}
        dmat = {ch: jnp.where(tri_mask, bcol[ch] - brow[ch] + irow[ch], -jnp.inf) for ch in chains}
        inter = {ch: bcol[ch] + m_old[ch] for ch in chains}
        mt = {ch: jnp.maximum(inter[ch], jnp.max(dmat[ch], axis=1, keepdims=True)) for ch in chains}
        w = {ch: qk[ch] * jnp.exp(dmat[ch] - mt[ch]) for ch in chains}
        wv = {ch: _mm(w[ch], vh[ch]) for ch in chains}
        bl = {ch: bcol[ch][L - 1:L, :] for ch in chains}
        ws = {ch: bl[ch] - bcol[ch] + icol[ch] for ch in chains}
        m_new = {ch: jnp.maximum(bl[ch] + m_old[ch], jnp.max(ws[ch], axis=0, keepdims=True)) for ch in chains}
        sc = {ch: jnp.exp(bl[ch] + m_old[ch] - m_new[ch]) for ch in chains}
        ek = {ch: kh[ch] * jnp.exp(ws[ch] - m_new[ch]) for ch in chains}
        ekT = {(b, h): kT[b][hsl(h), :] * jnp.exp(bl[b, h] - brow[b, h] + irow[b, h] - m_new[b, h]) for b, h in chains}
        kv = {ch: _mm(ekT[ch], vh[ch]) for ch in chains}
        outs = {}
        for b, h in chains:
            si = jnp.exp(inter[b, h] - mt[b, h])
            num = si * qC[b, h] + wv[b, h]
            den = (si * jnp.sum(qh[b, h] * n_scr[b, h], axis=1, keepdims=True)
                   + jnp.sum(w[b, h], axis=1, keepdims=True))
            outs[b, h] = num / jnp.maximum(jnp.abs(den), jnp.exp(-mt[b, h]))
        for b, h in chains:
            C_scr[b, h] = sc[b, h] * C_scr[b, h] + kv[b, h]
            n_scr[b, h] = sc[b, h] * n_scr[b, h] + jnp.sum(ek[b, h], axis=0, keepdims=True)
            m_scr[b, h] = m_new[b, h]
        for b in range(NB):
            ym_ref[b, rs, :] = (_head_norm(jnp.concatenate([outs[b, h] for h in range(N_HEADS)], axis=1), g_ref)
                                * _sigmoid(mo_ref[b, rs, :]))

    C_ref[...] = C_scr[...]
    for b in range(NB):
        n_ref[b] = jnp.concatenate([n_scr[b, h] for h in range(N_HEADS)], axis=1)
        m_ref[b] = jnp.concatenate([jnp.broadcast_to(m_scr[b, h], (1, LANES // N_HEADS)) for h in range(N_HEADS)],
                                   axis=1)


def mlstm_prompt(mqk, mv, mo, small, conv_w, conv_b, bif_row, norm_g, tril):
    B, T, _ = mqk.shape
    TB = MLSTM_TB
    blk = lambda n: pl.BlockSpec((B, TB, n), lambda t: (0, t, 0))
    const = lambda s: pl.BlockSpec(s, lambda t: (0,) * len(s))
    return pl.pallas_call(
        _mlstm_body,
        grid=(T // TB,),
        in_specs=[blk(2 * BRANCH_W), blk(BRANCH_W), blk(BRANCH_W), blk(LANES), const((CONV_W, 2 * BRANCH_W)),
                  const((1, 2 * BRANCH_W)), const((1, LANES)), const((1, BRANCH_W)), const((REC_L, REC_L))],
        out_specs=[blk(BRANCH_W), const((B, N_HEADS, HEAD_DIM, HEAD_DIM)), const((B, 1, BRANCH_W)),
                   const((B, 1, LANES)), const((B, CONV_W - 1, 2 * BRANCH_W))],
        out_shape=[jax.ShapeDtypeStruct((B, T, BRANCH_W), f32),
                   jax.ShapeDtypeStruct((B, N_HEADS, HEAD_DIM, HEAD_DIM), f32),
                   jax.ShapeDtypeStruct((B, 1, BRANCH_W), f32),
                   jax.ShapeDtypeStruct((B, 1, LANES), f32),
                   jax.ShapeDtypeStruct((B, CONV_W - 1, 2 * BRANCH_W), f32)],
        scratch_shapes=[pltpu.VMEM((B, TB + SUBLANES, 2 * BRANCH_W), f32), pltpu.VMEM((B, TB, BRANCH_W), f32),
                        pltpu.VMEM((B, TB, BRANCH_W), f32), pltpu.VMEM((B, N_HEADS, HEAD_DIM, HEAD_DIM), f32),
                        pltpu.VMEM((B, N_HEADS, 1, HEAD_DIM), f32), pltpu.VMEM((B, N_HEADS, 1, 1), f32)],
        compiler_params=_cparams(("arbitrary",)),
        name="mlstm_prompt",
    )(mqk, mv, mo, small, conv_w, conv_b.reshape(1, -1), bif_row, norm_g.reshape(1, BRANCH_W), tril)


def _gla_log_decay(sm, wa_ref, ba_ref):
    return _log_sigmoid(_mm(sm, wa_ref[...]) + ba_ref[...]) * (1.0 / GLA_TAU)


def _gla_body(gqk_ref, gv_ref, sm_ref, wa_ref, ba_ref, g_ref, lb_ref, ex_ref, yg_ref, S_ref, S_scr):
    t = pl.program_id(1)
    TB, L, R = REC_TB, REC_L, GLA_R
    NK = N_HEADS * GLA_DK

    @pl.when(t == 0)
    def _():
        S_scr[...] = jnp.zeros(S_scr.shape, f32)

    lb = lb_ref[...]
    ex = ex_ref[...]
    s_i = lax.broadcasted_iota(jnp.int32, (R, R, NK), 0)
    t_i = lax.broadcasted_iota(jnp.int32, (R, R, NK), 1)
    causal3 = t_i >= s_i
    lane_t = lax.broadcasted_iota(jnp.int32, (1, L), 1)
    bd_mask = (lax.broadcasted_iota(jnp.int32, (NK, BRANCH_W), 0) // GLA_DK
               == lax.broadcasted_iota(jnp.int32, (NK, BRANCH_W), 1) // HEAD_DIM)
    for c in range(TB // L):
        rs = slice(c * L, (c + 1) * L)
        la = _gla_log_decay(sm_ref[0, rs, :], wa_ref, ba_ref)
        bcl = _mm_sel(lb, la, 3)
        q_c = gqk_ref[0, rs, 0:NK] * GLA_DK ** -0.5
        k_c = gqk_ref[0, rs, NK:2 * NK]
        v_c = gv_ref[0, rs, :]
        bll = jnp.concatenate([jnp.broadcast_to(bcl[(I + 1) * R - 1:(I + 1) * R, :], (R, NK)) for I in range(L // R)],
                              axis=0)
        ktilT = (k_c * jnp.exp(bll - bcl)).T
        dblT = jnp.exp(bll).T
        qin = q_c * jnp.exp(bcl)
        outs = []
        for I in range(L // R):
            sl = slice(I * R, (I + 1) * R)
            S = S_scr[...]
            bb, vv = bcl[sl], v_c[sl]
            d = jnp.where(causal3, bb[None, :, :] - bb[:, None, :], -jnp.inf)
            p = q_c[sl][None, :, :] * k_c[sl][:, None, :] * jnp.exp(d)
            a = _mm(p.reshape(R * R, NK), ex).reshape(R, R, BRANCH_W)
            outs.append(_mm(qin[sl], S) + jnp.sum(a * vv[:, None, :], axis=0))
            in_blk = (lane_t >= I * R) & (lane_t < (I + 1) * R)
            upd = _mm(jnp.where(in_blk, ktilT, 0.0), v_c)
            S_scr[...] = dblT[:, I * R:I * R + 1] * S + jnp.where(bd_mask, upd, 0.0)
        yg_ref[0, rs, :] = _head_norm(jnp.concatenate(outs, axis=0), g_ref)

    for h in range(N_HEADS):
        S_ref[0, h] = S_scr[h * GLA_DK:(h + 1) * GLA_DK, h * HEAD_DIM:(h + 1) * HEAD_DIM]


def gla_prompt(gqk, gv, small, wa, ba, norm_g, lb, ex):
    B, T, _ = gqk.shape
    TB = REC_TB
    blk = lambda n: pl.BlockSpec((1, TB, n), lambda b, t: (b, t, 0))
    const = lambda s: pl.BlockSpec(s, lambda b, t: (0,) * len(s))
    return pl.pallas_call(
        _gla_body,
        grid=(B, T // TB),
        in_specs=[blk(BRANCH_W), blk(BRANCH_W), blk(LANES), const((LANES, LANES)), const((1, LANES)),
                  const((1, BRANCH_W)), const((REC_L, REC_L)), const((LANES, BRANCH_W))],
        out_specs=[blk(BRANCH_W), pl.BlockSpec((1, N_HEADS, GLA_DK, HEAD_DIM), lambda b, t: (b, 0, 0, 0))],
        out_shape=[jax.ShapeDtypeStruct((B, T, BRANCH_W), f32),
                   jax.ShapeDtypeStruct((B, N_HEADS, GLA_DK, HEAD_DIM), f32)],
        scratch_shapes=[pltpu.VMEM((N_HEADS * GLA_DK, BRANCH_W), f32)],
        compiler_params=_cparams(("arbitrary", "arbitrary")),
        name="gla_prompt",
    )(gqk, gv, small, wa, ba.reshape(1, LANES), norm_g.reshape(1, BRANCH_W), lb, ex)


def _lane_replicated_column(row):
    n = row.shape[1]
    eye = lax.broadcasted_iota(jnp.int32, (n, n), 0) == lax.broadcasted_iota(jnp.int32, (n, n), 1)
    col = jnp.sum(jnp.where(eye, jnp.broadcast_to(row, (n, n)), 0.0), axis=1, keepdims=True)
    return jnp.broadcast_to(col, (n, LANES))


def _rows_per_head(x4):
    return jnp.broadcast_to(x4[:, None, :], (N_HEADS, HEAD_DIM, LANES)).reshape(BRANCH_W, LANES)


def _lanes_per_head(col4):
    sel = (lax.broadcasted_iota(jnp.int32, (N_HEADS, BRANCH_W), 1) // HEAD_DIM
           == lax.broadcasted_iota(jnp.int32, (N_HEADS, BRANCH_W), 0))
    return jnp.sum(jnp.where(sel, col4, 0.0), axis=0, keepdims=True)


def _dec_attn_body(lam_init, n_pages, pt_ref, dq_ref, dkn_ref, dvn_ref, sq_ref, dl_ref, g_ref, u_ref, *rest):
    dk_pg, dv_pg, sk_pg, sv_pg = (rest[j * n_pages:(j + 1) * n_pages] for j in range(4))
    ya_ref, ys_ref = rest[4 * n_pages:]
    lam = _diff_lambda(dl_ref, lam_init)

    q_row = dq_ref[0].astype(f32) * LN2
    qx = _lane_replicated_column(q_row)
    s1, s2 = [], []
    for p in range(n_pages):
        r = jnp.sum((dk_pg[p][...] * qx).reshape(N_HEADS, 2, DIFF_D, LANES), axis=2)
        s1.append(r[:, 0, :])
        s2.append(r[:, 1, :])
    lane = lax.broadcasted_iota(jnp.int32, (N_HEADS, BRANCH_W), 1)
    head = lax.broadcasted_iota(jnp.int32, (N_HEADS, BRANCH_W), 0)
    qk_new = jnp.broadcast_to(q_row * dkn_ref[0], (N_HEADS, BRANCH_W))
    w_pages, w_new = [], []
    for m, s in enumerate((s1, s2)):
        lo = head * HEAD_DIM + m * DIFF_D
        s_new = jnp.sum(jnp.where((lane >= lo) & (lane < lo + DIFF_D), qk_new, 0.0), axis=1, keepdims=True)
        smax = s[0]
        for p in range(1, n_pages):
            smax = jnp.maximum(smax, s[p])
        mx = jnp.maximum(jnp.max(smax, axis=1, keepdims=True), s_new)
        e = [jnp.exp(sp - mx) for sp in s]
        e_new = jnp.exp(s_new - mx)
        tot = e[0]
        for p in range(1, n_pages):
            tot = tot + e[p]
        inv = 1.0 / (jnp.sum(tot, axis=1, keepdims=True) + e_new)
        w_pages.append([ep * inv for ep in e])
        w_new.append(e_new * inv)
    acc = jnp.zeros((BRANCH_W, LANES), f32)
    for p in range(n_pages):
        acc = acc + _rows_per_head(w_pages[0][p] - lam * w_pages[1][p]) * dv_pg[p][...]
    y = jnp.sum(acc.T, axis=0, keepdims=True) + _lanes_per_head(w_new[0] - lam * w_new[1]) * dvn_ref[0]
    ya_ref[0] = _head_norm(y, g_ref) * (1.0 - lam_init)

    qx = _lane_replicated_column(sq_ref[0].astype(f32) * LN2)
    z = jnp.concatenate([jnp.sum((sk_pg[p][...] * qx).reshape(N_HEADS, HEAD_DIM, LANES), axis=1)
                         for p in range(n_pages)], axis=0)
    sp = _softplus_neg_abs(z)
    ls = jnp.minimum(z, 0.0) - sp
    r = _sel_mm(-jnp.maximum(z, 0.0) - sp, u_ref[...], 2)
    base = ls + r[:, :LANES]
    carry = jnp.zeros((N_HEADS, LANES), f32)
    acc = jnp.zeros((BRANCH_W, LANES), f32)
    for p in reversed(range(n_pages)):
        rows = slice(p * N_HEADS, (p + 1) * N_HEADS)
        acc = acc + _rows_per_head(jnp.exp(base[rows] + carry)) * sv_pg[p][...]
        carry = carry + r[rows, LANES:]
    ys_ref[0] = jnp.sum(acc.T, axis=0, keepdims=True)


def _pages_as_feature_by_position(cache):
    d, n, pg, h, hd = cache.shape
    return jnp.transpose(cache, (0, 1, 3, 4, 2)).reshape(d, n, h * hd, pg)


def dec_attn(layer, lam_init, page_table, dqb, dk_new, dv_new, sqb, diff_lambda, norm_g, u, caches_t):
    B, n_pages = page_table.shape
    page = caches_t[0].shape[3]
    assert page == LANES
    row3 = lambda a: a.reshape(B, 1, BRANCH_W)
    rspec = pl.BlockSpec((1, 1, BRANCH_W), lambda b, pt: (b, 0, 0))
    const = lambda shp: pl.BlockSpec(shp, lambda b, pt: (0,) * len(shp))

    def pg(p):
        return pl.BlockSpec((None, None, BRANCH_W, page), lambda b, pt: (layer, pt[b * n_pages + p], 0, 0))

    in_specs = [rspec, rspec, rspec, rspec, const((4, DIFF_D)), const((1, BRANCH_W)), const((LANES, 2 * LANES))]
    args = [row3(dqb), row3(dk_new), row3(dv_new), row3(sqb), diff_lambda, norm_g.reshape(1, BRANCH_W), u]
    for c in caches_t:
        for p in range(n_pages):
            in_specs.append(pg(p))
            args.append(c)
    ya, ys = pl.pallas_call(
        functools.partial(_dec_attn_body, lam_init, n_pages),
        grid_spec=pltpu.PrefetchScalarGridSpec(
            num_scalar_prefetch=1, grid=(B,), in_specs=in_specs,
            out_specs=[pl.BlockSpec((1, 1, BRANCH_W), lambda b, pt: (b, 0, 0))] * 2),
        out_shape=[jax.ShapeDtypeStruct((B, 1, BRANCH_W), f32)] * 2,
        compiler_params=_cparams(("arbitrary",)),
        name="dec_attn",
    )(page_table.reshape(-1), *args)
    return ya.reshape(B, BRANCH_W), ys.reshape(B, BRANCH_W)


def _expand_heads(x4, width):
    return jnp.concatenate([jnp.broadcast_to(x4[:, h:h + 1], (x4.shape[0], width)) for h in range(N_HEADS)], axis=1)


def _dec_rec_body(mqk_ref, mv_ref, mo_ref, sm_ref, gqk_ref, gv_ref, conv_ref, C_ref, n_ref, m_ref, S_ref,
                  cw_ref, cb_ref, bif_ref, mg_ref, wa_ref, ba_ref, gg_ref,
                  ym_ref, yg_ref, conv_o, C_o, n_o, m_o, S_o, CT_scr, ST_scr, numT_scr, oT_scr):
    h = pl.program_id(0)
    W = 2 * BRANCH_W
    NK = N_HEADS * GLA_DK
    u = mqk_ref[...]
    rows = [conv_ref[:, j * W:(j + 1) * W] for j in range(CONV_W - 1)] + [u]
    act = _conv_taps(cw_ref, cb_ref, rows)
    q = act[:, :BRANCH_W]
    k = act[:, BRANCH_W:] * HEAD_DIM ** -0.5
    conv_o[...] = jnp.concatenate(rows[1:], axis=1)

    gi = sm_ref[...] + bif_ref[...]
    ig = gi[:, SM_I:SM_I + N_HEADS]
    lf = _log_sigmoid(gi)[:, SM_F:SM_F + N_HEADS]
    m_old = m_ref[...]
    m_new = jnp.maximum(lf + m_old, ig)
    sc = _expand_heads(jnp.exp(lf + m_old - m_new), HEAD_DIM)
    ek = _expand_heads(jnp.exp(ig - m_new), HEAD_DIM) * k
    n_new = sc * n_ref[...] + ek
    m_o[...] = m_new
    n_o[...] = n_new

    la = _gla_log_decay(sm_ref[...], wa_ref, ba_ref)
    gq = gqk_ref[:, 0:NK] * GLA_DK ** -0.5
    gk = gqk_ref[:, NK:2 * NK]

    r64 = pl.ds(pl.multiple_of(h * HEAD_DIM, HEAD_DIM), HEAD_DIM)
    r32 = pl.ds(pl.multiple_of(h * GLA_DK, GLA_DK), GLA_DK)
    numT_scr[0] = q.T
    numT_scr[1] = ek.T
    numT_scr[2] = mv_ref[...].T
    numT_scr[3] = sc.T
    qT, ekT, vT = numT_scr[0, r64, :], numT_scr[1, r64, :], numT_scr[2, r64, :]
    scT = numT_scr[3, pl.ds(h * HEAD_DIM, 1), :]
    CT_scr[...] = C_ref[...].T
    num = jnp.zeros((HEAD_DIM, LANES), f32)
    for kk in range(HEAD_DIM):
        blk = slice(kk * HEAD_DIM, (kk + 1) * HEAD_DIM)
        new = scT * CT_scr[blk, :] + ekT[kk:kk + 1, :] * vT
        CT_scr[blk, :] = new
        num = num + qT[kk:kk + 1, :] * new
    C_o[...] = CT_scr[...].T
    numT_scr[4, r64, :] = num

    oT_scr[0, 0:NK, :] = gq.T
    oT_scr[0, NK:2 * NK, :] = gk.T
    oT_scr[1, 0:NK, :] = jnp.exp(la).T
    oT_scr[2] = gv_ref[...].T
    gqT, gkT, decT = oT_scr[0, r32, :], oT_scr[0, pl.ds(pl.multiple_of(NK + h * GLA_DK, GLA_DK), GLA_DK), :], \
        oT_scr[1, r32, :]
    gvT = oT_scr[2, r64, :]
    ST_scr[...] = S_ref[...].T
    o = jnp.zeros((HEAD_DIM, LANES), f32)
    for kk in range(GLA_DK):
        blk = slice(kk * HEAD_DIM, (kk + 1) * HEAD_DIM)
        new = decT[kk:kk + 1, :] * ST_scr[blk, :] + gkT[kk:kk + 1, :] * gvT
        ST_scr[blk, :] = new
        o = o + gqT[kk:kk + 1, :] * new
    S_o[...] = ST_scr[...].T
    oT_scr[3, r64, :] = o

    @pl.when(h == N_HEADS - 1)
    def _():
        qn = jnp.concatenate([jnp.sum((q * n_new)[:, g * HEAD_DIM:(g + 1) * HEAD_DIM], axis=1, keepdims=True)
                              for g in range(N_HEADS)], axis=1)
        den = jnp.maximum(jnp.abs(qn), jnp.exp(-m_new))
        hm = numT_scr[4].T / _expand_heads(den, HEAD_DIM)
        ym_ref[...] = _head_norm(hm, mg_ref) * _sigmoid(mo_ref[...])
        yg_ref[...] = _head_norm(oT_scr[3].T, gg_ref)


def dec_rec(mqk, mv, mo, small, gqk, gv, conv, C, n, m, S, conv_w, conv_b, bif_row, mnorm_g, wa, ba, gnorm_g):
    B = mqk.shape[0]
    CW = HEAD_DIM * HEAD_DIM
    SW = GLA_DK * HEAD_DIM
    full = lambda a: pl.BlockSpec(a.shape, lambda h: (0,) * a.ndim)
    ins = [mqk, mv, mo, small, gqk, gv, conv.reshape(B, -1), C.reshape(B, N_HEADS * CW), n.reshape(B, BRANCH_W), m,
           S.reshape(B, N_HEADS * SW), conv_w, conv_b.reshape(1, -1), bif_row, mnorm_g.reshape(1, BRANCH_W), wa,
           ba.reshape(1, LANES), gnorm_g.reshape(1, BRANCH_W)]
    in_specs = [full(a) for a in ins]
    in_specs[7] = pl.BlockSpec((B, CW), lambda h: (0, h))
    in_specs[10] = pl.BlockSpec((B, SW), lambda h: (0, h))
    out_shape = [jax.ShapeDtypeStruct((B, BRANCH_W), f32), jax.ShapeDtypeStruct((B, BRANCH_W), f32),
                 jax.ShapeDtypeStruct((B, (CONV_W - 1) * 2 * BRANCH_W), f32),
                 jax.ShapeDtypeStruct((B, N_HEADS * CW), f32), jax.ShapeDtypeStruct((B, BRANCH_W), f32),
                 jax.ShapeDtypeStruct((B, N_HEADS), f32), jax.ShapeDtypeStruct((B, N_HEADS * SW), f32)]
    out_specs = [pl.BlockSpec(s.shape, lambda h: (0, 0)) for s in out_shape]
    out_specs[3] = pl.BlockSpec((B, CW), lambda h: (0, h))
    out_specs[6] = pl.BlockSpec((B, SW), lambda h: (0, h))
    ym, yg, conv_n, C_n, n_n, m_n, S_n = pl.pallas_call(
        _dec_rec_body,
        grid=(N_HEADS,),
        in_specs=in_specs, out_specs=out_specs, out_shape=out_shape,
        scratch_shapes=[pltpu.VMEM((CW, B), f32), pltpu.VMEM((SW, B), f32), pltpu.VMEM((5, BRANCH_W, B), f32),
                        pltpu.VMEM((4, BRANCH_W, B), f32)],
        compiler_params=_cparams(("arbitrary",)),
        name="dec_rec",
    )(*ins)
    return (ym, yg, conv_n.reshape(B, CONV_W - 1, 2 * BRANCH_W), C_n.reshape(B, N_HEADS, HEAD_DIM, HEAD_DIM),
            n_n.reshape(B, N_HEADS, HEAD_DIM), m_n, S_n.reshape(B, N_HEADS, GLA_DK, HEAD_DIM))


def _constants():
    t = np.arange(REC_L)
    tril = (t[None, :] <= t[:, None]).astype(np.float32)
    lb = tril * (t[None, :] // GLA_R == t[:, None] // GLA_R)
    ex = (np.arange(N_HEADS * GLA_DK)[:, None] // GLA_DK == np.arange(BRANCH_W)[None, :] // HEAD_DIM)
    cast = lambda a: jnp.asarray(a, f32).astype(MXU_DTYPE)
    return dict(tril=cast(tril), lb=cast(lb), ex=cast(ex), u_page=cast(_suffix_matrix(LANES)),
                u_sb=cast(_suffix_matrix(SB_TK)[:, :SB_TK]))


def _layer_params(l, prm):
    bif = prm['mlstm_b_if'][l].astype(f32).reshape(1, 2 * N_HEADS)
    wa = jnp.zeros((LANES, LANES), f32).at[SM_A:SM_A + GLA_LOWRANK].set(prm['gla_w_a2'][l])
    return dict(
        lam_init=0.8 - 0.6 * math.exp(-0.3 * l),
        bif_row=jnp.pad(bif, ((0, 0), (0, LANES - 2 * N_HEADS))),
        wa=wa.astype(MXU_DTYPE),
    )


def _prep_weights(w_in, w_gate, w_branch, w_out, w_ple, w_ple_gate):
    c = lambda a: a.astype(MXU_DTYPE)
    o = _ORIG
    parts = [w_in[:, :, :o['mi']], w_in[:, :, o['mo']:o['ga']], w_in[:, :, o['sq']:], w_in[:, :, o['mi']:o['mo']],
             w_in[:, :, o['ga']:o['sq']]]
    used = sum(p.shape[2] for p in parts)
    w_perm = jnp.concatenate([c(p) for p in parts]
                             + [jnp.zeros(w_in.shape[:2] + (N_PROJ - used,), MXU_DTYPE)], axis=2)
    return w_perm, c(w_gate), c(w_branch), c(w_out), c(w_ple), c(w_ple_gate)


def _prompt_layer(l, x, p_l, prm, wts, consts):
    B, T, _ = x.shape
    lp = _layer_params(l, prm)
    w_perm, w_gate, w_branch, w_out, w_ple, w_ple_gate = wts
    x2 = x.reshape(B * T, D_MODEL)
    (dqb, dk, dkb, dv, dvh, mqk, mv, mo, gqk, gv, sqb, sk, skb, sv, svh, z, small) = in_proj(
        x2, prm['g_pre'][l], w_perm[l], seq_len=T)
    r3 = lambda a: a.reshape(B, T, a.shape[-1])
    hm = lambda a: a.reshape(N_HEADS, B, T, LANES)
    ya = diff_attn(lp['lam_init'], r3(dqb), r3(dkb), hm(dvh), prm['diff_lambda'][l], prm['diff_norm_g'][l])
    ys = sb_attn(r3(sqb), r3(skb), hm(svh), consts['u_sb'])
    ym, C1, n1, m1, conv1 = mlstm_prompt(r3(mqk), r3(mv), r3(mo), r3(small), prm['mlstm_conv_w'][l],
                                         prm['mlstm_conv_b'][l], lp['bif_row'], prm['mlstm_norm_g'][l],
                                         consts['tril'])
    yg, S1 = gla_prompt(r3(gqk), r3(gv), r3(small), lp['wa'], _pad_ba(prm['gla_b_a'][l]), prm['gla_norm_g'][l],
                        consts['lb'], consts['ex'])
    flat = lambda a: a.reshape(B * T, BRANCH_W)
    y = out_proj(x2, p_l.reshape(B * T, P_DIM), flat(ya), flat(ym), flat(yg), flat(ys), z, prm['g_pre'][l],
                 prm['g_post'][l], w_gate[l], w_branch[l], w_out[l], w_ple[l], w_ple_gate[l])
    hd = lambda a: jnp.transpose(a.reshape(B, N_HEADS, HEAD_DIM, T), (0, 3, 1, 2))
    state = (hd(dk), hd(dv), hd(sk), hd(sv), C1, n1.reshape(B, N_HEADS, HEAD_DIM),
             m1[:, 0, ::LANES // N_HEADS], conv1, S1)
    return y.reshape(B, T, D_MODEL), state


def _pad_ba(ba):
    return ba.astype(f32)


def _decode_layer(l, x, p_l, prm, wts, consts, caches, states, page_table):
    B = x.shape[0]
    lp = _layer_params(l, prm)
    w_perm, w_gate, w_branch, w_out, w_ple, w_ple_gate = wts
    x2 = x.reshape(B, D_MODEL)
    (dqb, dk, dkb, dv, dvh, mqk, mv, mo, gqk, gv, sqb, sk, skb, sv, svh, z, small) = in_proj(
        x2, prm['g_pre'][l], w_perm[l])
    ya, ys = dec_attn(l, lp['lam_init'], page_table, dqb, dk, dv, sqb, prm['diff_lambda'][l], prm['diff_norm_g'][l],
                      consts['u_page'], caches)
    C0, n0, m0, conv0, S0 = states
    ym, yg, conv1, C1, n1, m1, S1 = dec_rec(mqk, mv, mo, small, gqk, gv, conv0[l], C0[l], n0[l], m0[l], S0[l],
                                            prm['mlstm_conv_w'][l], prm['mlstm_conv_b'][l], lp['bif_row'],
                                            prm['mlstm_norm_g'][l], lp['wa'], _pad_ba(prm['gla_b_a'][l]),
                                            prm['gla_norm_g'][l])
    y = out_proj(x2, p_l.reshape(B, P_DIM), ya, ym, yg, ys, z, prm['g_pre'][l], prm['g_post'][l], w_gate[l],
                 w_branch[l], w_out[l], w_ple[l], w_ple_gate[l])
    hd = lambda a: a.reshape(B, 1, N_HEADS, HEAD_DIM)
    state = (hd(dk), hd(dv), hd(sk), hd(sv), C1, n1, m1, conv1, S1)
    return y.reshape(B, 1, D_MODEL), state


def kernel(x_prompt, x_sample, cache_diff_k, cache_diff_v, cache_sb_k, cache_sb_v, state_mlstm_C, state_mlstm_n,
           state_mlstm_m, state_mlstm_conv, state_gla_S, page_table, p_prompt, p_sample, g_pre, g_post, w_in,
           diff_lambda, diff_norm_g, mlstm_conv_w, mlstm_conv_b, mlstm_b_if, mlstm_norm_g, gla_w_a2, gla_b_a,
           gla_norm_g, w_branch, w_gate, w_out, w_ple, w_ple_gate):
    prm = dict(g_pre=g_pre, g_post=g_post, diff_lambda=diff_lambda, diff_norm_g=diff_norm_g,
               mlstm_conv_w=mlstm_conv_w, mlstm_conv_b=mlstm_conv_b, mlstm_b_if=mlstm_b_if,
               mlstm_norm_g=mlstm_norm_g, gla_w_a2=gla_w_a2, gla_b_a=gla_b_a, gla_norm_g=gla_norm_g)
    wts = _prep_weights(w_in, w_gate, w_branch, w_out, w_ple, w_ple_gate)
    consts = _constants()
    depth = w_in.shape[0]
    caches = tuple(_pages_as_feature_by_position(c) for c in (cache_diff_k, cache_diff_v, cache_sb_k, cache_sb_v))
    states = (state_mlstm_C, state_mlstm_n, state_mlstm_m, state_mlstm_conv, state_gla_S)
    y_p, y_s = x_prompt, x_sample
    st_p, st_s = [], []
    for l in range(depth):
        y_p, s_p = _prompt_layer(l, y_p, p_prompt[l], prm, wts, consts)
        y_s, s_s = _decode_layer(l, y_s, p_sample[l], prm, wts, consts, caches, states, page_table)
        st_p.append(s_p)
        st_s.append(s_s)
    outs_p = [jnp.stack(t) for t in zip(*st_p)]
    outs_s = [jnp.stack(t) for t in zip(*st_s)]
    return (y_p, y_s, *outs_p, *outs_s)
```

```python
import functools
import math

import numpy as np
import jax
import jax.numpy as jnp
from jax import lax
from jax.experimental import pallas as pl
from jax.experimental.pallas import tpu as pltpu

f32 = jnp.float32
MXU_DTYPE = jnp.bfloat16

D_MODEL = 1024
N_BRANCH = 4
BRANCH_W = 256
N_HEADS = 4
HEAD_DIM = 64
DIFF_D = 32
GLA_DK = 32
GLA_LOWRANK = 16
GLA_TAU = 16.0
CONV_W = 4
EPS = 1e-6
P_DIM = 256
LOG2E = math.log2(math.e)
LN2 = math.log(2.0)

LANES = 128
SUBLANES = 8
VMEM_LIMIT = 56 * 1024 * 1024

_ORIG = dict(dq=0, dk=256, dv=512, mq=768, mk=1024, mv=1280, mi=1536, mf=1540, mo=1544, gq=1800,
             gk=1928, gv=2056, ga=2312, sq=2328, sk=2584, sv=2840, z=3096, end=4120)
N_PROJ = 4224
C_DQ, C_DK, C_DV, C_MQK, C_MV, C_MO, C_GQK, C_GV, C_SQ, C_SK, C_SV, C_Z, C_SM = (
    0, 256, 512, 768, 1280, 1536, 1792, 2048, 2304, 2560, 2816, 3072, 4096)
SM_I, SM_F, SM_A = 0, 4, 8

ATT_TQ = 512
DIFF_TK = 1024
SB_TK = 256
DIFF_HEADS_PER_STEP = 2
SB_HEADS_PER_STAGE = 2
REC_TB = 512
MLSTM_TB = 256
REC_L = 128
GLA_R = 32


def _cparams(sem):
    return pltpu.CompilerParams(dimension_semantics=sem, vmem_limit_bytes=VMEM_LIMIT)


def _mm(a, b):
    return jnp.dot(a.astype(MXU_DTYPE), b.astype(MXU_DTYPE), preferred_element_type=f32)


def _mm_nt(a, b):
    return lax.dot_general(a.astype(MXU_DTYPE), b.astype(MXU_DTYPE), (((1,), (1,)), ((), ())),
                           preferred_element_type=f32)


def _mm_tn(a, b):
    return lax.dot_general(a.astype(MXU_DTYPE), b.astype(MXU_DTYPE), (((0,), (0,)), ((), ())),
                           preferred_element_type=f32)


def _split_terms(a, n):
    if MXU_DTYPE == f32:
        return [a]
    out, r = [], a
    for i in range(n - 1):
        top = lax.bitcast_convert_type(lax.bitcast_convert_type(r, jnp.int32) & jnp.int32(-65536), f32)
        out.append(top.astype(MXU_DTYPE))
        r = r - top
    out.append(r.astype(MXU_DTYPE))
    return out


def _sel_mm(a, sel, n):
    acc = None
    for p in _split_terms(a, n):
        t = jnp.dot(p, sel, preferred_element_type=f32)
        acc = t if acc is None else acc + t
    return acc


def _mm_sel(sel, a, n):
    acc = None
    for p in _split_terms(a, n):
        t = jnp.dot(sel, p, preferred_element_type=f32)
        acc = t if acc is None else acc + t
    return acc


def _rms(x, g):
    return x * lax.rsqrt(jnp.mean(x * x, axis=-1, keepdims=True) + EPS) * g


def _sigmoid(x):
    return jax.nn.sigmoid(x)


def _softplus_neg_abs(x):
    return jnp.log(1.0 + jnp.exp(-jnp.abs(x)))


def _log_sigmoid(x):
    return jnp.minimum(x, 0.0) - _softplus_neg_abs(x)


def _head_norm(y, g_ref):
    outs = []
    for h in range(N_HEADS):
        yh = y[:, h * HEAD_DIM:(h + 1) * HEAD_DIM]
        outs.append(_rms(yh, g_ref[:, h * HEAD_DIM:(h + 1) * HEAD_DIM]))
    return jnp.concatenate(outs, axis=1)


def _store_heads(o_ref, t, fill):
    rows = t.shape[0]
    tail = jnp.where(lax.broadcasted_iota(jnp.int32, (rows, LANES - HEAD_DIM), 1) == 0, fill, 0.0)
    for h in range(N_HEADS):
        o_ref[h] = jnp.concatenate([t[:, h * HEAD_DIM:(h + 1) * HEAD_DIM], tail], axis=1).astype(o_ref.dtype)


def _in_proj_body(rows_by_feature, x_ref, g_ref, w_ref, dqb, dk, dkb, dv, dvh, mqk, mv, mo, gqk, gv, sqb, sk, skb, sv,
                  svh, z, small):
    hb = _rms(x_ref[...], g_ref[...]).astype(MXU_DTYPE)

    def proj(a, b):
        return jnp.dot(hb, w_ref[:, a:b], preferred_element_type=f32)

    def store_rows(o_ref, t):
        o_ref[...] = t.T if rows_by_feature else t

    dqb[...] = (proj(C_DQ, C_DK) * (DIFF_D ** -0.5 * LOG2E)).astype(dqb.dtype)
    t = proj(C_DK, C_DV)
    store_rows(dk, t)
    dkb[...] = t.astype(dkb.dtype)
    t = proj(C_DV, C_MQK)
    store_rows(dv, t)
    _store_heads(dvh, t, 1.0)
    mqk[...] = proj(C_MQK, C_MV)
    mv[...] = proj(C_MV, C_MO)
    mo[...] = proj(C_MO, C_GQK)
    gqk[...] = proj(C_GQK, C_GV)
    gv[...] = proj(C_GV, C_SQ)
    sqb[...] = (proj(C_SQ, C_SK) * (HEAD_DIM ** -0.5 * LOG2E)).astype(sqb.dtype)
    t = proj(C_SK, C_SV)
    store_rows(sk, t)
    skb[...] = t.astype(skb.dtype)
    t = proj(C_SV, C_Z)
    store_rows(sv, t)
    _store_heads(svh, t, 0.0)
    z[...] = proj(C_Z, C_SM)
    small[...] = proj(C_SM, N_PROJ)


def in_proj(x, g_pre, w_perm, seq_len=None):
    R = x.shape[0]
    tm = min(R, 512)
    HM = "head-major"
    KV = "kv-rows"
    widths = [(256, MXU_DTYPE), (KV, f32), (256, MXU_DTYPE), (KV, f32), (HM, MXU_DTYPE), (512, f32), (256, f32),
              (256, f32), (256, f32), (256, f32), (256, MXU_DTYPE), (KV, f32), (256, MXU_DTYPE), (KV, f32),
              (HM, MXU_DTYPE), (1024, f32), (LANES, f32)]
    if seq_len is not None:
        assert seq_len % tm == 0 and R % seq_len == 0
        nt = seq_len // tm

    def row(n):
        if n == HM:
            return pl.BlockSpec((N_HEADS, tm, LANES), lambda i: (0, i, 0))
        if n == KV and seq_len is not None:
            return pl.BlockSpec((None, BRANCH_W, tm), lambda i: (i // nt, 0, i % nt))
        return pl.BlockSpec((tm, BRANCH_W if n == KV else n), lambda i: (i, 0))

    def shape(n):
        if n == HM:
            return (N_HEADS, R, LANES)
        if n == KV:
            return (R // seq_len, BRANCH_W, seq_len) if seq_len is not None else (R, BRANCH_W)
        return (R, n)

    const = lambda s: pl.BlockSpec(s, lambda i: (0, 0), pipeline_mode=pl.Buffered(1))
    return pl.pallas_call(
        functools.partial(_in_proj_body, seq_len is not None),
        grid=(R // tm,),
        in_specs=[row(D_MODEL), const((1, D_MODEL)), const((D_MODEL, N_PROJ))],
        out_specs=[row(n) for n, _ in widths],
        out_shape=[jax.ShapeDtypeStruct(shape(n), d) for n, d in widths],
        compiler_params=_cparams(("arbitrary",)),
        name="in_proj",
    )(x, g_pre.reshape(1, D_MODEL), w_perm)


def _out_proj_body(x_ref, p_ref, ya, ym, yg, ys, z_ref, gpre, gpost, wg, wb, wo, wple, wpg, o_ref):
    x = x_ref[...]
    hb = _rms(x, gpre[...]).astype(MXU_DTYPE)
    acc = None
    for n, y in enumerate((ya, ym, yg, ys)):
        zz = z_ref[:, n * BRANCH_W:(n + 1) * BRANCH_W]
        br = y[...] * (zz * _sigmoid(zz))
        pb = _mm(br, wb[n])
        gt = _sigmoid(jnp.dot(hb, wg[:, n * D_MODEL:(n + 1) * D_MODEL], preferred_element_type=f32))
        acc = gt * pb if acc is None else acc + gt * pb
    x1 = x + _rms(_mm(acc, wo[...]), gpost[...])
    o_ref[...] = x1 + _sigmoid(_mm(x1, wpg[...])) * _mm(p_ref[...], wple[...])


def out_proj(x, p, ya, ym, yg, ys, z, g_pre, g_post, w_gate, w_branch, w_out, w_ple, w_ple_gate):
    R = x.shape[0]
    tm = min(R, 256)
    row = lambda n: pl.BlockSpec((tm, n), lambda i: (i, 0))
    const = lambda s: pl.BlockSpec(s, lambda i: (0,) * len(s), pipeline_mode=pl.Buffered(1))
    return pl.pallas_call(
        _out_proj_body,
        grid=(R // tm,),
        in_specs=[row(D_MODEL), row(P_DIM), row(BRANCH_W), row(BRANCH_W), row(BRANCH_W), row(BRANCH_W), row(D_MODEL),
                  const((1, D_MODEL)), const((1, D_MODEL)), const((D_MODEL, N_BRANCH * D_MODEL)),
                  const((N_BRANCH, BRANCH_W, D_MODEL)), const((D_MODEL, D_MODEL)), const((P_DIM, D_MODEL)),
                  const((D_MODEL, D_MODEL))],
        out_specs=row(D_MODEL),
        out_shape=jax.ShapeDtypeStruct((R, D_MODEL), f32),
        compiler_params=_cparams(("arbitrary",)),
        name="out_proj",
    )(x, p, ya, ym, yg, ys, z, g_pre.reshape(1, D_MODEL), g_post.reshape(1, D_MODEL), w_gate, w_branch, w_out,
      w_ple, w_ple_gate)


def _diff_lambda(dl_ref, lam_init):
    dl = dl_ref[...]
    return (jnp.exp(jnp.sum(dl[0:1] * dl[1:2], keepdims=True)) - jnp.exp(jnp.sum(dl[2:3] * dl[3:4], keepdims=True))
            + lam_init)


def _diff_masks(q):
    lane = lax.broadcasted_iota(jnp.int32, (1, BRANCH_W), 1)
    out = []
    for h in range(N_HEADS):
        for m in range(2):
            lo = h * HEAD_DIM + m * DIFF_D
            out.append(jnp.where((lane >= lo) & (lane < lo + DIFF_D), q, jnp.zeros_like(q)))
    return out


def _diff_attn_body(lam_init, q_ref, k_ref, v_ref, dl_ref, g_ref, o_ref, qm_scr, m_scr, acc_scr):
    i = pl.program_id(1)
    tq, big = ATT_TQ, DIFF_TK // ATT_TQ
    for idx, qq in enumerate(_diff_masks(q_ref[0])):
        qm_scr[idx] = qq
    m_scr[...] = jnp.full(m_scr.shape, -jnp.inf, f32)
    acc_scr[...] = jnp.zeros(acc_scr.shape, f32)

    def chunk(k0, width, masked):
        k = k_ref[0, pl.ds(k0, width), :]
        if masked:
            valid = (lax.broadcasted_iota(jnp.int32, (tq, width), 1) + k0
                     <= lax.broadcasted_iota(jnp.int32, (tq, width), 0) + i * tq)

        def head_pair(hp, carry):
            idxs = [DIFF_HEADS_PER_STEP * 2 * hp + j for j in range(2 * DIFF_HEADS_PER_STEP)]
            vs = [v_ref[DIFF_HEADS_PER_STEP * hp + j, pl.ds(k0, width), :] for j in range(DIFF_HEADS_PER_STEP)]
            scores = [_mm_nt(qm_scr[idx], k) for idx in idxs]
            for j0 in range(0, len(idxs), 2):
                js = (j0, j0 + 1)
                ss = [jnp.where(valid, scores[j], -jnp.inf) if masked else scores[j] for j in js]
                m_old = [m_scr[idxs[j]] for j in js]
                m_new = [jnp.maximum(mo, jnp.max(s, axis=1, keepdims=True)) for mo, s in zip(m_old, ss)]
                pv = [_mm(jnp.exp2(s - mn), vs[j0 // 2]) for s, mn in zip(ss, m_new)]
                for j, mo, mn, x in zip(js, m_old, m_new, pv):
                    acc_scr[idxs[j]] = jnp.exp2(mo - mn) * acc_scr[idxs[j]] + x
                    m_scr[idxs[j]] = mn
            return carry

        lax.fori_loop(0, N_HEADS // DIFF_HEADS_PER_STEP, head_pair, 0)

    assert big == 2
    for top in (1, 2, 3):
        if top == 1:
            cond = i == 0
        elif top == 2:
            cond = i % 2 == 1
        else:
            cond = (i % 2 == 0) & (i > 0)

        @pl.when(cond)
        def _(top=top):
            chunk(pl.multiple_of((i + 1 - top) * tq, tq), top * tq, True)

    lax.fori_loop(0, jnp.where(i == 0, 0, (i - 1) // 2),
                  lambda j, c: (chunk(pl.multiple_of(j * DIFF_TK, DIFF_TK), DIFF_TK, False), c)[1], 0)

    lam = _diff_lambda(dl_ref, lam_init)
    outs = []
    for h in range(N_HEADS):
        a1, a2 = acc_scr[2 * h], acc_scr[2 * h + 1]
        o1 = a1[:, :HEAD_DIM] * (1.0 / a1[:, HEAD_DIM:HEAD_DIM + 1])
        o2 = a2[:, :HEAD_DIM] * (1.0 / a2[:, HEAD_DIM:HEAD_DIM + 1])
        outs.append(o1 - lam * o2)
    o_ref[0] = _head_norm(jnp.concatenate(outs, axis=1), g_ref) * (1.0 - lam_init)


def diff_attn(lam_init, qb, kb, vh, diff_lambda, norm_g):
    B, T, _ = qb.shape
    tq = ATT_TQ
    return pl.pallas_call(
        functools.partial(_diff_attn_body, lam_init),
        grid=(B, T // tq),
        in_specs=[pl.BlockSpec((1, tq, BRANCH_W), lambda b, i: (b, i, 0)),
                  pl.BlockSpec((1, T, BRANCH_W), lambda b, i: (b, 0, 0), pipeline_mode=pl.Buffered(1)),
                  pl.BlockSpec((N_HEADS, None, T, LANES), lambda b, i: (0, b, 0, 0), pipeline_mode=pl.Buffered(1)),
                  pl.BlockSpec((4, DIFF_D), lambda b, i: (0, 0)),
                  pl.BlockSpec((1, BRANCH_W), lambda b, i: (0, 0))],
        out_specs=pl.BlockSpec((1, tq, BRANCH_W), lambda b, i: (b, i, 0)),
        out_shape=jax.ShapeDtypeStruct((B, T, BRANCH_W), f32),
        scratch_shapes=[pltpu.VMEM((2 * N_HEADS, tq, BRANCH_W), MXU_DTYPE), pltpu.VMEM((2 * N_HEADS, tq, 1), f32),
                        pltpu.VMEM((2 * N_HEADS, tq, LANES), f32)],
        compiler_params=_cparams(("arbitrary", "arbitrary")),
        name="diff_attn",
    )(qb, kb, vh, diff_lambda, norm_g.reshape(1, BRANCH_W))


def _head_masks(q):
    lane = lax.broadcasted_iota(jnp.int32, (1, BRANCH_W), 1)
    return [jnp.where((lane >= h * HEAD_DIM) & (lane < (h + 1) * HEAD_DIM), q, jnp.zeros_like(q))
            for h in range(N_HEADS)]


def _suffix_matrix(n):
    s = np.arange(n)[:, None]
    j = np.arange(n)[None, :]
    return np.concatenate([(s > j).astype(np.float32), np.ones((n, LANES), np.float32)], axis=1)


def _sb_groups(z2s, cs, u, valid):
    n = z2s[0].shape[1]
    ls, lf = [], []
    for z2 in z2s:
        sp = jnp.log2(1.0 + jnp.exp2(-jnp.abs(z2)))
        ls.append(jnp.minimum(z2, 0.0) - sp)
        f = ls[-1] - z2
        lf.append(f if valid is None else jnp.where(valid, f, 0.0))
    rs = [_mm(f, u) for f in lf]
    out = []
    for l, f, r, c in zip(ls, lf, rs, cs):
        a = jnp.exp2(l + r + jnp.concatenate([c] * (n // LANES), axis=1))
        out.append((a if valid is None else jnp.where(valid, a, 0.0),
                    c + jnp.broadcast_to(r[:, 0:1] + f[:, 0:1], c.shape)))
    return out


def _sb_attn_body(q_ref, k_ref, v_ref, u_ref, o_ref, qm_scr, c_scr, acc_scr):
    i = pl.program_id(1)
    tq, tk = ATT_TQ, SB_TK
    for h, qq in enumerate(_head_masks(q_ref[0])):
        qm_scr[h] = qq
    c_scr[...] = jnp.zeros(c_scr.shape, f32)
    acc_scr[...] = jnp.zeros(acc_scr.shape, f32)
    row = lax.broadcasted_iota(jnp.int32, (tq, tk), 0) + i * tq
    col = lax.broadcasted_iota(jnp.int32, (tq, tk), 1)
    u = u_ref[...]

    def group(g, masked):
        k0 = pl.multiple_of(g * tk, tk)
        k = k_ref[0, pl.ds(k0, tk), :]
        valid = (col + k0 < row) if masked else None

        zs = [_mm_nt(qm_scr[h], k) for h in range(N_HEADS)]
        for h0 in range(0, N_HEADS, SB_HEADS_PER_STAGE):
            hs = range(h0, h0 + SB_HEADS_PER_STAGE)
            res = _sb_groups([zs[h] for h in hs], [c_scr[h] for h in hs], u, valid)
            pv = [_mm(r[0], v_ref[h, pl.ds(k0, tk), :]) for h, r in zip(hs, res)]
            for h, r, x in zip(hs, res, pv):
                acc_scr[h] = acc_scr[h] + x
                c_scr[h] = r[1]

    n_diag = tq // tk
    for d in range(n_diag):
        group((i + 1) * n_diag - 1 - d, True)
    lax.fori_loop(0, i * n_diag, lambda n, c: (group(i * n_diag - 1 - n, False), c)[1], 0)
    o_ref[0] = jnp.concatenate([acc_scr[h][:, :HEAD_DIM] for h in range(N_HEADS)], axis=1)


def sb_attn(qb, kb, vh, u):
    B, T, _ = qb.shape
    tq = ATT_TQ
    return pl.pallas_call(
        _sb_attn_body,
        grid=(B, T // tq),
        in_specs=[pl.BlockSpec((1, tq, BRANCH_W), lambda b, i: (b, i, 0)),
                  pl.BlockSpec((1, T, BRANCH_W), lambda b, i: (b, 0, 0)),
                  pl.BlockSpec((N_HEADS, None, T, LANES), lambda b, i: (0, b, 0, 0)),
                  pl.BlockSpec(u.shape, lambda b, i: (0, 0))],
        out_specs=pl.BlockSpec((1, tq, BRANCH_W), lambda b, i: (b, i, 0)),
        out_shape=jax.ShapeDtypeStruct((B, T, BRANCH_W), f32),
        scratch_shapes=[pltpu.VMEM((N_HEADS, tq, BRANCH_W), MXU_DTYPE), pltpu.VMEM((N_HEADS, tq, LANES), f32),
                        pltpu.VMEM((N_HEADS, tq, LANES), f32)],
        compiler_params=_cparams(("arbitrary", "arbitrary")),
        name="sb_attn",
    )(qb, kb, vh, u)


def _conv_taps(cw_ref, cb_ref, rows):
    y = cb_ref[...]
    for j in range(CONV_W):
        y = y + rows[j] * cw_ref[j:j + 1, :]
    return y * _sigmoid(y)


def _mlstm_body(mqk_ref, mv_ref, mo_ref, sm_ref, cw_ref, cb_ref, bif_ref, g_ref, tril_ref,
                ym_ref, C_ref, n_ref, m_ref, conv_ref, xc_scr, q_scr, k_scr, C_scr, n_scr, m_scr):
    t = pl.program_id(0)
    NB, TB, L = mqk_ref.shape[0], MLSTM_TB, REC_L
    pad = SUBLANES

    @pl.when(t == 0)
    def _():
        xc_scr[:, 0:pad, :] = jnp.zeros((NB, pad, 2 * BRANCH_W), f32)
        C_scr[...] = jnp.zeros(C_scr.shape, f32)
        n_scr[...] = jnp.zeros(n_scr.shape, f32)
        m_scr[...] = jnp.zeros(m_scr.shape, f32)

    for b in range(NB):
        xc_scr[b, pad:pad + TB, :] = mqk_ref[b]
        act = _conv_taps(cw_ref, cb_ref, [xc_scr[b, pad - (CONV_W - 1) + j:pad - (CONV_W - 1) + j + TB, :]
                                          for j in range(CONV_W)])
        q_scr[b] = act[:, :BRANCH_W]
        k_scr[b] = act[:, BRANCH_W:] * HEAD_DIM ** -0.5
        conv_ref[b] = xc_scr[b, pad + TB - (CONV_W - 1):pad + TB, :]
        xc_scr[b, 0:pad, :] = xc_scr[b, TB:TB + pad, :]

    tril = tril_ref[...]
    tri_mask = lax.broadcasted_iota(jnp.int32, (L, L), 1) <= lax.broadcasted_iota(jnp.int32, (L, L), 0)
    chains = [(b, h) for b in range(NB) for h in range(N_HEADS)]
    hsl = lambda h: slice(h * HEAD_DIM, (h + 1) * HEAD_DIM)
    for c in range(TB // L):
        rs = slice(c * L, (c + 1) * L)
        gi = [sm_ref[b, rs, :] + bif_ref[...] for b in range(NB)]
        bcum = [_mm_sel(tril, _log_sigmoid(g), 3) for g in gi]
        bT = [x.T for x in bcum]
        iT = [g.T for g in gi]
        kT = [k_scr[b, rs, :].T for b in range(NB)]
        qh = {(b, h): q_scr[b, rs, hsl(h)] for b, h in chains}
        kh = {(b, h): k_scr[b, rs, hsl(h)] for b, h in chains}
        vh = {(b, h): mv_ref[b, rs, hsl(h)] for b, h in chains}
        qk = {ch: _mm_nt(qh[ch], kh[ch]) for ch in chains}
        qC = {(b, h): _mm(qh[b, h], C_scr[b, h]) for b, h in chains}
        bcol = {(b, h): bcum[b][:, SM_F + h:SM_F + h + 1] for b, h in chains}
        brow = {(b, h): bT[b][SM_F + h:SM_F + h + 1, :] for b, h in chains}
        irow = {(b, h): iT[b][SM_I + h:SM_I + h + 1, :] for b, h in chains}
        icol = {(b, h): gi[b][:, SM_I + h:SM_I + h + 1] for b, h in chains}
        m_old = {(b, h): m_scr[b, h] for b, h in chains}
        dmat = {ch: jnp.where(tri_mask, bcol[ch] - brow[ch] + irow[ch], -jnp.inf) for ch in chains}
        inter = {ch: bcol[ch] + m_old[ch] for ch in chains}
        mt = {ch: jnp.maximum(inter[ch], jnp.max(dmat[ch], axis=1, keepdims=True)) for ch in chains}
        w = {ch: qk[ch] * jnp.exp(dmat[ch] - mt[ch]) for ch in chains}
        wv = {ch: _mm(w[ch], vh[ch]) for ch in chains}
        bl = {ch: bcol[ch][L - 1:L, :] for ch in chains}
        ws = {ch: bl[ch] - bcol[ch] + icol[ch] for ch in chains}
        m_new = {ch: jnp.maximum(bl[ch] + m_old[ch], jnp.max(ws[ch], axis=0, keepdims=True)) for ch in chains}
        sc = {ch: jnp.exp(bl[ch] + m_old[ch] - m_new[ch]) for ch in chains}
        ek = {ch: kh[ch] * jnp.exp(ws[ch] - m_new[ch]) for ch in chains}
        ekT = {(b, h): kT[b][hsl(h), :] * jnp.exp(bl[b, h] - brow[b, h] + irow[b, h] - m_new[b, h]) for b, h in chains}
        kv = {ch: _mm(ekT[ch], vh[ch]) for ch in chains}
        outs = {}
        for b, h in chains:
            si = jnp.exp(inter[b, h] - mt[b, h])
            num = si * qC[b, h] + wv[b, h]
            den = (si * jnp.sum(qh[b, h] * n_scr[b, h], axis=1, keepdims=True)
                   + jnp.sum(w[b, h], axis=1, keepdims=True))
            outs[b, h] = num / jnp.maximum(jnp.abs(den), jnp.exp(-mt[b, h]))
        for b, h in chains:
            C_scr[b, h] = sc[b, h] * C_scr[b, h] + kv[b, h]
            n_scr[b, h] = sc[b, h] * n_scr[b, h] + jnp.sum(ek[b, h], axis=0, keepdims=True)
            m_scr[b, h] = m_new[b, h]
        for b in range(NB):
            ym_ref[b, rs, :] = (_head_norm(jnp.concatenate([outs[b, h] for h in range(N_HEADS)], axis=1), g_ref)
                                * _sigmoid(mo_ref[b, rs, :]))

    C_ref[...] = C_scr[...]
    for b in range(NB):
        n_ref[b] = jnp.concatenate([n_scr[b, h] for h in range(N_HEADS)], axis=1)
        m_ref[b] = jnp.concatenate([jnp.broadcast_to(m_scr[b, h], (1, LANES // N_HEADS)) for h in range(N_HEADS)],
                                   axis=1)


def mlstm_prompt(mqk, mv, mo, small, conv_w, conv_b, bif_row, norm_g, tril):
    B, T, _ = mqk.shape
    TB = MLSTM_TB
    blk = lambda n: pl.BlockSpec((B, TB, n), lambda t: (0, t, 0))
    const = lambda s: pl.BlockSpec(s, lambda t: (0,) * len(s))
    return pl.pallas_call(
        _mlstm_body,
        grid=(T // TB,),
        in_specs=[blk(2 * BRANCH_W), blk(BRANCH_W), blk(BRANCH_W), blk(LANES), const((CONV_W, 2 * BRANCH_W)),
                  const((1, 2 * BRANCH_W)), const((1, LANES)), const((1, BRANCH_W)), const((REC_L, REC_L))],
        out_specs=[blk(BRANCH_W), const((B, N_HEADS, HEAD_DIM, HEAD_DIM)), const((B, 1, BRANCH_W)),
                   const((B, 1, LANES)), const((B, CONV_W - 1, 2 * BRANCH_W))],
        out_shape=[jax.ShapeDtypeStruct((B, T, BRANCH_W), f32),
                   jax.ShapeDtypeStruct((B, N_HEADS, HEAD_DIM, HEAD_DIM), f32),
                   jax.ShapeDtypeStruct((B, 1, BRANCH_W), f32),
                   jax.ShapeDtypeStruct((B, 1, LANES), f32),
                   jax.ShapeDtypeStruct((B, CONV_W - 1, 2 * BRANCH_W), f32)],
        scratch_shapes=[pltpu.VMEM((B, TB + SUBLANES, 2 * BRANCH_W), f32), pltpu.VMEM((B, TB, BRANCH_W), f32),
                        pltpu.VMEM((B, TB, BRANCH_W), f32), pltpu.VMEM((B, N_HEADS, HEAD_DIM, HEAD_DIM), f32),
                        pltpu.VMEM((B, N_HEADS, 1, HEAD_DIM), f32), pltpu.VMEM((B, N_HEADS, 1, 1), f32)],
        compiler_params=_cparams(("arbitrary",)),
        name="mlstm_prompt",
    )(mqk, mv, mo, small, conv_w, conv_b.reshape(1, -1), bif_row, norm_g.reshape(1, BRANCH_W), tril)


def _gla_log_decay(sm, wa_ref, ba_ref):
    return _log_sigmoid(_mm(sm, wa_ref[...]) + ba_ref[...]) * (1.0 / GLA_TAU)


def _gla_body(gqk_ref, gv_ref, sm_ref, wa_ref, ba_ref, g_ref, lb_ref, ex_ref, yg_ref, S_ref, S_scr):
    t = pl.program_id(1)
    TB, L, R = REC_TB, REC_L, GLA_R
    NK = N_HEADS * GLA_DK

    @pl.when(t == 0)
    def _():
        S_scr[...] = jnp.zeros(S_scr.shape, f32)

    lb = lb_ref[...]
    ex = ex_ref[...]
    s_i = lax.broadcasted_iota(jnp.int32, (R, R, NK), 0)
    t_i = lax.broadcasted_iota(jnp.int32, (R, R, NK), 1)
    causal3 = t_i >= s_i
    lane_t = lax.broadcasted_iota(jnp.int32, (1, L), 1)
    bd_mask = (lax.broadcasted_iota(jnp.int32, (NK, BRANCH_W), 0) // GLA_DK
               == lax.broadcasted_iota(jnp.int32, (NK, BRANCH_W), 1) // HEAD_DIM)
    for c in range(TB // L):
        rs = slice(c * L, (c + 1) * L)
        la = _gla_log_decay(sm_ref[0, rs, :], wa_ref, ba_ref)
        bcl = _mm_sel(lb, la, 3)
        q_c = gqk_ref[0, rs, 0:NK] * GLA_DK ** -0.5
        k_c = gqk_ref[0, rs, NK:2 * NK]
        v_c = gv_ref[0, rs, :]
        bll = jnp.concatenate([jnp.broadcast_to(bcl[(I + 1) * R - 1:(I + 1) * R, :], (R, NK)) for I in range(L // R)],
                              axis=0)
        ktilT = (k_c * jnp.exp(bll - bcl)).T
        dblT = jnp.exp(bll).T
        qin = q_c * jnp.exp(bcl)
        outs = []
        for I in range(L // R):
            sl = slice(I * R, (I + 1) * R)
            S = S_scr[...]
            bb, vv = bcl[sl], v_c[sl]
            d = jnp.where(causal3, bb[None, :, :] - bb[:, None, :], -jnp.inf)
            p = q_c[sl][None, :, :] * k_c[sl][:, None, :] * jnp.exp(d)
            a = _mm(p.reshape(R * R, NK), ex).reshape(R, R, BRANCH_W)
            outs.append(_mm(qin[sl], S) + jnp.sum(a * vv[:, None, :], axis=0))
            in_blk = (lane_t >= I * R) & (lane_t < (I + 1) * R)
            upd = _mm(jnp.where(in_blk, ktilT, 0.0), v_c)
            S_scr[...] = dblT[:, I * R:I * R + 1] * S + jnp.where(bd_mask, upd, 0.0)
        yg_ref[0, rs, :] = _head_norm(jnp.concatenate(outs, axis=0), g_ref)

    for h in range(N_HEADS):
        S_ref[0, h] = S_scr[h * GLA_DK:(h + 1) * GLA_DK, h * HEAD_DIM:(h + 1) * HEAD_DIM]


def gla_prompt(gqk, gv, small, wa, ba, norm_g, lb, ex):
    B, T, _ = gqk.shape
    TB = REC_TB
    blk = lambda n: pl.BlockSpec((1, TB, n), lambda b, t: (b, t, 0))
    const = lambda s: pl.BlockSpec(s, lambda b, t: (0,) * len(s))
    return pl.pallas_call(
        _gla_body,
        grid=(B, T // TB),
        in_specs=[blk(BRANCH_W), blk(BRANCH_W), blk(LANES), const((LANES, LANES)), const((1, LANES)),
                  const((1, BRANCH_W)), const((REC_L, REC_L)), const((LANES, BRANCH_W))],
        out_specs=[blk(BRANCH_W), pl.BlockSpec((1, N_HEADS, GLA_DK, HEAD_DIM), lambda b, t: (b, 0, 0, 0))],
        out_shape=[jax.ShapeDtypeStruct((B, T, BRANCH_W), f32),
                   jax.ShapeDtypeStruct((B, N_HEADS, GLA_DK, HEAD_DIM), f32)],
        scratch_shapes=[pltpu.VMEM((N_HEADS * GLA_DK, BRANCH_W), f32)],
        compiler_params=_cparams(("arbitrary", "arbitrary")),
        name="gla_prompt",
    )(gqk, gv, small, wa, ba.reshape(1, LANES), norm_g.reshape(1, BRANCH_W), lb, ex)


def _lane_replicated_column(row):
    n = row.shape[1]
    eye = lax.broadcasted_iota(jnp.int32, (n, n), 0) == lax.broadcasted_iota(jnp.int32, (n, n), 1)
    col = jnp.sum(jnp.where(eye, jnp.broadcast_to(row, (n, n)), 0.0), axis=1, keepdims=True)
    return jnp.broadcast_to(col, (n, LANES))


def _rows_per_head(x4):
    return jnp.broadcast_to(x4[:, None, :], (N_HEADS, HEAD_DIM, LANES)).reshape(BRANCH_W, LANES)


def _lanes_per_head(col4):
    sel = (lax.broadcasted_iota(jnp.int32, (N_HEADS, BRANCH_W), 1) // HEAD_DIM
           == lax.broadcasted_iota(jnp.int32, (N_HEADS, BRANCH_W), 0))
    return jnp.sum(jnp.where(sel, col4, 0.0), axis=0, keepdims=True)


def _dec_attn_body(lam_init, n_pages, pt_ref, dq_ref, dkn_ref, dvn_ref, sq_ref, dl_ref, g_ref, u_ref, *rest):
    dk_pg, dv_pg, sk_pg, sv_pg = (rest[j * n_pages:(j + 1) * n_pages] for j in range(4))
    ya_ref, ys_ref = rest[4 * n_pages:]
    lam = _diff_lambda(dl_ref, lam_init)

    q_row = dq_ref[0].astype(f32) * LN2
    qx = _lane_replicated_column(q_row)
    s1, s2 = [], []
    for p in range(n_pages):
        r = jnp.sum((dk_pg[p][...] * qx).reshape(N_HEADS, 2, DIFF_D, LANES), axis=2)
        s1.append(r[:, 0, :])
        s2.append(r[:, 1, :])
    lane = lax.broadcasted_iota(jnp.int32, (N_HEADS, BRANCH_W), 1)
    head = lax.broadcasted_iota(jnp.int32, (N_HEADS, BRANCH_W), 0)
    qk_new = jnp.broadcast_to(q_row * dkn_ref[0], (N_HEADS, BRANCH_W))
    w_pages, w_new = [], []
    for m, s in enumerate((s1, s2)):
        lo = head * HEAD_DIM + m * DIFF_D
        s_new = jnp.sum(jnp.where((lane >= lo) & (lane < lo + DIFF_D), qk_new, 0.0), axis=1, keepdims=True)
        smax = s[0]
        for p in range(1, n_pages):
            smax = jnp.maximum(smax, s[p])
        mx = jnp.maximum(jnp.max(smax, axis=1, keepdims=True), s_new)
        e = [jnp.exp(sp - mx) for sp in s]
        e_new = jnp.exp(s_new - mx)
        tot = e[0]
        for p in range(1, n_pages):
            tot = tot + e[p]
        inv = 1.0 / (jnp.sum(tot, axis=1, keepdims=True) + e_new)
        w_pages.append([ep * inv for ep in e])
        w_new.append(e_new * inv)
    acc = jnp.zeros((BRANCH_W, LANES), f32)
    for p in range(n_pages):
        acc = acc + _rows_per_head(w_pages[0][p] - lam * w_pages[1][p]) * dv_pg[p][...]
    y = jnp.sum(acc.T, axis=0, keepdims=True) + _lanes_per_head(w_new[0] - lam * w_new[1]) * dvn_ref[0]
    ya_ref[0] = _head_norm(y, g_ref) * (1.0 - lam_init)

    qx = _lane_replicated_column(sq_ref[0].astype(f32) * LN2)
    z = jnp.concatenate([jnp.sum((sk_pg[p][...] * qx).reshape(N_HEADS, HEAD_DIM, LANES), axis=1)
                         for p in range(n_pages)], axis=0)
    sp = _softplus_neg_abs(z)
    ls = jnp.minimum(z, 0.0) - sp
    r = _sel_mm(-jnp.maximum(z, 0.0) - sp, u_ref[...], 2)
    base = ls + r[:, :LANES]
    carry = jnp.zeros((N_HEADS, LANES), f32)
    acc = jnp.zeros((BRANCH_W, LANES), f32)
    for p in reversed(range(n_pages)):
        rows = slice(p * N_HEADS, (p + 1) * N_HEADS)
        acc = acc + _rows_per_head(jnp.exp(base[rows] + carry)) * sv_pg[p][...]
        carry = carry + r[rows, LANES:]
    ys_ref[0] = jnp.sum(acc.T, axis=0, keepdims=True)


def _pages_as_feature_by_position(cache):
    d, n, pg, h, hd = cache.shape
    return jnp.transpose(cache, (0, 1, 3, 4, 2)).reshape(d, n, h * hd, pg)


def dec_attn(layer, lam_init, page_table, dqb, dk_new, dv_new, sqb, diff_lambda, norm_g, u, caches_t):
    B, n_pages = page_table.shape
    page = caches_t[0].shape[3]
    assert page == LANES
    row3 = lambda a: a.reshape(B, 1, BRANCH_W)
    rspec = pl.BlockSpec((1, 1, BRANCH_W), lambda b, pt: (b, 0, 0))
    const = lambda shp: pl.BlockSpec(shp, lambda b, pt: (0,) * len(shp))

    def pg(p):
        return pl.BlockSpec((None, None, BRANCH_W, page), lambda b, pt: (layer, pt[b * n_pages + p], 0, 0))

    in_specs = [rspec, rspec, rspec, rspec, const((4, DIFF_D)), const((1, BRANCH_W)), const((LANES, 2 * LANES))]
    args = [row3(dqb), row3(dk_new), row3(dv_new), row3(sqb), diff_lambda, norm_g.reshape(1, BRANCH_W), u]
    for c in caches_t:
        for p in range(n_pages):
            in_specs.append(pg(p))
            args.append(c)
    ya, ys = pl.pallas_call(
        functools.partial(_dec_attn_body, lam_init, n_pages),
        grid_spec=pltpu.PrefetchScalarGridSpec(
            num_scalar_prefetch=1, grid=(B,), in_specs=in_specs,
            out_specs=[pl.BlockSpec((1, 1, BRANCH_W), lambda b, pt: (b, 0, 0))] * 2),
        out_shape=[jax.ShapeDtypeStruct((B, 1, BRANCH_W), f32)] * 2,
        compiler_params=_cparams(("arbitrary",)),
        name="dec_attn",
    )(page_table.reshape(-1), *args)
    return ya.reshape(B, BRANCH_W), ys.reshape(B, BRANCH_W)


def _expand_heads(x4, width):
    return jnp.concatenate([jnp.broadcast_to(x4[:, h:h + 1], (x4.shape[0], width)) for h in range(N_HEADS)], axis=1)


def _dec_rec_body(mqk_ref, mv_ref, mo_ref, sm_ref, gqk_ref, gv_ref, conv_ref, C_ref, n_ref, m_ref, S_ref,
                  cw_ref, cb_ref, bif_ref, mg_ref, wa_ref, ba_ref, gg_ref,
                  ym_ref, yg_ref, conv_o, C_o, n_o, m_o, S_o, CT_scr, ST_scr, numT_scr, oT_scr):
    h = pl.program_id(0)
    W = 2 * BRANCH_W
    NK = N_HEADS * GLA_DK
    u = mqk_ref[...]
    rows = [conv_ref[:, j * W:(j + 1) * W] for j in range(CONV_W - 1)] + [u]
    act = _conv_taps(cw_ref, cb_ref, rows)
    q = act[:, :BRANCH_W]
    k = act[:, BRANCH_W:] * HEAD_DIM ** -0.5
    conv_o[...] = jnp.concatenate(rows[1:], axis=1)

    gi = sm_ref[...] + bif_ref[...]
    ig = gi[:, SM_I:SM_I + N_HEADS]
    lf = _log_sigmoid(gi)[:, SM_F:SM_F + N_HEADS]
    m_old = m_ref[...]
    m_new = jnp.maximum(lf + m_old, ig)
    sc = _expand_heads(jnp.exp(lf + m_old - m_new), HEAD_DIM)
    ek = _expand_heads(jnp.exp(ig - m_new), HEAD_DIM) * k
    n_new = sc * n_ref[...] + ek
    m_o[...] = m_new
    n_o[...] = n_new

    la = _gla_log_decay(sm_ref[...], wa_ref, ba_ref)
    gq = gqk_ref[:, 0:NK] * GLA_DK ** -0.5
    gk = gqk_ref[:, NK:2 * NK]

    r64 = pl.ds(pl.multiple_of(h * HEAD_DIM, HEAD_DIM), HEAD_DIM)
    r32 = pl.ds(pl.multiple_of(h * GLA_DK, GLA_DK), GLA_DK)
    numT_scr[0] = q.T
    numT_scr[1] = ek.T
    numT_scr[2] = mv_ref[...].T
    numT_scr[3] = sc.T
    qT, ekT, vT = numT_scr[0, r64, :], numT_scr[1, r64, :], numT_scr[2, r64, :]
    scT = numT_scr[3, pl.ds(h * HEAD_DIM, 1), :]
    CT_scr[...] = C_ref[...].T
    num = jnp.zeros((HEAD_DIM, LANES), f32)
    for kk in range(HEAD_DIM):
        blk = slice(kk * HEAD_DIM, (kk + 1) * HEAD_DIM)
        new = scT * CT_scr[blk, :] + ekT[kk:kk + 1, :] * vT
        CT_scr[blk, :] = new
        num = num + qT[kk:kk + 1, :] * new
    C_o[...] = CT_scr[...].T
    numT_scr[4, r64, :] = num

    oT_scr[0, 0:NK, :] = gq.T
    oT_scr[0, NK:2 * NK, :] = gk.T
    oT_scr[1, 0:NK, :] = jnp.exp(la).T
    oT_scr[2] = gv_ref[...].T
    gqT, gkT, decT = oT_scr[0, r32, :], oT_scr[0, pl.ds(pl.multiple_of(NK + h * GLA_DK, GLA_DK), GLA_DK), :], \
        oT_scr[1, r32, :]
    gvT = oT_scr[2, r64, :]
    ST_scr[...] = S_ref[...].T
    o = jnp.zeros((HEAD_DIM, LANES), f32)
    for kk in range(GLA_DK):
        blk = slice(kk * HEAD_DIM, (kk + 1) * HEAD_DIM)
        new = decT[kk:kk + 1, :] * ST_scr[blk, :] + gkT[kk:kk + 1, :] * gvT
        ST_scr[blk, :] = new
        o = o + gqT[kk:kk + 1, :] * new
    S_o[...] = ST_scr[...].T
    oT_scr[3, r64, :] = o

    @pl.when(h == N_HEADS - 1)
    def _():
        qn = jnp.concatenate([jnp.sum((q * n_new)[:, g * HEAD_DIM:(g + 1) * HEAD_DIM], axis=1, keepdims=True)
                              for g in range(N_HEADS)], axis=1)
        den = jnp.maximum(jnp.abs(qn), jnp.exp(-m_new))
        hm = numT_scr[4].T / _expand_heads(den, HEAD_DIM)
        ym_ref[...] = _head_norm(hm, mg_ref) * _sigmoid(mo_ref[...])
        yg_ref[...] = _head_norm(oT_scr[3].T, gg_ref)


def dec_rec(mqk, mv, mo, small, gqk, gv, conv, C, n, m, S, conv_w, conv_b, bif_row, mnorm_g, wa, ba, gnorm_g):
    B = mqk.shape[0]
    CW = HEAD_DIM * HEAD_DIM
    SW = GLA_DK * HEAD_DIM
    full = lambda a: pl.BlockSpec(a.shape, lambda h: (0,) * a.ndim)
    ins = [mqk, mv, mo, small, gqk, gv, conv.reshape(B, -1), C.reshape(B, N_HEADS * CW), n.reshape(B, BRANCH_W), m,
           S.reshape(B, N_HEADS * SW), conv_w, conv_b.reshape(1, -1), bif_row, mnorm_g.reshape(1, BRANCH_W), wa,
           ba.reshape(1, LANES), gnorm_g.reshape(1, BRANCH_W)]
    in_specs = [full(a) for a in ins]
    in_specs[7] = pl.BlockSpec((B, CW), lambda h: (0, h))
    in_specs[10] = pl.BlockSpec((B, SW), lambda h: (0, h))
    out_shape = [jax.ShapeDtypeStruct((B, BRANCH_W), f32), jax.ShapeDtypeStruct((B, BRANCH_W), f32),
                 jax.ShapeDtypeStruct((B, (CONV_W - 1) * 2 * BRANCH_W), f32),
                 jax.ShapeDtypeStruct((B, N_HEADS * CW), f32), jax.ShapeDtypeStruct((B, BRANCH_W), f32),
                 jax.ShapeDtypeStruct((B, N_HEADS), f32), jax.ShapeDtypeStruct((B, N_HEADS * SW), f32)]
    out_specs = [pl.BlockSpec(s.shape, lambda h: (0, 0)) for s in out_shape]
    out_specs[3] = pl.BlockSpec((B, CW), lambda h: (0, h))
    out_specs[6] = pl.BlockSpec((B, SW), lambda h: (0, h))
    ym, yg, conv_n, C_n, n_n, m_n, S_n = pl.pallas_call(
        _dec_rec_body,
        grid=(N_HEADS,),
        in_specs=in_specs, out_specs=out_specs, out_shape=out_shape,
        scratch_shapes=[pltpu.VMEM((CW, B), f32), pltpu.VMEM((SW, B), f32), pltpu.VMEM((5, BRANCH_W, B), f32),
                        pltpu.VMEM((4, BRANCH_W, B), f32)],
        compiler_params=_cparams(("arbitrary",)),
        name="dec_rec",
    )(*ins)
    return (ym, yg, conv_n.reshape(B, CONV_W - 1, 2 * BRANCH_W), C_n.reshape(B, N_HEADS, HEAD_DIM, HEAD_DIM),
            n_n.reshape(B, N_HEADS, HEAD_DIM), m_n, S_n.reshape(B, N_HEADS, GLA_DK, HEAD_DIM))


def _constants():
    t = np.arange(REC_L)
    tril = (t[None, :] <= t[:, None]).astype(np.float32)
    lb = tril * (t[None, :] // GLA_R == t[:, None] // GLA_R)
    ex = (np.arange(N_HEADS * GLA_DK)[:, None] // GLA_DK == np.arange(BRANCH_W)[None, :] // HEAD_DIM)
    cast = lambda a: jnp.asarray(a, f32).astype(MXU_DTYPE)
    return dict(tril=cast(tril), lb=cast(lb), ex=cast(ex), u_page=cast(_suffix_matrix(LANES)),
                u_sb=cast(_suffix_matrix(SB_TK)[:, :SB_TK]))


def _layer_params(l, prm):
    bif = prm['mlstm_b_if'][l].astype(f32).reshape(1, 2 * N_HEADS)
    wa = jnp.zeros((LANES, LANES), f32).at[SM_A:SM_A + GLA_LOWRANK].set(prm['gla_w_a2'][l])
    return dict(
        lam_init=0.8 - 0.6 * math.exp(-0.3 * l),
        bif_row=jnp.pad(bif, ((0, 0), (0, LANES - 2 * N_HEADS))),
        wa=wa.astype(MXU_DTYPE),
    )


def _prep_weights(w_in, w_gate, w_branch, w_out, w_ple, w_ple_gate):
    c = lambda a: a.astype(MXU_DTYPE)
    o = _ORIG
    parts = [w_in[:, :, :o['mi']], w_in[:, :, o['mo']:o['ga']], w_in[:, :, o['sq']:], w_in[:, :, o['mi']:o['mo']],
             w_in[:, :, o['ga']:o['sq']]]
    used = sum(p.shape[2] for p in parts)
    w_perm = jnp.concatenate([c(p) for p in parts]
                             + [jnp.zeros(w_in.shape[:2] + (N_PROJ - used,), MXU_DTYPE)], axis=2)
    return w_perm, c(w_gate), c(w_branch), c(w_out), c(w_ple), c(w_ple_gate)


def _prompt_layer(l, x, p_l, prm, wts, consts):
    B, T, _ = x.shape
    lp = _layer_params(l, prm)
    w_perm, w_gate, w_branch, w_out, w_ple, w_ple_gate = wts
    x2 = x.reshape(B * T, D_MODEL)
    (dqb, dk, dkb, dv, dvh, mqk, mv, mo, gqk, gv, sqb, sk, skb, sv, svh, z, small) = in_proj(
        x2, prm['g_pre'][l], w_perm[l], seq_len=T)
    r3 = lambda a: a.reshape(B, T, a.shape[-1])
    hm = lambda a: a.reshape(N_HEADS, B, T, LANES)
    ya = diff_attn(lp['lam_init'], r3(dqb), r3(dkb), hm(dvh), prm['diff_lambda'][l], prm['diff_norm_g'][l])
    ys = sb_attn(r3(sqb), r3(skb), hm(svh), consts['u_sb'])
    ym, C1, n1, m1, conv1 = mlstm_prompt(r3(mqk), r3(mv), r3(mo), r3(small), prm['mlstm_conv_w'][l],
                                         prm['mlstm_conv_b'][l], lp['bif_row'], prm['mlstm_norm_g'][l],
                                         consts['tril'])
    yg, S1 = gla_prompt(r3(gqk), r3(gv), r3(small), lp['wa'], _pad_ba(prm['gla_b_a'][l]), prm['gla_norm_g'][l],
                        consts['lb'], consts['ex'])
    flat = lambda a: a.reshape(B * T, BRANCH_W)
    y = out_proj(x2, p_l.reshape(B * T, P_DIM), flat(ya), flat(ym), flat(yg), flat(ys), z, prm['g_pre'][l],
                 prm['g_post'][l], w_gate[l], w_branch[l], w_out[l], w_ple[l], w_ple_gate[l])
    hd = lambda a: jnp.transpose(a.reshape(B, N_HEADS, HEAD_DIM, T), (0, 3, 1, 2))
    state = (hd(dk), hd(dv), hd(sk), hd(sv), C1, n1.reshape(B, N_HEADS, HEAD_DIM),
             m1[:, 0, ::LANES // N_HEADS], conv1, S1)
    return y.reshape(B, T, D_MODEL), state


def _pad_ba(ba):
    return ba.astype(f32)


def _decode_layer(l, x, p_l, prm, wts, consts, caches, states, page_table):
    B = x.shape[0]
    lp = _layer_params(l, prm)
    w_perm, w_gate, w_branch, w_out, w_ple, w_ple_gate = wts
    x2 = x.reshape(B, D_MODEL)
    (dqb, dk, dkb, dv, dvh, mqk, mv, mo, gqk, gv, sqb, sk, skb, sv, svh, z, small) = in_proj(
        x2, prm['g_pre'][l], w_perm[l])
    ya, ys = dec_attn(l, lp['lam_init'], page_table, dqb, dk, dv, sqb, prm['diff_lambda'][l], prm['diff_norm_g'][l],
                      consts['u_page'], caches)
    C0, n0, m0, conv0, S0 = states
    ym, yg, conv1, C1, n1, m1, S1 = dec_rec(mqk, mv, mo, small, gqk, gv, conv0[l], C0[l], n0[l], m0[l], S0[l],
                                            prm['mlstm_conv_w'][l], prm['mlstm_conv_b'][l], lp['bif_row'],
                                            prm['mlstm_norm_g'][l], lp['wa'], _pad_ba(prm['gla_b_a'][l]),
                                            prm['gla_norm_g'][l])
    y = out_proj(x2, p_l.reshape(B, P_DIM), ya, ym, yg, ys, z, prm['g_pre'][l], prm['g_post'][l], w_gate[l],
                 w_branch[l], w_out[l], w_ple[l], w_ple_gate[l])
    hd = lambda a: a.reshape(B, 1, N_HEADS, HEAD_DIM)
    state = (hd(dk), hd(dv), hd(sk), hd(sv), C1, n1, m1, conv1, S1)
    return y.reshape(B, 1, D_MODEL), state


def kernel(x_prompt, x_sample, cache_diff_k, cache_diff_v, cache_sb_k, cache_sb_v, state_mlstm_C, state_mlstm_n,
           state_mlstm_m, state_mlstm_conv, state_gla_S, page_table, p_prompt, p_sample, g_pre, g_post, w_in,
           diff_lambda, diff_norm_g, mlstm_conv_w, mlstm_conv_b, mlstm_b_if, mlstm_norm_g, gla_w_a2, gla_b_a,
           gla_norm_g, w_branch, w_gate, w_out, w_ple, w_ple_gate):
    prm = dict(g_pre=g_pre, g_post=g_post, diff_lambda=diff_lambda, diff_norm_g=diff_norm_g,
               mlstm_conv_w=mlstm_conv_w, mlstm_conv_b=mlstm_conv_b, mlstm_b_if=mlstm_b_if,
               mlstm_norm_g=mlstm_norm_g, gla_w_a2=gla_w_a2, gla_b_a=gla_b_a, gla_norm_g=gla_norm_g)
    wts = _prep_weights(w_in, w_gate, w_branch, w_out, w_ple, w_ple_gate)
    consts = _constants()
    depth = w_in.shape[0]
    caches = tuple(_pages_as_feature_by_position(c) for c in (cache_diff_k, cache_diff_v, cache_sb_k, cache_sb_v))
    states = (state_mlstm_C, state_mlstm_n, state_mlstm_m, state_mlstm_conv, state_gla_S)
    y_p, y_s = x_prompt, x_sample
    st_p, st_s = [], []
    for l in range(depth):
        y_p, s_p = _prompt_layer(l, y_p, p_prompt[l], prm, wts, consts)
        y_s, s_s = _decode_layer(l, y_s, p_sample[l], prm, wts, consts, caches, states, page_table)
        st_p.append(s_p)
        st_s.append(s_s)
    outs_p = [jnp.stack(t) for t in zip(*st_p)]
    outs_s = [jnp.stack(t) for t in zip(*st_s)]
    return (y_p, y_s, *outs_p, *outs_s)
```

```python
import functools
import math

import numpy as np
import jax
import jax.numpy as jnp
from jax import lax
from jax.experimental import pallas as pl
from jax.experimental.pallas import tpu as pltpu

f32 = jnp.float32
MXU_DTYPE = jnp.bfloat16

D_MODEL = 1024
N_BRANCH = 4
BRANCH_W = 256
N_HEADS = 4
HEAD_DIM = 64
DIFF_D = 32
GLA_DK = 32
GLA_LOWRANK = 16
GLA_TAU = 16.0
CONV_W = 4
EPS = 1e-6
P_DIM = 256
LOG2E = math.log2(math.e)
LN2 = math.log(2.0)

LANES = 128
SUBLANES = 8
VMEM_LIMIT = 56 * 1024 * 1024

_ORIG = dict(dq=0, dk=256, dv=512, mq=768, mk=1024, mv=1280, mi=1536, mf=1540, mo=1544, gq=1800,
             gk=1928, gv=2056, ga=2312, sq=2328, sk=2584, sv=2840, z=3096, end=4120)
N_PROJ = 4224
C_DQ, C_DK, C_DV, C_MQK, C_MV, C_MO, C_GQK, C_GV, C_SQ, C_SK, C_SV, C_Z, C_SM = (
    0, 256, 512, 768, 1280, 1536, 1792, 2048, 2304, 2560, 2816, 3072, 4096)
SM_I, SM_F, SM_A = 0, 4, 8

ATT_TQ = 512
DIFF_TK = 1024
SB_TK = 256
DIFF_HEADS_PER_STEP = 2
SB_HEADS_PER_STAGE = 2
REC_TB = 512
MLSTM_TB = 256
REC_L = 128
GLA_R = 32


def _cparams(sem):
    return pltpu.CompilerParams(dimension_semantics=sem, vmem_limit_bytes=VMEM_LIMIT)


def _mm(a, b):
    return jnp.dot(a.astype(MXU_DTYPE), b.astype(MXU_DTYPE), preferred_element_type=f32)


def _mm_nt(a, b):
    return lax.dot_general(a.astype(MXU_DTYPE), b.astype(MXU_DTYPE), (((1,), (1,)), ((), ())),
                           preferred_element_type=f32)


def _split_terms(a, n):
    if MXU_DTYPE == f32:
        return [a]
    out, r = [], a
    for i in range(n - 1):
        top = lax.bitcast_convert_type(lax.bitcast_convert_type(r, jnp.int32) & jnp.int32(-65536), f32)
        out.append(top.astype(MXU_DTYPE))
        r = r - top
    out.append(r.astype(MXU_DTYPE))
    return out


def _sel_mm(a, sel, n):
    acc = None
    for p in _split_terms(a, n):
        t = jnp.dot(p, sel, preferred_element_type=f32)
        acc = t if acc is None else acc + t
    return acc


def _mm_sel(sel, a, n):
    acc = None
    for p in _split_terms(a, n):
        t = jnp.dot(sel, p, preferred_element_type=f32)
        acc = t if acc is None else acc + t
    return acc


def _rms(x, g):
    return x * lax.rsqrt(jnp.mean(x * x, axis=-1, keepdims=True) + EPS) * g


def _sigmoid(x):
    return jax.nn.sigmoid(x)


def _softplus_neg_abs(x):
    return jnp.log(1.0 + jnp.exp(-jnp.abs(x)))


def _log_sigmoid(x):
    return jnp.minimum(x, 0.0) - _softplus_neg_abs(x)


def _head_norm(y, g_ref):
    outs = []
    for h in range(N_HEADS):
        yh = y[:, h * HEAD_DIM:(h + 1) * HEAD_DIM]
        outs.append(_rms(yh, g_ref[:, h * HEAD_DIM:(h + 1) * HEAD_DIM]))
    return jnp.concatenate(outs, axis=1)


def _store_heads(o_ref, t, fill):
    rows = t.shape[0]
    tail = jnp.where(lax.broadcasted_iota(jnp.int32, (rows, LANES - HEAD_DIM), 1) == 0, fill, 0.0)
    for h in range(N_HEADS):
        o_ref[h] = jnp.concatenate([t[:, h * HEAD_DIM:(h + 1) * HEAD_DIM], tail], axis=1).astype(o_ref.dtype)


def _in_proj_body(rows_by_feature, x_ref, g_ref, w_ref, dqb, dk, dkb, dv, dvh, mqk, mv, mo, gqk, gv, sqb, sk, skb, sv,
                  svh, z, small):
    hb = _rms(x_ref[...], g_ref[...]).astype(MXU_DTYPE)

    def proj(a, b):
        return jnp.dot(hb, w_ref[:, a:b], preferred_element_type=f32)

    def store_rows(o_ref, t):
        o_ref[...] = t.T if rows_by_feature else t

    dqb[...] = (proj(C_DQ, C_DK) * (DIFF_D ** -0.5 * LOG2E)).astype(dqb.dtype)
    t = proj(C_DK, C_DV)
    store_rows(dk, t)
    dkb[...] = t.astype(dkb.dtype)
    t = proj(C_DV, C_MQK)
    store_rows(dv, t)
    _store_heads(dvh, t, 1.0)
    mqk[...] = proj(C_MQK, C_MV)
    mv[...] = proj(C_MV, C_MO)
    mo[...] = proj(C_MO, C_GQK)
    gqk[...] = proj(C_GQK, C_GV)
    gv[...] = proj(C_GV, C_SQ)
    sqb[...] = (proj(C_SQ, C_SK) * (HEAD_DIM ** -0.5 * LOG2E)).astype(sqb.dtype)
    t = proj(C_SK, C_SV)
    store_rows(sk, t)
    skb[...] = t.astype(skb.dtype)
    t = proj(C_SV, C_Z)
    store_rows(sv, t)
    _store_heads(svh, t, 0.0)
    z[...] = proj(C_Z, C_SM)
    small[...] = proj(C_SM, N_PROJ)


def _layer_weight_spec(w, layer):
    return pl.BlockSpec((None,) + w.shape[1:], lambda i: (layer,) + (0,) * (w.ndim - 1),
                        pipeline_mode=pl.Buffered(1))


def in_proj(x, g_pre, w_perm, layer, seq_len=None):
    R = x.shape[0]
    tm = min(R, 512)
    HM = "head-major"
    KV = "kv-rows"
    widths = [(256, MXU_DTYPE), (KV, f32), (256, MXU_DTYPE), (KV, f32), (HM, MXU_DTYPE), (512, f32), (256, f32),
              (256, f32), (256, f32), (256, f32), (256, MXU_DTYPE), (KV, f32), (256, MXU_DTYPE), (KV, f32),
              (HM, MXU_DTYPE), (1024, f32), (LANES, f32)]
    if seq_len is not None:
        assert seq_len % tm == 0 and R % seq_len == 0
        nt = seq_len // tm

    def row(n):
        if n == HM:
            return pl.BlockSpec((N_HEADS, tm, LANES), lambda i: (0, i, 0))
        if n == KV and seq_len is not None:
            return pl.BlockSpec((None, BRANCH_W, tm), lambda i: (i // nt, 0, i % nt))
        return pl.BlockSpec((tm, BRANCH_W if n == KV else n), lambda i: (i, 0))

    def shape(n):
        if n == HM:
            return (N_HEADS, R, LANES)
        if n == KV:
            return (R // seq_len, BRANCH_W, seq_len) if seq_len is not None else (R, BRANCH_W)
        return (R, n)

    const = lambda s: pl.BlockSpec(s, lambda i: (0, 0), pipeline_mode=pl.Buffered(1))
    return pl.pallas_call(
        functools.partial(_in_proj_body, seq_len is not None),
        grid=(R // tm,),
        in_specs=[row(D_MODEL), const((1, D_MODEL)), _layer_weight_spec(w_perm, layer)],
        out_specs=[row(n) for n, _ in widths],
        out_shape=[jax.ShapeDtypeStruct(shape(n), d) for n, d in widths],
        compiler_params=_cparams(("arbitrary",)),
        name="in_proj",
    )(x, g_pre.reshape(1, D_MODEL), w_perm)


def _out_proj_body(x_ref, p_ref, ya, ym, yg, ys, z_ref, gpre, gpost, wg, wb, wo, wple, wpg, o_ref):
    x = x_ref[...]
    hb = _rms(x, gpre[...]).astype(MXU_DTYPE)
    acc = None
    for n, y in enumerate((ya, ym, yg, ys)):
        zz = z_ref[:, n * BRANCH_W:(n + 1) * BRANCH_W]
        br = y[...] * (zz * _sigmoid(zz))
        pb = _mm(br, wb[n])
        gt = _sigmoid(jnp.dot(hb, wg[:, n * D_MODEL:(n + 1) * D_MODEL], preferred_element_type=f32))
        acc = gt * pb if acc is None else acc + gt * pb
    x1 = x + _rms(_mm(acc, wo[...]), gpost[...])
    o_ref[...] = x1 + _sigmoid(_mm(x1, wpg[...])) * _mm(p_ref[...], wple[...])


def out_proj(x, p, ya, ym, yg, ys, z, g_pre, g_post, layer, w_gate, w_branch, w_out, w_ple, w_ple_gate):
    R = x.shape[0]
    tm = min(R, 256)
    row = lambda n: pl.BlockSpec((tm, n), lambda i: (i, 0))
    const = lambda s: pl.BlockSpec(s, lambda i: (0,) * len(s), pipeline_mode=pl.Buffered(1))
    return pl.pallas_call(
        _out_proj_body,
        grid=(R // tm,),
        in_specs=[row(D_MODEL), row(P_DIM), row(BRANCH_W), row(BRANCH_W), row(BRANCH_W), row(BRANCH_W), row(D_MODEL),
                  const((1, D_MODEL)), const((1, D_MODEL))]
                 + [_layer_weight_spec(w, layer) for w in (w_gate, w_branch, w_out, w_ple, w_ple_gate)],
        out_specs=row(D_MODEL),
        out_shape=jax.ShapeDtypeStruct((R, D_MODEL), f32),
        compiler_params=_cparams(("arbitrary",)),
        name="out_proj",
    )(x, p, ya, ym, yg, ys, z, g_pre.reshape(1, D_MODEL), g_post.reshape(1, D_MODEL), w_gate, w_branch, w_out,
      w_ple, w_ple_gate)


def _diff_lambda(dl_ref, lam_init):
    dl = dl_ref[...]
    return (jnp.exp(jnp.sum(dl[0:1] * dl[1:2], keepdims=True)) - jnp.exp(jnp.sum(dl[2:3] * dl[3:4], keepdims=True))
            + lam_init)


def _diff_masks(q):
    lane = lax.broadcasted_iota(jnp.int32, (1, BRANCH_W), 1)
    out = []
    for h in range(N_HEADS):
        for m in range(2):
            lo = h * HEAD_DIM + m * DIFF_D
            out.append(jnp.where((lane >= lo) & (lane < lo + DIFF_D), q, jnp.zeros_like(q)))
    return out


def _diff_attn_body(lam_init, q_ref, k_ref, v_ref, dl_ref, g_ref, o_ref, qm_scr, m_scr, acc_scr):
    i = pl.program_id(1)
    tq, big = ATT_TQ, DIFF_TK // ATT_TQ
    for idx, qq in enumerate(_diff_masks(q_ref[0])):
        qm_scr[idx] = qq
    m_scr[...] = jnp.full(m_scr.shape, -jnp.inf, f32)
    acc_scr[...] = jnp.zeros(acc_scr.shape, f32)

    def chunk(k0, width, masked):
        k = k_ref[0, pl.ds(k0, width), :]
        if masked:
            valid = (lax.broadcasted_iota(jnp.int32, (tq, width), 1) + k0
                     <= lax.broadcasted_iota(jnp.int32, (tq, width), 0) + i * tq)

        def head_pair(hp, carry):
            idxs = [DIFF_HEADS_PER_STEP * 2 * hp + j for j in range(2 * DIFF_HEADS_PER_STEP)]
            vs = [v_ref[DIFF_HEADS_PER_STEP * hp + j, pl.ds(k0, width), :] for j in range(DIFF_HEADS_PER_STEP)]
            scores = [_mm_nt(qm_scr[idx], k) for idx in idxs]
            for j0 in range(0, len(idxs), 2):
                js = (j0, j0 + 1)
                ss = [jnp.where(valid, scores[j], -jnp.inf) if masked else scores[j] for j in js]
                m_old = [m_scr[idxs[j]] for j in js]
                m_new = [jnp.maximum(mo, jnp.max(s, axis=1, keepdims=True)) for mo, s in zip(m_old, ss)]
                pv = [_mm(jnp.exp2(s - mn), vs[j0 // 2]) for s, mn in zip(ss, m_new)]
                for j, mo, mn, x in zip(js, m_old, m_new, pv):
                    acc_scr[idxs[j]] = jnp.exp2(mo - mn) * acc_scr[idxs[j]] + x
                    m_scr[idxs[j]] = mn
            return carry

        lax.fori_loop(0, N_HEADS // DIFF_HEADS_PER_STEP, head_pair, 0)

    assert big == 2
    for top in (1, 2, 3):
        if top == 1:
            cond = i == 0
        elif top == 2:
            cond = i % 2 == 1
        else:
            cond = (i % 2 == 0) & (i > 0)

        @pl.when(cond)
        def _(top=top):
            chunk(pl.multiple_of((i + 1 - top) * tq, tq), top * tq, True)

    lax.fori_loop(0, jnp.where(i == 0, 0, (i - 1) // 2),
                  lambda j, c: (chunk(pl.multiple_of(j * DIFF_TK, DIFF_TK), DIFF_TK, False), c)[1], 0)

    lam = _diff_lambda(dl_ref, lam_init)
    outs = []
    for h in range(N_HEADS):
        a1, a2 = acc_scr[2 * h], acc_scr[2 * h + 1]
        o1 = a1[:, :HEAD_DIM] * (1.0 / a1[:, HEAD_DIM:HEAD_DIM + 1])
        o2 = a2[:, :HEAD_DIM] * (1.0 / a2[:, HEAD_DIM:HEAD_DIM + 1])
        outs.append(o1 - lam * o2)
    o_ref[0] = _head_norm(jnp.concatenate(outs, axis=1), g_ref) * (1.0 - lam_init)


def diff_attn(lam_init, qb, kb, vh, diff_lambda, norm_g):
    B, T, _ = qb.shape
    tq = ATT_TQ
    return pl.pallas_call(
        functools.partial(_diff_attn_body, lam_init),
        grid=(B, T // tq),
        in_specs=[pl.BlockSpec((1, tq, BRANCH_W), lambda b, i: (b, i, 0)),
                  pl.BlockSpec((1, T, BRANCH_W), lambda b, i: (b, 0, 0), pipeline_mode=pl.Buffered(1)),
                  pl.BlockSpec((N_HEADS, None, T, LANES), lambda b, i: (0, b, 0, 0), pipeline_mode=pl.Buffered(1)),
                  pl.BlockSpec((4, DIFF_D), lambda b, i: (0, 0)),
                  pl.BlockSpec((1, BRANCH_W), lambda b, i: (0, 0))],
        out_specs=pl.BlockSpec((1, tq, BRANCH_W), lambda b, i: (b, i, 0)),
        out_shape=jax.ShapeDtypeStruct((B, T, BRANCH_W), f32),
        scratch_shapes=[pltpu.VMEM((2 * N_HEADS, tq, BRANCH_W), MXU_DTYPE), pltpu.VMEM((2 * N_HEADS, tq, 1), f32),
                        pltpu.VMEM((2 * N_HEADS, tq, LANES), f32)],
        compiler_params=_cparams(("arbitrary", "arbitrary")),
        name="diff_attn",
    )(qb, kb, vh, diff_lambda, norm_g.reshape(1, BRANCH_W))


def _head_masks(q):
    lane = lax.broadcasted_iota(jnp.int32, (1, BRANCH_W), 1)
    return [jnp.where((lane >= h * HEAD_DIM) & (lane < (h + 1) * HEAD_DIM), q, jnp.zeros_like(q))
            for h in range(N_HEADS)]


def _suffix_matrix(n):
    s = np.arange(n)[:, None]
    j = np.arange(n)[None, :]
    return np.concatenate([(s > j).astype(np.float32), np.ones((n, LANES), np.float32)], axis=1)


def _sb_groups(z2s, cs, u, valid):
    n = z2s[0].shape[1]
    ls, lf = [], []
    for z2 in z2s:
        sp = jnp.log2(1.0 + jnp.exp2(-jnp.abs(z2)))
        ls.append(jnp.minimum(z2, 0.0) - sp)
        f = ls[-1] - z2
        lf.append(f if valid is None else jnp.where(valid, f, 0.0))
    rs = [_mm(f, u) for f in lf]
    out = []
    for l, f, r, c in zip(ls, lf, rs, cs):
        a = jnp.exp2(l + r + jnp.concatenate([c] * (n // LANES), axis=1))
        out.append((a if valid is None else jnp.where(valid, a, 0.0),
                    c + jnp.broadcast_to(r[:, 0:1] + f[:, 0:1], c.shape)))
    return out


def _sb_attn_body(q_ref, k_ref, v_ref, u_ref, o_ref, qm_scr, c_scr, acc_scr):
    i = pl.program_id(1)
    tq, tk = ATT_TQ, SB_TK
    for h, qq in enumerate(_head_masks(q_ref[0])):
        qm_scr[h] = qq
    c_scr[...] = jnp.zeros(c_scr.shape, f32)
    acc_scr[...] = jnp.zeros(acc_scr.shape, f32)
    row = lax.broadcasted_iota(jnp.int32, (tq, tk), 0) + i * tq
    col = lax.broadcasted_iota(jnp.int32, (tq, tk), 1)
    u = u_ref[...]

    def group(g, masked):
        k0 = pl.multiple_of(g * tk, tk)
        k = k_ref[0, pl.ds(k0, tk), :]
        valid = (col + k0 < row) if masked else None

        zs = [_mm_nt(qm_scr[h], k) for h in range(N_HEADS)]
        for h0 in range(0, N_HEADS, SB_HEADS_PER_STAGE):
            hs = range(h0, h0 + SB_HEADS_PER_STAGE)
            res = _sb_groups([zs[h] for h in hs], [c_scr[h] for h in hs], u, valid)
            pv = [_mm(r[0], v_ref[h, pl.ds(k0, tk), :]) for h, r in zip(hs, res)]
            for h, r, x in zip(hs, res, pv):
                acc_scr[h] = acc_scr[h] + x
                c_scr[h] = r[1]

    n_diag = tq // tk
    for d in range(n_diag):
        group((i + 1) * n_diag - 1 - d, True)
    lax.fori_loop(0, i * n_diag, lambda n, c: (group(i * n_diag - 1 - n, False), c)[1], 0)
    o_ref[0] = jnp.concatenate([acc_scr[h][:, :HEAD_DIM] for h in range(N_HEADS)], axis=1)


def sb_attn(qb, kb, vh, u):
    B, T, _ = qb.shape
    tq = ATT_TQ
    return pl.pallas_call(
        _sb_attn_body,
        grid=(B, T // tq),
        in_specs=[pl.BlockSpec((1, tq, BRANCH_W), lambda b, i: (b, i, 0)),
                  pl.BlockSpec((1, T, BRANCH_W), lambda b, i: (b, 0, 0)),
                  pl.BlockSpec((N_HEADS, None, T, LANES), lambda b, i: (0, b, 0, 0)),
                  pl.BlockSpec(u.shape, lambda b, i: (0, 0))],
        out_specs=pl.BlockSpec((1, tq, BRANCH_W), lambda b, i: (b, i, 0)),
        out_shape=jax.ShapeDtypeStruct((B, T, BRANCH_W), f32),
        scratch_shapes=[pltpu.VMEM((N_HEADS, tq, BRANCH_W), MXU_DTYPE), pltpu.VMEM((N_HEADS, tq, LANES), f32),
                        pltpu.VMEM((N_HEADS, tq, LANES), f32)],
        compiler_params=_cparams(("arbitrary", "arbitrary")),
        name="sb_attn",
    )(qb, kb, vh, u)


def _conv_taps(cw_ref, cb_ref, rows):
    y = cb_ref[...]
    for j in range(CONV_W):
        y = y + rows[j] * cw_ref[j:j + 1, :]
    return y * _sigmoid(y)


def _mlstm_body(mqk_ref, mv_ref, mo_ref, sm_ref, cw_ref, cb_ref, bif_ref, g_ref, tril_ref,
                ym_ref, C_ref, n_ref, m_ref, conv_ref, xc_scr, q_scr, k_scr, C_scr, n_scr, m_scr):
    t = pl.program_id(0)
    NB, TB, L = mqk_ref.shape[0], MLSTM_TB, REC_L
    pad = SUBLANES

    @pl.when(t == 0)
    def _():
        xc_scr[:, 0:pad, :] = jnp.zeros((NB, pad, 2 * BRANCH_W), f32)
        C_scr[...] = jnp.zeros(C_scr.shape, f32)
        n_scr[...] = jnp.zeros(n_scr.shape, f32)
        m_scr[...] = jnp.zeros(m_scr.shape, f32)

    for b in range(NB):
        xc_scr[b, pad:pad + TB, :] = mqk_ref[b]
        act = _conv_taps(cw_ref, cb_ref, [xc_scr[b, pad - (CONV_W - 1) + j:pad - (CONV_W - 1) + j + TB, :]
                                          for j in range(CONV_W)])
        q_scr[b] = act[:, :BRANCH_W]
        k_scr[b] = act[:, BRANCH_W:] * HEAD_DIM ** -0.5
        conv_ref[b] = xc_scr[b, pad + TB - (CONV_W - 1):pad + TB, :]
        xc_scr[b, 0:pad, :] = xc_scr[b, TB:TB + pad, :]

    tril = tril_ref[...]
    tri_mask = lax.broadcasted_iota(jnp.int32, (L, L), 1) <= lax.broadcasted_iota(jnp.int32, (L, L), 0)
    chains = [(b, h) for b in range(NB) for h in range(N_HEADS)]
    hsl = lambda h: slice(h * HEAD_DIM, (h + 1) * HEAD_DIM)
    for c in range(TB // L):
        rs = slice(c * L, (c + 1) * L)
        gi = [sm_ref[b, rs, :] + bif_ref[...] for b in range(NB)]
        bcum = [_mm_sel(tril, _log_sigmoid(g), 3) for g in gi]
        bT = [x.T for x in bcum]
        iT = [g.T for g in gi]
        kT = [k_scr[b, rs, :].T for b in range(NB)]
        qh = {(b, h): q_scr[b, rs, hsl(h)] for b, h in chains}
        kh = {(b, h): k_scr[b, rs, hsl(h)] for b, h in chains}
        vh = {(b, h): mv_ref[b, rs, hsl(h)] for b, h in chains}
        qk = {ch: _mm_nt(qh[ch], kh[ch]) for ch in chains}
        qC = {(b, h): _mm(qh[b, h], C_scr[b, h]) for b, h in chains}
        bcol = {(b, h): bcum[b][:, SM_F + h:SM_F + h + 1] for b, h in chains}
        brow = {(b, h): bT[b][SM_F + h:SM_F + h + 1, :] for b, h in chains}
        irow = {(b, h): iT[b][SM_I + h:SM_I + h + 1, :] for b, h in chains}
        icol = {(b, h): gi[b][:, SM_I + h:SM_I + h + 1] for b, h in chains}
        m_old = {(b, h): m_scr[b, h] for b, h in chains}
        dmat = {ch: jnp.where(tri_mask, bcol[ch] - brow[ch] + irow[ch], -jnp.inf) for ch in chains}
        inter = {ch: bcol[ch] + m_old[ch] for ch in chains}
        mt = {ch: jnp.maximum(inter[ch], jnp.max(dmat[ch], axis=1, keepdims=True)) for ch in chains}
        w = {ch: qk[ch] * jnp.exp(dmat[ch] - mt[ch]) for ch in chains}
        wv = {ch: _mm(w[ch], vh[ch]) for ch in chains}
        bl = {ch: bcol[ch][L - 1:L, :] for ch in chains}
        ws = {ch: bl[ch] - bcol[ch] + icol[ch] for ch in chains}
        m_new = {ch: jnp.maximum(bl[ch] + m_old[ch], jnp.max(ws[ch], axis=0, keepdims=True)) for ch in chains}
        sc = {ch: jnp.exp(bl[ch] + m_old[ch] - m_new[ch]) for ch in chains}
        ek = {ch: kh[ch] * jnp.exp(ws[ch] - m_new[ch]) for ch in chains}
        ekT = {(b, h): kT[b][hsl(h), :] * jnp.exp(bl[b, h] - brow[b, h] + irow[b, h] - m_new[b, h]) for b, h in chains}
        kv = {ch: _mm(ekT[ch], vh[ch]) for ch in chains}
        outs = {}
        for b, h in chains:
            si = jnp.exp(inter[b, h] - mt[b, h])
            num = si * qC[b, h] + wv[b, h]
            den = (si * jnp.sum(qh[b, h] * n_scr[b, h], axis=1, keepdims=True)
                   + jnp.sum(w[b, h], axis=1, keepdims=True))
            outs[b, h] = num / jnp.maximum(jnp.abs(den), jnp.exp(-mt[b, h]))
        for b, h in chains:
            C_scr[b, h] = sc[b, h] * C_scr[b, h] + kv[b, h]
            n_scr[b, h] = sc[b, h] * n_scr[b, h] + jnp.sum(ek[b, h], axis=0, keepdims=True)
            m_scr[b, h] = m_new[b, h]
        for b in range(NB):
            ym_ref[b, rs, :] = (_head_norm(jnp.concatenate([outs[b, h] for h in range(N_HEADS)], axis=1), g_ref)
                                * _sigmoid(mo_ref[b, rs, :]))

    C_ref[...] = C_scr[...]
    for b in range(NB):
        n_ref[b] = jnp.concatenate([n_scr[b, h] for h in range(N_HEADS)], axis=1)
        m_ref[b] = jnp.concatenate([jnp.broadcast_to(m_scr[b, h], (1, LANES // N_HEADS)) for h in range(N_HEADS)],
                                   axis=1)


def mlstm_prompt(mqk, mv, mo, small, conv_w, conv_b, bif_row, norm_g, tril):
    B, T, _ = mqk.shape
    TB = MLSTM_TB
    blk = lambda n: pl.BlockSpec((B, TB, n), lambda t: (0, t, 0))
    const = lambda s: pl.BlockSpec(s, lambda t: (0,) * len(s))
    return pl.pallas_call(
        _mlstm_body,
        grid=(T // TB,),
        in_specs=[blk(2 * BRANCH_W), blk(BRANCH_W), blk(BRANCH_W), blk(LANES), const((CONV_W, 2 * BRANCH_W)),
                  const((1, 2 * BRANCH_W)), const((1, LANES)), const((1, BRANCH_W)), const((REC_L, REC_L))],
        out_specs=[blk(BRANCH_W), const((B, N_HEADS, HEAD_DIM, HEAD_DIM)), const((B, 1, BRANCH_W)),
                   const((B, 1, LANES)), const((B, CONV_W - 1, 2 * BRANCH_W))],
        out_shape=[jax.ShapeDtypeStruct((B, T, BRANCH_W), f32),
                   jax.ShapeDtypeStruct((B, N_HEADS, HEAD_DIM, HEAD_DIM), f32),
                   jax.ShapeDtypeStruct((B, 1, BRANCH_W), f32),
                   jax.ShapeDtypeStruct((B, 1, LANES), f32),
                   jax.ShapeDtypeStruct((B, CONV_W - 1, 2 * BRANCH_W), f32)],
        scratch_shapes=[pltpu.VMEM((B, TB + SUBLANES, 2 * BRANCH_W), f32), pltpu.VMEM((B, TB, BRANCH_W), f32),
                        pltpu.VMEM((B, TB, BRANCH_W), f32), pltpu.VMEM((B, N_HEADS, HEAD_DIM, HEAD_DIM), f32),
                        pltpu.VMEM((B, N_HEADS, 1, HEAD_DIM), f32), pltpu.VMEM((B, N_HEADS, 1, 1), f32)],
        compiler_params=_cparams(("arbitrary",)),
        name="mlstm_prompt",
    )(mqk, mv, mo, small, conv_w, conv_b.reshape(1, -1), bif_row, norm_g.reshape(1, BRANCH_W), tril)


def _gla_log_decay(sm, wa_ref, ba_ref):
    return _log_sigmoid(_mm(sm, wa_ref[...]) + ba_ref[...]) * (1.0 / GLA_TAU)


def _gla_body(gqk_ref, gv_ref, sm_ref, wa_ref, ba_ref, g_ref, lb_ref, ex_ref, yg_ref, S_ref, S_scr):
    t = pl.program_id(1)
    TB, L, R = REC_TB, REC_L, GLA_R
    NK = N_HEADS * GLA_DK

    @pl.when(t == 0)
    def _():
        S_scr[...] = jnp.zeros(S_scr.shape, f32)

    lb = lb_ref[...]
    ex = ex_ref[...]
    s_i = lax.broadcasted_iota(jnp.int32, (R, R, NK), 0)
    t_i = lax.broadcasted_iota(jnp.int32, (R, R, NK), 1)
    causal3 = t_i >= s_i
    lane_t = lax.broadcasted_iota(jnp.int32, (1, L), 1)
    bd_mask = (lax.broadcasted_iota(jnp.int32, (NK, BRANCH_W), 0) // GLA_DK
               == lax.broadcasted_iota(jnp.int32, (NK, BRANCH_W), 1) // HEAD_DIM)
    for c in range(TB // L):
        rs = slice(c * L, (c + 1) * L)
        la = _gla_log_decay(sm_ref[0, rs, :], wa_ref, ba_ref)
        bcl = _mm_sel(lb, la, 3)
        q_c = gqk_ref[0, rs, 0:NK] * GLA_DK ** -0.5
        k_c = gqk_ref[0, rs, NK:2 * NK]
        v_c = gv_ref[0, rs, :]
        bll = jnp.concatenate([jnp.broadcast_to(bcl[(I + 1) * R - 1:(I + 1) * R, :], (R, NK)) for I in range(L // R)],
                              axis=0)
        ktilT = (k_c * jnp.exp(bll - bcl)).T
        dblT = jnp.exp(bll).T
        qin = q_c * jnp.exp(bcl)
        outs = []
        for I in range(L // R):
            sl = slice(I * R, (I + 1) * R)
            S = S_scr[...]
            bb, vv = bcl[sl], v_c[sl]
            d = jnp.where(causal3, bb[None, :, :] - bb[:, None, :], -jnp.inf)
            p = q_c[sl][None, :, :] * k_c[sl][:, None, :] * jnp.exp(d)
            a = _mm(p.reshape(R * R, NK), ex).reshape(R, R, BRANCH_W)
            outs.append(_mm(qin[sl], S) + jnp.sum(a * vv[:, None, :], axis=0))
            in_blk = (lane_t >= I * R) & (lane_t < (I + 1) * R)
            upd = _mm(jnp.where(in_blk, ktilT, 0.0), v_c)
            S_scr[...] = dblT[:, I * R:I * R + 1] * S + jnp.where(bd_mask, upd, 0.0)
        yg_ref[0, rs, :] = _head_norm(jnp.concatenate(outs, axis=0), g_ref)

    for h in range(N_HEADS):
        S_ref[0, h] = S_scr[h * GLA_DK:(h + 1) * GLA_DK, h * HEAD_DIM:(h + 1) * HEAD_DIM]


def gla_prompt(gqk, gv, small, wa, ba, norm_g, lb, ex):
    B, T, _ = gqk.shape
    TB = REC_TB
    blk = lambda n: pl.BlockSpec((1, TB, n), lambda b, t: (b, t, 0))
    const = lambda s: pl.BlockSpec(s, lambda b, t: (0,) * len(s))
    return pl.pallas_call(
        _gla_body,
        grid=(B, T // TB),
        in_specs=[blk(BRANCH_W), blk(BRANCH_W), blk(LANES), const((LANES, LANES)), const((1, LANES)),
                  const((1, BRANCH_W)), const((REC_L, REC_L)), const((LANES, BRANCH_W))],
        out_specs=[blk(BRANCH_W), pl.BlockSpec((1, N_HEADS, GLA_DK, HEAD_DIM), lambda b, t: (b, 0, 0, 0))],
        out_shape=[jax.ShapeDtypeStruct((B, T, BRANCH_W), f32),
                   jax.ShapeDtypeStruct((B, N_HEADS, GLA_DK, HEAD_DIM), f32)],
        scratch_shapes=[pltpu.VMEM((N_HEADS * GLA_DK, BRANCH_W), f32)],
        compiler_params=_cparams(("arbitrary", "arbitrary")),
        name="gla_prompt",
    )(gqk, gv, small, wa, ba.reshape(1, LANES), norm_g.reshape(1, BRANCH_W), lb, ex)


def _lane_replicated_column(row):
    n = row.shape[1]
    eye = lax.broadcasted_iota(jnp.int32, (n, n), 0) == lax.broadcasted_iota(jnp.int32, (n, n), 1)
    col = jnp.sum(jnp.where(eye, jnp.broadcast_to(row, (n, n)), 0.0), axis=1, keepdims=True)
    return jnp.broadcast_to(col, (n, LANES))


def _rows_per_head(x4):
    return jnp.broadcast_to(x4[:, None, :], (N_HEADS, HEAD_DIM, LANES)).reshape(BRANCH_W, LANES)


def _lanes_per_head(col4):
    sel = (lax.broadcasted_iota(jnp.int32, (N_HEADS, BRANCH_W), 1) // HEAD_DIM
           == lax.broadcasted_iota(jnp.int32, (N_HEADS, BRANCH_W), 0))
    return jnp.sum(jnp.where(sel, col4, 0.0), axis=0, keepdims=True)


def _dec_attn_body(lam_init, n_pages, pt_ref, dq_ref, dkn_ref, dvn_ref, sq_ref, dl_ref, g_ref, u_ref, *rest):
    dk_pg, dv_pg, sk_pg, sv_pg = (rest[j * n_pages:(j + 1) * n_pages] for j in range(4))
    ya_ref, ys_ref = rest[4 * n_pages:]
    lam = _diff_lambda(dl_ref, lam_init)

    q_row = dq_ref[0].astype(f32) * LN2
    qx = _lane_replicated_column(q_row)
    s1, s2 = [], []
    for p in range(n_pages):
        r = jnp.sum((dk_pg[p][...] * qx).reshape(N_HEADS, 2, DIFF_D, LANES), axis=2)
        s1.append(r[:, 0, :])
        s2.append(r[:, 1, :])
    lane = lax.broadcasted_iota(jnp.int32, (N_HEADS, BRANCH_W), 1)
    head = lax.broadcasted_iota(jnp.int32, (N_HEADS, BRANCH_W), 0)
    qk_new = jnp.broadcast_to(q_row * dkn_ref[0], (N_HEADS, BRANCH_W))
    w_pages, w_new = [], []
    for m, s in enumerate((s1, s2)):
        lo = head * HEAD_DIM + m * DIFF_D
        s_new = jnp.sum(jnp.where((lane >= lo) & (lane < lo + DIFF_D), qk_new, 0.0), axis=1, keepdims=True)
        smax = s[0]
        for p in range(1, n_pages):
            smax = jnp.maximum(smax, s[p])
        mx = jnp.maximum(jnp.max(smax, axis=1, keepdims=True), s_new)
        e = [jnp.exp(sp - mx) for sp in s]
        e_new = jnp.exp(s_new - mx)
        tot = e[0]
        for p in range(1, n_pages):
            tot = tot + e[p]
        inv = 1.0 / (jnp.sum(tot, axis=1, keepdims=True) + e_new)
        w_pages.append([ep * inv for ep in e])
        w_new.append(e_new * inv)
    acc = jnp.zeros((BRANCH_W, LANES), f32)
    for p in range(n_pages):
        acc = acc + _rows_per_head(w_pages[0][p] - lam * w_pages[1][p]) * dv_pg[p][...]
    y = jnp.sum(acc.T, axis=0, keepdims=True) + _lanes_per_head(w_new[0] - lam * w_new[1]) * dvn_ref[0]
    ya_ref[0] = _head_norm(y, g_ref) * (1.0 - lam_init)

    qx = _lane_replicated_column(sq_ref[0].astype(f32) * LN2)
    z = jnp.concatenate([jnp.sum((sk_pg[p][...] * qx).reshape(N_HEADS, HEAD_DIM, LANES), axis=1)
                         for p in range(n_pages)], axis=0)
    sp = _softplus_neg_abs(z)
    ls = jnp.minimum(z, 0.0) - sp
    r = _sel_mm(-jnp.maximum(z, 0.0) - sp, u_ref[...], 2)
    base = ls + r[:, :LANES]
    carry = jnp.zeros((N_HEADS, LANES), f32)
    acc = jnp.zeros((BRANCH_W, LANES), f32)
    for p in reversed(range(n_pages)):
        rows = slice(p * N_HEADS, (p + 1) * N_HEADS)
        acc = acc + _rows_per_head(jnp.exp(base[rows] + carry)) * sv_pg[p][...]
        carry = carry + r[rows, LANES:]
    ys_ref[0] = jnp.sum(acc.T, axis=0, keepdims=True)


def _pages_as_feature_by_position(cache):
    d, n, pg, h, hd = cache.shape
    return jnp.transpose(cache, (0, 1, 3, 4, 2)).reshape(d, n, h * hd, pg)


def dec_attn(layer, lam_init, page_table, dqb, dk_new, dv_new, sqb, diff_lambda, norm_g, u, caches_t):
    B, n_pages = page_table.shape
    page = caches_t[0].shape[3]
    assert page == LANES
    row3 = lambda a: a.reshape(B, 1, BRANCH_W)
    rspec = pl.BlockSpec((1, 1, BRANCH_W), lambda b, pt: (b, 0, 0))
    const = lambda shp: pl.BlockSpec(shp, lambda b, pt: (0,) * len(shp))

    def pg(p):
        return pl.BlockSpec((None, None, BRANCH_W, page), lambda b, pt: (layer, pt[b * n_pages + p], 0, 0))

    in_specs = [rspec, rspec, rspec, rspec, const((4, DIFF_D)), const((1, BRANCH_W)), const((LANES, 2 * LANES))]
    args = [row3(dqb), row3(dk_new), row3(dv_new), row3(sqb), diff_lambda, norm_g.reshape(1, BRANCH_W), u]
    for c in caches_t:
        for p in range(n_pages):
            in_specs.append(pg(p))
            args.append(c)
    ya, ys = pl.pallas_call(
        functools.partial(_dec_attn_body, lam_init, n_pages),
        grid_spec=pltpu.PrefetchScalarGridSpec(
            num_scalar_prefetch=1, grid=(B,), in_specs=in_specs,
            out_specs=[pl.BlockSpec((1, 1, BRANCH_W), lambda b, pt: (b, 0, 0))] * 2),
        out_shape=[jax.ShapeDtypeStruct((B, 1, BRANCH_W), f32)] * 2,
        compiler_params=_cparams(("arbitrary",)),
        name="dec_attn",
    )(page_table.reshape(-1), *args)
    return ya.reshape(B, BRANCH_W), ys.reshape(B, BRANCH_W)


def _expand_heads(x4, width):
    return jnp.concatenate([jnp.broadcast_to(x4[:, h:h + 1], (x4.shape[0], width)) for h in range(N_HEADS)], axis=1)


def _dec_rec_body(mqk_ref, mv_ref, mo_ref, sm_ref, gqk_ref, gv_ref, conv_ref, C_ref, n_ref, m_ref, S_ref,
                  cw_ref, cb_ref, bif_ref, mg_ref, wa_ref, ba_ref, gg_ref,
                  ym_ref, yg_ref, conv_o, C_o, n_o, m_o, S_o, CT_scr, ST_scr, numT_scr, oT_scr):
    h = pl.program_id(0)
    W = 2 * BRANCH_W
    NK = N_HEADS * GLA_DK
    u = mqk_ref[...]
    rows = [conv_ref[:, j * W:(j + 1) * W] for j in range(CONV_W - 1)] + [u]
    act = _conv_taps(cw_ref, cb_ref, rows)
    q = act[:, :BRANCH_W]
    k = act[:, BRANCH_W:] * HEAD_DIM ** -0.5
    conv_o[...] = jnp.concatenate(rows[1:], axis=1)

    gi = sm_ref[...] + bif_ref[...]
    ig = gi[:, SM_I:SM_I + N_HEADS]
    lf = _log_sigmoid(gi)[:, SM_F:SM_F + N_HEADS]
    m_old = m_ref[...]
    m_new = jnp.maximum(lf + m_old, ig)
    sc = _expand_heads(jnp.exp(lf + m_old - m_new), HEAD_DIM)
    ek = _expand_heads(jnp.exp(ig - m_new), HEAD_DIM) * k
    n_new = sc * n_ref[...] + ek
    m_o[...] = m_new
    n_o[...] = n_new

    la = _gla_log_decay(sm_ref[...], wa_ref, ba_ref)
    gq = gqk_ref[:, 0:NK] * GLA_DK ** -0.5
    gk = gqk_ref[:, NK:2 * NK]

    r64 = pl.ds(pl.multiple_of(h * HEAD_DIM, HEAD_DIM), HEAD_DIM)
    r32 = pl.ds(pl.multiple_of(h * GLA_DK, GLA_DK), GLA_DK)
    numT_scr[0] = q.T
    numT_scr[1] = ek.T
    numT_scr[2] = mv_ref[...].T
    numT_scr[3] = sc.T
    qT, ekT, vT = numT_scr[0, r64, :], numT_scr[1, r64, :], numT_scr[2, r64, :]
    scT = numT_scr[3, pl.ds(h * HEAD_DIM, 1), :]
    CT_scr[...] = C_ref[...].T
    num = jnp.zeros((HEAD_DIM, LANES), f32)
    for kk in range(HEAD_DIM):
        blk = slice(kk * HEAD_DIM, (kk + 1) * HEAD_DIM)
        new = scT * CT_scr[blk, :] + ekT[kk:kk + 1, :] * vT
        CT_scr[blk, :] = new
        num = num + qT[kk:kk + 1, :] * new
    C_o[...] = CT_scr[...].T
    numT_scr[4, r64, :] = num

    oT_scr[0, 0:NK, :] = gq.T
    oT_scr[0, NK:2 * NK, :] = gk.T
    oT_scr[1, 0:NK, :] = jnp.exp(la).T
    oT_scr[2] = gv_ref[...].T
    gqT, gkT, decT = oT_scr[0, r32, :], oT_scr[0, pl.ds(pl.multiple_of(NK + h * GLA_DK, GLA_DK), GLA_DK), :], \
        oT_scr[1, r32, :]
    gvT = oT_scr[2, r64, :]
    ST_scr[...] = S_ref[...].T
    o = jnp.zeros((HEAD_DIM, LANES), f32)
    for kk in range(GLA_DK):
        blk = slice(kk * HEAD_DIM, (kk + 1) * HEAD_DIM)
        new = decT[kk:kk + 1, :] * ST_scr[blk, :] + gkT[kk:kk + 1, :] * gvT
        ST_scr[blk, :] = new
        o = o + gqT[kk:kk + 1, :] * new
    S_o[...] = ST_scr[...].T
    oT_scr[3, r64, :] = o

    @pl.when(h == N_HEADS - 1)
    def _():
        qn = jnp.concatenate([jnp.sum((q * n_new)[:, g * HEAD_DIM:(g + 1) * HEAD_DIM], axis=1, keepdims=True)
                              for g in range(N_HEADS)], axis=1)
        den = jnp.maximum(jnp.abs(qn), jnp.exp(-m_new))
        hm = numT_scr[4].T / _expand_heads(den, HEAD_DIM)
        ym_ref[...] = _head_norm(hm, mg_ref) * _sigmoid(mo_ref[...])
        yg_ref[...] = _head_norm(oT_scr[3].T, gg_ref)


def dec_rec(mqk, mv, mo, small, gqk, gv, conv, C, n, m, S, conv_w, conv_b, bif_row, mnorm_g, wa, ba, gnorm_g):
    B = mqk.shape[0]
    CW = HEAD_DIM * HEAD_DIM
    SW = GLA_DK * HEAD_DIM
    full = lambda a: pl.BlockSpec(a.shape, lambda h: (0,) * a.ndim)
    ins = [mqk, mv, mo, small, gqk, gv, conv.reshape(B, -1), C.reshape(B, N_HEADS * CW), n.reshape(B, BRANCH_W), m,
           S.reshape(B, N_HEADS * SW), conv_w, conv_b.reshape(1, -1), bif_row, mnorm_g.reshape(1, BRANCH_W), wa,
           ba.reshape(1, LANES), gnorm_g.reshape(1, BRANCH_W)]
    in_specs = [full(a) for a in ins]
    in_specs[7] = pl.BlockSpec((B, CW), lambda h: (0, h))
    in_specs[10] = pl.BlockSpec((B, SW), lambda h: (0, h))
    out_shape = [jax.ShapeDtypeStruct((B, BRANCH_W), f32), jax.ShapeDtypeStruct((B, BRANCH_W), f32),
                 jax.ShapeDtypeStruct((B, (CONV_W - 1) * 2 * BRANCH_W), f32),
                 jax.ShapeDtypeStruct((B, N_HEADS * CW), f32), jax.ShapeDtypeStruct((B, BRANCH_W), f32),
                 jax.ShapeDtypeStruct((B, N_HEADS), f32), jax.ShapeDtypeStruct((B, N_HEADS * SW), f32)]
    out_specs = [pl.BlockSpec(s.shape, lambda h: (0, 0)) for s in out_shape]
    out_specs[3] = pl.BlockSpec((B, CW), lambda h: (0, h))
    out_specs[6] = pl.BlockSpec((B, SW), lambda h: (0, h))
    ym, yg, conv_n, C_n, n_n, m_n, S_n = pl.pallas_call(
        _dec_rec_body,
        grid=(N_HEADS,),
        in_specs=in_specs, out_specs=out_specs, out_shape=out_shape,
        scratch_shapes=[pltpu.VMEM((CW, B), f32), pltpu.VMEM((SW, B), f32), pltpu.VMEM((5, BRANCH_W, B), f32),
                        pltpu.VMEM((4, BRANCH_W, B), f32)],
        compiler_params=_cparams(("arbitrary",)),
        name="dec_rec",
    )(*ins)
    return (ym, yg, conv_n.reshape(B, CONV_W - 1, 2 * BRANCH_W), C_n.reshape(B, N_HEADS, HEAD_DIM, HEAD_DIM),
            n_n.reshape(B, N_HEADS, HEAD_DIM), m_n, S_n.reshape(B, N_HEADS, GLA_DK, HEAD_DIM))


def _constants():
    t = np.arange(REC_L)
    tril = (t[None, :] <= t[:, None]).astype(np.float32)
    lb = tril * (t[None, :] // GLA_R == t[:, None] // GLA_R)
    ex = (np.arange(N_HEADS * GLA_DK)[:, None] // GLA_DK == np.arange(BRANCH_W)[None, :] // HEAD_DIM)
    cast = lambda a: jnp.asarray(a, f32).astype(MXU_DTYPE)
    return dict(tril=cast(tril), lb=cast(lb), ex=cast(ex), u_page=cast(_suffix_matrix(LANES)),
                u_sb=cast(_suffix_matrix(SB_TK)[:, :SB_TK]))


def _layer_params(l, prm):
    bif = prm['mlstm_b_if'][l].astype(f32).reshape(1, 2 * N_HEADS)
    wa = jnp.zeros((LANES, LANES), f32).at[SM_A:SM_A + GLA_LOWRANK].set(prm['gla_w_a2'][l])
    return dict(
        lam_init=0.8 - 0.6 * math.exp(-0.3 * l),
        bif_row=jnp.pad(bif, ((0, 0), (0, LANES - 2 * N_HEADS))),
        wa=wa.astype(MXU_DTYPE),
    )


def _prep_weights(w_in, w_gate, w_branch, w_out, w_ple, w_ple_gate):
    c = lambda a: a.astype(MXU_DTYPE)
    o = _ORIG
    parts = [w_in[:, :, :o['mi']], w_in[:, :, o['mo']:o['ga']], w_in[:, :, o['sq']:], w_in[:, :, o['mi']:o['mo']],
             w_in[:, :, o['ga']:o['sq']]]
    used = sum(p.shape[2] for p in parts)
    w_perm = jnp.concatenate([c(p) for p in parts]
                             + [jnp.zeros(w_in.shape[:2] + (N_PROJ - used,), MXU_DTYPE)], axis=2)
    return w_perm, c(w_gate), c(w_branch), c(w_out), c(w_ple), c(w_ple_gate)


def _prompt_layer(l, x, p_l, prm, wts, consts):
    B, T, _ = x.shape
    lp = _layer_params(l, prm)
    w_perm, w_gate, w_branch, w_out, w_ple, w_ple_gate = wts
    x2 = x.reshape(B * T, D_MODEL)
    (dqb, dk, dkb, dv, dvh, mqk, mv, mo, gqk, gv, sqb, sk, skb, sv, svh, z, small) = in_proj(
        x2, prm['g_pre'][l], w_perm, l, seq_len=T)
    r3 = lambda a: a.reshape(B, T, a.shape[-1])
    hm = lambda a: a.reshape(N_HEADS, B, T, LANES)
    ya = diff_attn(lp['lam_init'], r3(dqb), r3(dkb), hm(dvh), prm['diff_lambda'][l], prm['diff_norm_g'][l])
    ys = sb_attn(r3(sqb), r3(skb), hm(svh), consts['u_sb'])
    ym, C1, n1, m1, conv1 = mlstm_prompt(r3(mqk), r3(mv), r3(mo), r3(small), prm['mlstm_conv_w'][l],
                                         prm['mlstm_conv_b'][l], lp['bif_row'], prm['mlstm_norm_g'][l],
                                         consts['tril'])
    yg, S1 = gla_prompt(r3(gqk), r3(gv), r3(small), lp['wa'], prm['gla_b_a'][l], prm['gla_norm_g'][l],
                        consts['lb'], consts['ex'])
    flat = lambda a: a.reshape(B * T, BRANCH_W)
    y = out_proj(x2, p_l.reshape(B * T, P_DIM), flat(ya), flat(ym), flat(yg), flat(ys), z, prm['g_pre'][l],
                 prm['g_post'][l], l, w_gate, w_branch, w_out, w_ple, w_ple_gate)
    hd = lambda a: jnp.transpose(a.reshape(B, N_HEADS, HEAD_DIM, T), (0, 3, 1, 2))
    state = (hd(dk), hd(dv), hd(sk), hd(sv), C1, n1.reshape(B, N_HEADS, HEAD_DIM),
             m1[:, 0, ::LANES // N_HEADS], conv1, S1)
    return y.reshape(B, T, D_MODEL), state


def _decode_layer(l, x, p_l, prm, wts, consts, caches, states, page_table):
    B = x.shape[0]
    lp = _layer_params(l, prm)
    w_perm, w_gate, w_branch, w_out, w_ple, w_ple_gate = wts
    x2 = x.reshape(B, D_MODEL)
    (dqb, dk, dkb, dv, dvh, mqk, mv, mo, gqk, gv, sqb, sk, skb, sv, svh, z, small) = in_proj(
        x2, prm['g_pre'][l], w_perm, l)
    ya, ys = dec_attn(l, lp['lam_init'], page_table, dqb, dk, dv, sqb, prm['diff_lambda'][l], prm['diff_norm_g'][l],
                      consts['u_page'], caches)
    C0, n0, m0, conv0, S0 = states
    ym, yg, conv1, C1, n1, m1, S1 = dec_rec(mqk, mv, mo, small, gqk, gv, conv0[l], C0[l], n0[l], m0[l], S0[l],
                                            prm['mlstm_conv_w'][l], prm['mlstm_conv_b'][l], lp['bif_row'],
                                            prm['mlstm_norm_g'][l], lp['wa'], prm['gla_b_a'][l],
                                            prm['gla_norm_g'][l])
    y = out_proj(x2, p_l.reshape(B, P_DIM), ya, ym, yg, ys, z, prm['g_pre'][l], prm['g_post'][l], l, w_gate,
                 w_branch, w_out, w_ple, w_ple_gate)
    hd = lambda a: a.reshape(B, 1, N_HEADS, HEAD_DIM)
    state = (hd(dk), hd(dv), hd(sk), hd(sv), C1, n1, m1, conv1, S1)
    return y.reshape(B, 1, D_MODEL), state


def kernel(x_prompt, x_sample, cache_diff_k, cache_diff_v, cache_sb_k, cache_sb_v, state_mlstm_C, state_mlstm_n,
           state_mlstm_m, state_mlstm_conv, state_gla_S, page_table, p_prompt, p_sample, g_pre, g_post, w_in,
           diff_lambda, diff_norm_g, mlstm_conv_w, mlstm_conv_b, mlstm_b_if, mlstm_norm_g, gla_w_a2, gla_b_a,
           gla_norm_g, w_branch, w_gate, w_out, w_ple, w_ple_gate):
    prm = dict(g_pre=g_pre, g_post=g_post, diff_lambda=diff_lambda, diff_norm_g=diff_norm_g,
               mlstm_conv_w=mlstm_conv_w, mlstm_conv_b=mlstm_conv_b, mlstm_b_if=mlstm_b_if,
               mlstm_norm_g=mlstm_norm_g, gla_w_a2=gla_w_a2, gla_b_a=gla_b_a, gla_norm_g=gla_norm_g)
    wts = _prep_weights(w_in, w_gate, w_branch, w_out, w_ple, w_ple_gate)
    consts = _constants()
    depth = w_in.shape[0]
    caches = tuple(_pages_as_feature_by_position(c) for c in (cache_diff_k, cache_diff_v, cache_sb_k, cache_sb_v))
    states = (state_mlstm_C, state_mlstm_n, state_mlstm_m, state_mlstm_conv, state_gla_S)
    y_p, y_s = x_prompt, x_sample
    st_p, st_s = [], []
    for l in range(depth):
        y_p, s_p = _prompt_layer(l, y_p, p_prompt[l], prm, wts, consts)
        y_s, s_s = _decode_layer(l, y_s, p_sample[l], prm, wts, consts, caches, states, page_table)
        st_p.append(s_p)
        st_s.append(s_s)
    outs_p = [jnp.stack(t) for t in zip(*st_p)]
    outs_s = [jnp.stack(t) for t in zip(*st_s)]
    return (y_p, y_s, *outs_p, *outs_s)
```
